```python
import jax, jax.numpy as jnp
from jax import lax
import numpy as np

D_MODEL = 4096
BATCH = 1
SEQ = 8192
DEPTH = 4

CTX_LEN = 256
GRID_W = 64
HEAD_DIM = 128
ROPE_THETA = 10000.0
NORM_EPS = 1e-6
NEG_INF = -1e30
BLOCK = 128
D_FF = 5120
COND_RANK = 1024
N_MOD = 9

A_HEADS = 8
A_KV_HEADS = 2
A_WINDOW = 128
B_HEADS = 8
B_KV_HEADS = 2
C_HEADS = 8
C_Q_RANK = 768
C_KV_RANK = 512
C_NOPE_DIM = 128
C_ROPE_DIM = 64
C_V_DIM = 128
D_HEADS = 8
NA_KH = 8
NA_KW = 16

A_COLS = (A_HEADS + 2 * A_KV_HEADS) * HEAD_DIM
B_COLS = (B_HEADS + 2 * B_KV_HEADS) * HEAD_DIM
C_COLS = C_Q_RANK + C_KV_RANK + C_ROPE_DIM
D_COLS = 3 * D_HEADS * HEAD_DIM
IN_COLS = A_COLS + B_COLS + C_COLS + D_COLS
MIX_WIDTH = (A_HEADS + B_HEADS + D_HEADS) * HEAD_DIM + C_HEADS * C_V_DIM

kernel_name = "hymba_style_four_mixer_diffusion_trunk"


def rms_norm(x, gain):
    xf = x.astype(jnp.float32)
    y = xf * lax.rsqrt(jnp.mean(xf * xf, axis=-1, keepdims=True) + NORM_EPS)
    return (y * gain.astype(jnp.float32)).astype(x.dtype)


def softmax_f32(logits):
    return jax.nn.softmax(logits.astype(jnp.float32), axis=-1)


def rope_1d(x, pos):
    d = x.shape[-1]
    inv = ROPE_THETA ** (-jnp.arange(0, d, 2, dtype=jnp.float32) / d)
    ang = pos.astype(jnp.float32)[:, None] * inv[None, :]
    ang = ang.reshape(ang.shape[:1] + (1,) * (x.ndim - 3) + ang.shape[1:])
    cos = jnp.cos(ang).astype(x.dtype)
    sin = jnp.sin(ang).astype(x.dtype)
    x1, x2 = jnp.split(x, 2, axis=-1)
    return jnp.concatenate([x1 * cos - x2 * sin, x2 * cos + x1 * sin], axis=-1)


def rope_2d(x, row, col):
    half = x.shape[-1] // 2
    return jnp.concatenate([rope_1d(x[..., :half], row), rope_1d(x[..., half:], col)], axis=-1)


def split_qkv(t, n_q, n_kv):
    hq, hk = n_q * HEAD_DIM, (n_q + n_kv) * HEAD_DIM
    heads = lambda u, n: u.reshape(u.shape[:-1] + (n, HEAD_DIM))
    return heads(t[..., :hq], n_q), heads(t[..., hq:hk], n_kv), heads(t[..., hk:], n_kv)


def ctx_gqa(qc, kc, vc, scale):
    bsz, n_ctx, n_h, hd = qc.shape
    n_kv = kc.shape[2]
    qg = qc.reshape(bsz, n_ctx, n_kv, n_h // n_kv, hd)
    p = softmax_f32(jnp.einsum('bqkgd,bskd->bkgqs', qg, kc) * scale).astype(vc.dtype)
    return jnp.einsum('bkgqs,bskd->bqkgd', p, vc).reshape(bsz, n_ctx, n_h * vc.shape[-1])


def swiglu(h, w_gu, w_down):
    g, u = jnp.split(h @ w_gu, 2, axis=-1)
    return (jax.nn.silu(g) * u) @ w_down


def modulation(s, w_down, w_up):
    m = (s @ w_down) @ w_up
    return jnp.split(m[:, None, :], N_MOD, axis=-1)


def modulate(x, gain, shift, scale):
    return rms_norm(x, gain) * (1.0 + scale) + shift


def window_sink_attention(t, tc, sink, row, col, need_ctx):
    q, k, v = split_qkv(t, A_HEADS, A_KV_HEADS)
    qc, kc, vc = split_qkv(tc, A_HEADS, A_KV_HEADS)
    bsz, seq, n_h, hd = q.shape
    n_kv, grp, n_ctx = A_KV_HEADS, A_HEADS // A_KV_HEADS, kc.shape[1]
    n_blk = seq // BLOCK
    span = BLOCK + 2 * A_WINDOW
    scale = hd ** -0.5
    q = rope_2d(q, row, col)
    k = rope_2d(k, row, col)
    pad = ((0, 0), (A_WINDOW, A_WINDOW), (0, 0), (0, 0))
    kp, vp = jnp.pad(k, pad), jnp.pad(v, pad)
    kidx = jnp.arange(n_blk)[:, None] * BLOCK + jnp.arange(span)[None, :]
    kb, vb = kp[:, kidx], vp[:, kidx]
    qb = q.reshape(bsz, n_blk, BLOCK, n_kv, grp, hd)
    qpos = jnp.arange(n_blk)[:, None] * BLOCK + jnp.arange(BLOCK)[None, :]
    kpos = kidx - A_WINDOW
    valid = ((jnp.abs(qpos[:, :, None] - kpos[:, None, :]) <= A_WINDOW)
             & (kpos >= 0)[:, None, :] & (kpos < seq)[:, None, :])
    s_win = jnp.einsum('bnqkgd,bnskd->bnkgqs', qb, kb).astype(jnp.float32) * scale
    s_win = jnp.where(valid[None, :, None, None], s_win, NEG_INF)
    s_ctx = jnp.einsum('bnqkgd,bskd->bnkgqs', qb, kc).astype(jnp.float32) * scale
    s_sink = jnp.broadcast_to(sink.astype(jnp.float32).reshape(1, 1, n_kv, grp, 1, 1), s_win.shape[:-1] + (1,))
    p = softmax_f32(jnp.concatenate([s_win, s_ctx, s_sink], axis=-1)).astype(v.dtype)
    out = (jnp.einsum('bnkgqs,bnskd->bnqkgd', p[..., :span], vb)
           + jnp.einsum('bnkgqs,bskd->bnqkgd', p[..., span:span + n_ctx], vc))
    out = out.reshape(bsz, seq, n_h * hd)
    out_c = None
    if need_ctx:
        qcg = qc.reshape(bsz, n_ctx, n_kv, grp, hd)
        s_cc = jnp.einsum('bqkgd,bskd->bkgqs', qcg, kc).astype(jnp.float32) * scale
        s_cs = jnp.broadcast_to(sink.astype(jnp.float32).reshape(1, n_kv, grp, 1, 1), s_cc.shape[:-1] + (1,))
        pc = softmax_f32(jnp.concatenate([s_cc, s_cs], axis=-1)).astype(vc.dtype)
        out_c = jnp.einsum('bkgqs,bskd->bqkgd', pc[..., :n_ctx], vc).reshape(bsz, n_ctx, n_h * hd)
    return out, out_c


def dense_qknorm_attention(t, tc, g_q, g_k, row, col, need_ctx):
    q, k, v = split_qkv(t, B_HEADS, B_KV_HEADS)
    qc, kc, vc = split_qkv(tc, B_HEADS, B_KV_HEADS)
    bsz, seq, n_h, hd = q.shape
    n_kv, grp = B_KV_HEADS, B_HEADS // B_KV_HEADS
    n_blk = seq // BLOCK
    scale = hd ** -0.5
    q = rope_2d(rms_norm(q, g_q), row, col)
    k = rope_2d(rms_norm(k, g_k), row, col)
    qc, kc = rms_norm(qc, g_q), rms_norm(kc, g_k)
    k_all = jnp.concatenate([kc, k], axis=1)
    v_all = jnp.concatenate([vc, v], axis=1)
    qb = jnp.moveaxis(q.reshape(bsz, n_blk, BLOCK, n_kv, grp, hd), 1, 0)

    def block(qi):
        p = softmax_f32(jnp.einsum('bqkgd,bskd->bkgqs', qi, k_all) * scale).astype(v_all.dtype)
        return jnp.einsum('bkgqs,bskd->bqkgd', p, v_all)

    out = jnp.moveaxis(lax.map(block, qb), 0, 1).reshape(bsz, seq, n_h * hd)
    out_c = ctx_gqa(qc, kc, vc, scale) if need_ctx else None
    return out, out_c


def latent_attention(t, tc, g_q, g_kv, w_q_up, w_kv_up, row, col, need_ctx):
    def expand(u, rotate):
        lead = u.shape[:2]
        q_a = u[..., :C_Q_RANK]
        kv_a = u[..., C_Q_RANK:C_Q_RANK + C_KV_RANK]
        k_pe = u[..., C_Q_RANK + C_KV_RANK:]
        q = (rms_norm(q_a, g_q) @ w_q_up).reshape(lead + (C_HEADS, C_NOPE_DIM + C_ROPE_DIM))
        kv = (rms_norm(kv_a, g_kv) @ w_kv_up).reshape(lead + (C_HEADS, C_NOPE_DIM + C_V_DIM))
        q_nope, q_pe = q[..., :C_NOPE_DIM], q[..., C_NOPE_DIM:]
        k_nope, v = kv[..., :C_NOPE_DIM], kv[..., C_NOPE_DIM:]
        if rotate:
            q_pe = rope_2d(q_pe, row, col)
            k_pe = rope_2d(k_pe, row, col)
        return q_nope, q_pe, k_nope, k_pe, v

    qn, qp, kn, kp, v = expand(t, True)
    qnc, qpc, knc, kpc, vc = expand(tc, False)
    bsz, seq = t.shape[:2]
    n_blk = seq // BLOCK
    scale = (C_NOPE_DIM + C_ROPE_DIM) ** -0.5
    kn_all = jnp.concatenate([knc, kn], axis=1)
    kp_all = jnp.concatenate([kpc, kp], axis=1)
    v_all = jnp.concatenate([vc, v], axis=1)

    def scores(qn_, qp_, kn_, kp_):
        return (jnp.einsum('bqhd,bshd->bhqs', qn_, kn_)
                + jnp.einsum('bqhr,bsr->bhqs', qp_, kp_)).astype(jnp.float32) * scale

    def block(qs):
        qn_b, qp_b = qs
        p = softmax_f32(scores(qn_b, qp_b, kn_all, kp_all)).astype(v_all.dtype)
        return jnp.einsum('bhqs,bshd->bqhd', p, v_all)

    to_blocks = lambda u: jnp.moveaxis(u.reshape((bsz, n_blk, BLOCK) + u.shape[2:]), 1, 0)
    out = lax.map(block, (to_blocks(qn), to_blocks(qp)))
    out = jnp.moveaxis(out, 0, 1).reshape(bsz, seq, C_HEADS * C_V_DIM)
    out_c = None
    if need_ctx:
        pc = softmax_f32(scores(qnc, qpc, knc, kpc)).astype(vc.dtype)
        out_c = jnp.einsum('bhqs,bshd->bqhd', pc, vc).reshape(bsz, tc.shape[1], C_HEADS * C_V_DIM)
    return out, out_c


def neighbourhood_attention(t, tc, rel_bias, need_ctx):
    q, k, v = split_qkv(t, D_HEADS, D_HEADS)
    qc, kc, vc = split_qkv(tc, D_HEADS, D_HEADS)
    bsz, seq, n_h, hd = q.shape
    rows = seq // GRID_W
    kh = min(NA_KH, rows)
    n_nb = kh * NA_KW
    scale = hd ** -0.5
    qg = q.reshape(bsz, rows, GRID_W, n_h, hd)
    kg = k.reshape(bsz, rows, GRID_W, n_h, hd)
    vg = v.reshape(bsz, rows, GRID_W, n_h, hd)
    colq = jnp.arange(GRID_W)
    cstart = jnp.clip(colq - NA_KW // 2, 0, GRID_W - NA_KW)
    cidx = cstart[:, None] + jnp.arange(NA_KW)[None, :]
    coff = cidx - colq[:, None] + (NA_KW - 1)

    def row_block(r):
        rs = jnp.clip(r - kh // 2, 0, rows - kh)
        q_r = lax.dynamic_index_in_dim(qg, r, axis=1, keepdims=False)
        k_win = lax.dynamic_slice_in_dim(kg, rs, kh, axis=1)[:, :, cidx]
        v_win = lax.dynamic_slice_in_dim(vg, rs, kh, axis=1)[:, :, cidx]
        roff = rs + jnp.arange(kh) - r + (NA_KH - 1)
        bias = rel_bias[:, roff[None, :, None], coff[:, None, :]]
        s_nb = jnp.einsum('bchd,bkcjhd->bhckj', q_r, k_win).astype(jnp.float32) * scale
        s_nb = (s_nb + bias.astype(jnp.float32)[None]).reshape(bsz, n_h, GRID_W, n_nb)
        s_cx = jnp.einsum('bchd,bshd->bhcs', q_r, kc).astype(jnp.float32) * scale
        p = softmax_f32(jnp.concatenate([s_nb, s_cx], axis=-1)).astype(v.dtype)
        p_nb = p[..., :n_nb].reshape(bsz, n_h, GRID_W, kh, NA_KW)
        return (jnp.einsum('bhckj,bkcjhd->bchd', p_nb, v_win)
                + jnp.einsum('bhcs,bshd->bchd', p[..., n_nb:], vc))

    out = lax.map(row_block, jnp.arange(rows))
    out = jnp.moveaxis(out, 0, 1).reshape(bsz, seq, n_h * hd)
    out_c = ctx_gqa(qc, kc, vc, scale) if need_ctx else None
    return out, out_c


def token_mixing(h, hc, w_in, a_sink, b_q_norm, b_k_norm, c_q_norm, c_kv_norm,
                 c_w_q_up, c_w_kv_up, d_rel_bias, w_out, row, col, need_ctx):
    p, pc = h @ w_in, hc @ w_in
    o1, o2, o3 = A_COLS, A_COLS + B_COLS, A_COLS + B_COLS + C_COLS
    oa, oac = window_sink_attention(p[..., :o1], pc[..., :o1], a_sink, row, col, need_ctx)
    ob, obc = dense_qknorm_attention(p[..., o1:o2], pc[..., o1:o2], b_q_norm, b_k_norm, row, col, need_ctx)
    oc, occ = latent_attention(p[..., o2:o3], pc[..., o2:o3], c_q_norm, c_kv_norm,
                               c_w_q_up, c_w_kv_up, row, col, need_ctx)
    od, odc = neighbourhood_attention(p[..., o3:], pc[..., o3:], d_rel_bias, need_ctx)
    out = jnp.concatenate([oa, ob, oc, od], axis=-1) @ w_out
    out_c = jnp.concatenate([oac, obc, occ, odc], axis=-1) @ w_out if need_ctx else None
    return out, out_c


def setup_inputs(seed: int = 0) -> dict:
    key = jax.random.key(seed)
    ks = jax.random.split(key, 24)
    f32 = jnp.float32
    nrm = lambda k, shape, s: jax.random.normal(k, shape, f32) * s
    gain = lambda k, shape: 1.0 + 0.02 * jax.random.normal(k, shape, f32)
    L, D = DEPTH, D_MODEL
    return {
        "x": nrm(ks[0], (BATCH, SEQ, D), 1.0),
        "c": nrm(ks[1], (BATCH, D), 1.0),
        "ctx": nrm(ks[2], (BATCH, CTX_LEN, D), 1.0),
        "c_ctx": nrm(ks[3], (D,), 1.0),
        "w_mod_down": nrm(ks[4], (L, D, COND_RANK), D ** -0.5),
        "w_mod_up": nrm(ks[5], (L, COND_RANK, N_MOD * D), 0.5 * COND_RANK ** -0.5),
        "norm_ffn1": gain(ks[6], (L, D)),
        "ffn1_w_gu": nrm(ks[7], (L, D, 2 * D_FF), D ** -0.5),
        "ffn1_w_down": nrm(ks[8], (L, D_FF, D), D_FF ** -0.5),
        "norm_mix": gain(ks[9], (L, D)),
        "w_in": nrm(ks[10], (L, D, IN_COLS), D ** -0.5),
        "a_sink": nrm(ks[11], (L, A_HEADS), 0.5),
        "b_q_norm": gain(ks[12], (L, HEAD_DIM)),
        "b_k_norm": gain(ks[13], (L, HEAD_DIM)),
        "c_q_norm": gain(ks[14], (L, C_Q_RANK)),
        "c_kv_norm": gain(ks[15], (L, C_KV_RANK)),
        "c_w_q_up": nrm(ks[16], (L, C_Q_RANK, C_HEADS * (C_NOPE_DIM + C_ROPE_DIM)), C_Q_RANK ** -0.5),
        "c_w_kv_up": nrm(ks[17], (L, C_KV_RANK, C_HEADS * (C_NOPE_DIM + C_V_DIM)), C_KV_RANK ** -0.5),
        "d_rel_bias": nrm(ks[18], (L, D_HEADS, 2 * NA_KH - 1, 2 * NA_KW - 1), 0.5),
        "w_out": nrm(ks[19], (L, MIX_WIDTH, D), MIX_WIDTH ** -0.5),
        "norm_ffn2": gain(ks[20], (L, D)),
        "ffn2_w_gu": nrm(ks[21], (L, D, 2 * D_FF), D ** -0.5),
        "ffn2_w_down": nrm(ks[22], (L, D_FF, D), D_FF ** -0.5),
        "final_norm": gain(ks[23], (D,)),
    }


def reference(x, c, ctx, c_ctx, w_mod_down, w_mod_up, norm_ffn1, ffn1_w_gu, ffn1_w_down,
              norm_mix, w_in, a_sink, b_q_norm, b_k_norm, c_q_norm, c_kv_norm, c_w_q_up,
              c_w_kv_up, d_rel_bias, w_out, norm_ffn2, ffn2_w_gu, ffn2_w_down, final_norm):
    seq = x.shape[1]
    t = jnp.arange(seq, dtype=jnp.int32)
    row, col = t // GRID_W, t % GRID_W
    s_lat = jax.nn.silu(c)
    s_ctx = jax.nn.silu(c_ctx)[None]
    xl, xc = x, ctx
    for l in range(DEPTH):
        need_ctx = l < DEPTH - 1
        sh1, sc1, g1, sh2, sc2, g2, sh3, sc3, g3 = modulation(s_lat, w_mod_down[l], w_mod_up[l])
        ch1, cc1, cg1, ch2, cc2, cg2, ch3, cc3, cg3 = modulation(s_ctx, w_mod_down[l], w_mod_up[l])
        xl = xl + 0.5 * g1 * swiglu(modulate(xl, norm_ffn1[l], sh1, sc1), ffn1_w_gu[l], ffn1_w_down[l])
        xc = xc + 0.5 * cg1 * swiglu(modulate(xc, norm_ffn1[l], ch1, cc1), ffn1_w_gu[l], ffn1_w_down[l])
        h = modulate(xl, norm_mix[l], sh2, sc2)
        hc = modulate(xc, norm_mix[l], ch2, cc2)
        o, oc = token_mixing(h, hc, w_in[l], a_sink[l], b_q_norm[l], b_k_norm[l], c_q_norm[l],
                             c_kv_norm[l], c_w_q_up[l], c_w_kv_up[l], d_rel_bias[l], w_out[l],
                             row, col, need_ctx)
        xl = xl + g2 * o
        xl = xl + 0.5 * g3 * swiglu(modulate(xl, norm_ffn2[l], sh3, sc3), ffn2_w_gu[l], ffn2_w_down[l])
        if need_ctx:
            xc = xc + cg2 * oc
            xc = xc + 0.5 * cg3 * swiglu(modulate(xc, norm_ffn2[l], ch3, cc3), ffn2_w_gu[l], ffn2_w_down[l])
    return rms_norm(xl, final_norm)
```

```python
import functools

import jax
import jax.numpy as jnp
from jax import lax
from jax.experimental import pallas as pl
from jax.experimental.pallas import tpu as pltpu

F32 = jnp.float32
BF16 = jnp.bfloat16

GRID_W = 64
HEAD_DIM = 128
ROPE_THETA = 10000.0
NORM_EPS = 1e-6
NEG_INF = -1e30
N_MOD = 9
A_HEADS, A_KV_HEADS, A_WINDOW = 8, 2, 128
B_HEADS, B_KV_HEADS = 8, 2
C_HEADS, C_NOPE_DIM, C_ROPE_DIM, C_V_DIM = 8, 128, 64, 128
D_HEADS, NA_KH, NA_KW = 8, 8, 16
A_BLOCK = 128

LANES = 128
VMEM_LIMIT_BYTES = 56 * 1024 * 1024
C_QK_PAD = 2 * LANES

A_Q0, A_K0, A_V0 = 0, 8, 10
B_Q0, B_K0, B_V0 = 12, 20, 22
D_Q0, D_K0, D_V0 = 24, 32, 40
MAIN_COLS = 48 * LANES


def _params(*sem):
    return pltpu.CompilerParams(dimension_semantics=sem, vmem_limit_bytes=VMEM_LIMIT_BYTES)


def _pick(n, prefs):
    for p in prefs:
        if n % p == 0:
            return p
    raise ValueError(f"no tile in {prefs} divides {n}")


def _dot(a, b):
    return jnp.dot(a, b, preferred_element_type=F32)


def _dot_nt(a, b):
    return lax.dot_general(a, b, (((1,), (1,)), ((), ())), preferred_element_type=F32)


def _silu(x):
    return x / (1.0 + jnp.exp(-x))


def _rope(x, cos, sin_signed, half):
    n = x.shape[-1]
    lane = lax.broadcasted_iota(jnp.int32, x.shape, x.ndim - 1)
    first = jnp.bitwise_and(lane, 2 * half - 1) < half
    rot = jnp.where(first, pltpu.roll(x, n - half, x.ndim - 1), pltpu.roll(x, half, x.ndim - 1))
    return x * cos + rot * sin_signed


def _mod_down_kernel(s_ref, w_ref, o_ref, acc_ref):
    k = pl.program_id(1)

    @pl.when(k == 0)
    def _():
        acc_ref[...] = jnp.zeros_like(acc_ref)

    acc_ref[...] += _dot(_silu(s_ref[...]).astype(BF16), w_ref[...].astype(BF16))

    @pl.when(k == pl.num_programs(1) - 1)
    def _():
        o_ref[...] = acc_ref[...]


def _mod_up_kernel(t_ref, w_ref, o_ref):
    o_ref[...] = _dot(t_ref[...].astype(BF16), w_ref[...].astype(BF16))


def _modulation(c, c_ctx, w_down, w_up):
    n_layers, d, rank = w_down.shape
    n_out = w_up.shape[2]
    s = jnp.zeros((8, d), F32).at[0].set(c[0]).at[1].set(c_ctx)
    tk = _pick(d, (1024, 512, 256, 128))
    t = pl.pallas_call(
        _mod_down_kernel,
        out_shape=jax.ShapeDtypeStruct((n_layers, 8, rank), F32),
        grid=(n_layers, d // tk),
        in_specs=[pl.BlockSpec((8, tk), lambda l, k: (0, k)),
                  pl.BlockSpec((None, tk, rank), lambda l, k: (l, k, 0))],
        out_specs=pl.BlockSpec((None, 8, rank), lambda l, k: (l, 0, 0)),
        scratch_shapes=[pltpu.VMEM((8, rank), F32)],
        compiler_params=_params("parallel", "arbitrary"),
        name="mod_down",
    )(s, w_down)
    tn = _pick(n_out, (2048, 1024, 512, 256, 128))
    m = pl.pallas_call(
        _mod_up_kernel,
        out_shape=jax.ShapeDtypeStruct((n_layers, 8, n_out), F32),
        grid=(n_layers, n_out // tn),
        in_specs=[pl.BlockSpec((None, 8, rank), lambda l, j: (l, 0, 0)),
                  pl.BlockSpec((None, rank, tn), lambda l, j: (l, 0, j))],
        out_specs=pl.BlockSpec((None, 8, tn), lambda l, j: (l, 0, j)),
        compiler_params=_params("parallel", "parallel"),
        name="mod_up",
    )(t, w_up)
    return m[:, :2, :].reshape(n_layers, 2, N_MOD, 1, d)


def _norm_mod_kernel(x_ref, gain_ref, shift_ref, scale_ref, o_ref):
    x = x_ref[...]
    y = x * lax.rsqrt(jnp.mean(x * x, axis=-1, keepdims=True) + NORM_EPS) * gain_ref[...]
    o_ref[...] = (y * (1.0 + scale_ref[...]) + shift_ref[...]).astype(o_ref.dtype)


def _norm_mod(x, gain, mod_l, k_shift, k_scale, n_lat, tr):
    rows, d = x.shape
    n_lat_blocks = n_lat // tr

    def mod_spec(k):
        return pl.BlockSpec((None, None, 1, d),
                            lambda i: (jnp.where(i >= n_lat_blocks, 1, 0), k, 0, 0))

    return pl.pallas_call(
        _norm_mod_kernel,
        out_shape=jax.ShapeDtypeStruct((rows, d), BF16),
        grid=(rows // tr,),
        in_specs=[pl.BlockSpec((tr, d), lambda i: (i, 0)),
                  pl.BlockSpec((1, d), lambda i: (0, 0)),
                  mod_spec(k_shift), mod_spec(k_scale)],
        out_specs=pl.BlockSpec((tr, d), lambda i: (i, 0)),
        compiler_params=_params("parallel"),
        name="norm_mod",
    )(x, gain.reshape(1, d), mod_l, mod_l)


def _final_norm_kernel(x_ref, gain_ref, o_ref):
    x = x_ref[...]
    o_ref[...] = x * lax.rsqrt(jnp.mean(x * x, axis=-1, keepdims=True) + NORM_EPS) * gain_ref[...]


def _final_norm(x, gain, tr):
    rows, d = x.shape
    return pl.pallas_call(
        _final_norm_kernel,
        out_shape=jax.ShapeDtypeStruct((rows, d), F32),
        grid=(rows // tr,),
        in_specs=[pl.BlockSpec((tr, d), lambda i: (i, 0)),
                  pl.BlockSpec((1, d), lambda i: (0, 0))],
        out_specs=pl.BlockSpec((tr, d), lambda i: (i, 0)),
        compiler_params=_params("parallel"),
        name="final_norm",
    )(x, gain.reshape(1, d))


def _gate_up_kernel(h_ref, wg_ref, wu_ref, o_ref):
    h = h_ref[...]
    g = _dot(h, wg_ref[...])
    u = _dot(h, wu_ref[...])
    o_ref[...] = (_silu(g) * u).astype(o_ref.dtype)


def _gate_up(h, w_gu, n_rows, tm):
    d = h.shape[1]
    f = w_gu.shape[1] // 2
    tn = _pick(f, (512, 256, 128))
    nj = f // tn
    return pl.pallas_call(
        _gate_up_kernel,
        out_shape=jax.ShapeDtypeStruct((n_rows, f), BF16),
        grid=(n_rows // tm, nj),
        in_specs=[pl.BlockSpec((tm, d), lambda i, j: (i, 0)),
                  pl.BlockSpec((d, tn), lambda i, j: (0, j)),
                  pl.BlockSpec((d, tn), lambda i, j: (0, j + nj))],
        out_specs=pl.BlockSpec((tm, tn), lambda i, j: (i, j)),
        compiler_params=_params("parallel", "arbitrary"),
        name="gate_up",
    )(h, w_gu, w_gu)


def _residual_kernel(a_ref, w_ref, x_ref, g_ref, o_ref, *, coef, n_lat, tm):
    y = _dot(a_ref[...], w_ref[...])
    row = pl.program_id(0) * tm + lax.broadcasted_iota(jnp.int32, (tm, 1), 0)
    gate = jnp.where(row < n_lat, g_ref[0], g_ref[1])
    o_ref[...] = x_ref[...] + (coef * gate) * y


def _residual_matmul(a, w, x, mod_l, k_gate, coef, n_rows, n_lat, tm):
    kdim = a.shape[1]
    d = w.shape[1]
    tn = _pick(d, (512, 256, 128))
    return pl.pallas_call(
        functools.partial(_residual_kernel, coef=coef, n_lat=n_lat, tm=tm),
        out_shape=jax.ShapeDtypeStruct((n_rows, d), F32),
        grid=(n_rows // tm, d // tn),
        in_specs=[pl.BlockSpec((tm, kdim), lambda i, j: (i, 0)),
                  pl.BlockSpec((kdim, tn), lambda i, j: (0, j)),
                  pl.BlockSpec((tm, tn), lambda i, j: (i, j)),
                  pl.BlockSpec((2, None, 1, tn), lambda i, j: (0, k_gate, 0, j))],
        out_specs=pl.BlockSpec((tm, tn), lambda i, j: (i, j)),
        compiler_params=_params("parallel", "arbitrary"),
        name="residual_matmul",
    )(a, w, x, mod_l)


def _matmul_kernel(a_ref, w_ref, o_ref):
    o_ref[...] = _dot(a_ref[...], w_ref[...]).astype(o_ref.dtype)


def _matmul(a, w, tm, tn, out_dtype):
    rows, kdim = a.shape
    n = w.shape[1]
    return pl.pallas_call(
        _matmul_kernel,
        out_shape=jax.ShapeDtypeStruct((rows, n), out_dtype),
        grid=(rows // tm, n // tn),
        in_specs=[pl.BlockSpec((tm, kdim), lambda i, j: (i, 0)),
                  pl.BlockSpec((kdim, tn), lambda i, j: (0, j))],
        out_specs=pl.BlockSpec((tm, tn), lambda i, j: (i, j)),
        compiler_params=_params("parallel", "arbitrary"),
        name="matmul",
    )(a, w)


def _prep_kernel(p_ref, cos_ref, sin_ref, cosc_ref, sinc_ref, gq_ref, gk_ref, gcq_ref, gckv_ref,
                 main_ref, qa_ref, kva_ref, kpe_ref, *, q_rank, kv_rank):
    cos, sin = cos_ref[...], sin_ref[...]
    scale = HEAD_DIM ** -0.5

    def blk(b):
        return p_ref[:, b * LANES:(b + 1) * LANES]

    def put(b, val):
        main_ref[:, b * LANES:(b + 1) * LANES] = val.astype(main_ref.dtype)

    def head_norm(x, g):
        return x * lax.rsqrt(jnp.mean(x * x, axis=-1, keepdims=True) + NORM_EPS) * g

    for h in range(A_HEADS):
        put(A_Q0 + h, _rope(blk(A_Q0 + h), cos, sin, 32) * scale)
    for h in range(A_KV_HEADS):
        put(A_K0 + h, _rope(blk(A_K0 + h), cos, sin, 32))
        put(A_V0 + h, blk(A_V0 + h))
    for h in range(B_HEADS):
        put(B_Q0 + h, _rope(head_norm(blk(B_Q0 + h), gq_ref[...]), cos, sin, 32) * scale)
    for h in range(B_KV_HEADS):
        put(B_K0 + h, _rope(head_norm(blk(B_K0 + h), gk_ref[...]), cos, sin, 32))
        put(B_V0 + h, blk(B_V0 + h))
    for h in range(D_HEADS):
        put(D_Q0 + h, blk(D_Q0 + h) * scale)
        put(D_K0 + h, blk(D_K0 + h))
        put(D_V0 + h, blk(D_V0 + h))

    def low_rank_norm(col0, width, g_ref, o_ref):
        x = p_ref[:, col0:col0 + width]
        y = x * lax.rsqrt(jnp.mean(x * x, axis=-1, keepdims=True) + NORM_EPS) * g_ref[...]
        o_ref[...] = y.astype(o_ref.dtype)

    low_rank_norm(MAIN_COLS, q_rank, gcq_ref, qa_ref)
    low_rank_norm(MAIN_COLS + q_rank, kv_rank, gckv_ref, kva_ref)
    kpe = p_ref[:, MAIN_COLS + q_rank + kv_rank:MAIN_COLS + q_rank + kv_rank + LANES]
    kpe_ref[...] = _rope(kpe, cosc_ref[...], sinc_ref[...], 16).astype(kpe_ref.dtype)


def _prep(p, tables, g_q, g_k, g_cq, g_ckv, tr):
    rows, cols = p.shape
    q_rank, kv_rank = g_cq.shape[0], g_ckv.shape[0]
    row_spec = lambda w: pl.BlockSpec((tr, w), lambda i: (i, 0))
    vec_spec = lambda w: pl.BlockSpec((1, w), lambda i: (0, 0))
    return pl.pallas_call(
        functools.partial(_prep_kernel, q_rank=q_rank, kv_rank=kv_rank),
        out_shape=(jax.ShapeDtypeStruct((rows, MAIN_COLS), BF16),
                   jax.ShapeDtypeStruct((rows, q_rank), BF16),
                   jax.ShapeDtypeStruct((rows, kv_rank), BF16),
                   jax.ShapeDtypeStruct((rows, LANES), BF16)),
        grid=(rows // tr,),
        in_specs=[row_spec(cols), row_spec(LANES), row_spec(LANES), row_spec(LANES), row_spec(LANES),
                  vec_spec(LANES), vec_spec(LANES), vec_spec(q_rank), vec_spec(kv_rank)],
        out_specs=(row_spec(MAIN_COLS), row_spec(q_rank), row_spec(kv_rank), row_spec(LANES)),
        compiler_params=_params("parallel"),
        name="prep",
    )(p, *tables, g_q.reshape(1, -1), g_k.reshape(1, -1), g_cq.reshape(1, -1), g_ckv.reshape(1, -1))


def _rope_tables(n_lat, n_ctx):
    t = jnp.arange(n_lat, dtype=jnp.int32)
    row, col = (t // GRID_W).astype(F32), (t % GRID_W).astype(F32)
    lane = jnp.arange(LANES)

    def table(dim, live):
        half = dim // 2
        pair = half // 2
        inv = ROPE_THETA ** (-jnp.arange(0, half, 2, dtype=F32) / half)
        inv_lane = inv[lane % pair]
        pos = jnp.where(((lane // half) % 2 == 0)[None, :], row[:, None], col[:, None])
        ang = pos * inv_lane[None, :]
        sign = jnp.where((lane % half) < pair, -1.0, 1.0).astype(F32)
        on = (lane < live)[None, :]
        cos = jnp.where(on, jnp.cos(ang), 1.0)
        sin = jnp.where(on, jnp.sin(ang) * sign[None, :], 0.0)
        pad = lambda a, v: jnp.concatenate([a, jnp.full((n_ctx, LANES), v, F32)], axis=0)
        return pad(cos, 1.0), pad(sin, 0.0)

    cos, sin = table(HEAD_DIM, LANES)
    cosc, sinc = table(C_ROPE_DIM, C_ROPE_DIM)
    return cos, sin, cosc, sinc


def _cq_up_kernel(a_ref, w_ref, cos_ref, sin_ref, o_ref, *, scale):
    y = _dot(a_ref[...], w_ref[...])
    cos, sin = cos_ref[...], sin_ref[...]
    for h in range(C_HEADS):
        c0 = h * C_QK_PAD
        o_ref[:, c0:c0 + LANES] = (y[:, c0:c0 + LANES] * scale).astype(o_ref.dtype)
        pe = _rope(y[:, c0 + LANES:c0 + 2 * LANES], cos, sin, 16)
        o_ref[:, c0 + LANES:c0 + 2 * LANES] = (pe * scale).astype(o_ref.dtype)


def _ckv_up_kernel(a_ref, w_ref, kpe_ref, k_ref, v_ref):
    y = _dot(a_ref[...], w_ref[...])
    kpe = kpe_ref[...]
    for h in range(C_HEADS):
        k_ref[:, h * C_QK_PAD:h * C_QK_PAD + LANES] = y[:, h * LANES:(h + 1) * LANES].astype(k_ref.dtype)
        k_ref[:, h * C_QK_PAD + LANES:(h + 1) * C_QK_PAD] = kpe
    v_ref[...] = y[:, C_HEADS * LANES:].astype(v_ref.dtype)


def _mla_expand(qa, kva, kpe, w_q, w_kv, cosc, sinc, tm):
    rows, q_rank = qa.shape
    kv_rank = kva.shape[1]
    scale = (C_NOPE_DIM + C_ROPE_DIM) ** -0.5
    row_spec = lambda w: pl.BlockSpec((tm, w), lambda i: (i, 0))
    full_spec = lambda a: pl.BlockSpec(a.shape, lambda i: (0, 0))
    qc = pl.pallas_call(
        functools.partial(_cq_up_kernel, scale=scale),
        out_shape=jax.ShapeDtypeStruct((rows, C_HEADS * C_QK_PAD), BF16),
        grid=(rows // tm,),
        in_specs=[row_spec(q_rank), full_spec(w_q), row_spec(LANES), row_spec(LANES)],
        out_specs=row_spec(C_HEADS * C_QK_PAD),
        compiler_params=_params("parallel"),
        name="mla_q_up",
    )(qa, w_q, cosc, sinc)
    kc, vc = pl.pallas_call(
        _ckv_up_kernel,
        out_shape=(jax.ShapeDtypeStruct((rows, C_HEADS * C_QK_PAD), BF16),
                   jax.ShapeDtypeStruct((rows, C_HEADS * C_V_DIM), BF16)),
        grid=(rows // tm,),
        in_specs=[row_spec(kv_rank), full_spec(w_kv), row_spec(LANES)],
        out_specs=(row_spec(C_HEADS * C_QK_PAD), row_spec(C_HEADS * C_V_DIM)),
        compiler_params=_params("parallel"),
        name="mla_kv_up",
    )(kva, w_kv, kpe)
    return qc, kc, vc


def _stack_heads(q_ref, n, width):
    if n == 1:
        return q_ref[...]
    return jnp.concatenate([q_ref[:, g * width:(g + 1) * width] for g in range(n)], axis=0)


def _flash_kernel(q_ref, k_ref, v_ref, o_ref, m_sc, l_sc, acc_sc, *, group, dk, dv, tk, n_lat, n_ctx):
    tq = q_ref.shape[0]
    q = _stack_heads(q_ref, group, dk)

    s = _dot_nt(q, k_ref[n_lat:n_lat + n_ctx, :])
    m0 = jnp.max(s, axis=-1, keepdims=True)
    p = jnp.exp(s - m0)
    m_sc[...] = m0
    l_sc[...] = jnp.sum(p, axis=-1, keepdims=True)
    acc_sc[...] = _dot(p.astype(BF16), v_ref[n_lat:n_lat + n_ctx, :])

    def step(c, carry):
        start = pl.multiple_of(c * tk, tk)
        s = _dot_nt(q, k_ref[pl.ds(start, tk), :])
        m_prev = m_sc[...]
        m_new = jnp.maximum(m_prev, jnp.max(s, axis=-1, keepdims=True))
        alpha = jnp.exp(m_prev - m_new)
        p = jnp.exp(s - m_new)
        l_sc[...] = alpha * l_sc[...] + jnp.sum(p, axis=-1, keepdims=True)
        acc_sc[...] = alpha * acc_sc[...] + _dot(p.astype(BF16), v_ref[pl.ds(start, tk), :])
        m_sc[...] = m_new
        return carry

    lax.fori_loop(0, n_lat // tk, step, 0)
    out = acc_sc[...] * (1.0 / l_sc[...])
    for g in range(group):
        o_ref[:, g * dv:(g + 1) * dv] = out[g * tq:(g + 1) * tq].astype(o_ref.dtype)


def _flash(q_arr, k_arr, v_arr, *, n_kv, group, dk, dv, q_blk0, k_blk0, v_blk0, tq, tk, n_lat, n_ctx):
    rows = k_arr.shape[0]
    return pl.pallas_call(
        functools.partial(_flash_kernel, group=group, dk=dk, dv=dv, tk=tk, n_lat=n_lat, n_ctx=n_ctx),
        out_shape=jax.ShapeDtypeStruct((n_lat, n_kv * group * dv), BF16),
        grid=(n_kv, n_lat // tq),
        in_specs=[pl.BlockSpec((tq, group * dk), lambda g, i: (i, q_blk0 + g)),
                  pl.BlockSpec((rows, dk), lambda g, i: (0, k_blk0 + g)),
                  pl.BlockSpec((rows, dv), lambda g, i: (0, v_blk0 + g))],
        out_specs=pl.BlockSpec((tq, group * dv), lambda g, i: (i, g)),
        scratch_shapes=[pltpu.VMEM((group * tq, 1), F32), pltpu.VMEM((group * tq, 1), F32),
                        pltpu.VMEM((group * tq, dv), F32)],
        compiler_params=_params("parallel", "arbitrary"),
        name="flash",
    )(q_arr, k_arr, v_arr)


def _window_kernel(sink_ref, q_ref, k_ref, v_ref, o_ref, *, group, n_lat, n_ctx, window):
    g, n = pl.program_id(0), pl.program_id(1)
    tq = q_ref.shape[0]
    span = 3 * tq
    q = _stack_heads(q_ref, group, HEAD_DIM)
    start = pl.multiple_of(jnp.clip((n - 1) * tq, 0, n_lat - span), tq)
    s_w = _dot_nt(q, k_ref[pl.ds(start, span), :])
    qpos = n * tq + jnp.bitwise_and(lax.broadcasted_iota(jnp.int32, (group * tq, 1), 0), tq - 1)
    kpos = start + lax.broadcasted_iota(jnp.int32, (1, span), 1)
    s_w = jnp.where(jnp.abs(qpos - kpos) <= window, s_w, NEG_INF)
    s_c = _dot_nt(q, k_ref[n_lat:n_lat + n_ctx, :])
    sink = jnp.concatenate([jnp.full((tq, 1), sink_ref[g * group + i], F32) for i in range(group)], axis=0)
    m = jnp.maximum(jnp.maximum(jnp.max(s_w, axis=-1, keepdims=True),
                                jnp.max(s_c, axis=-1, keepdims=True)), sink)
    p_w, p_c = jnp.exp(s_w - m), jnp.exp(s_c - m)
    denom = (jnp.sum(p_w, axis=-1, keepdims=True) + jnp.sum(p_c, axis=-1, keepdims=True)
             + jnp.exp(sink - m))
    out = (_dot(p_w.astype(BF16), v_ref[pl.ds(start, span), :])
           + _dot(p_c.astype(BF16), v_ref[n_lat:n_lat + n_ctx, :])) * (1.0 / denom)
    for i in range(group):
        o_ref[:, i * HEAD_DIM:(i + 1) * HEAD_DIM] = out[i * tq:(i + 1) * tq].astype(o_ref.dtype)


def _window_attention(main, sink, n_lat, n_ctx):
    rows = main.shape[0]
    group = A_HEADS // A_KV_HEADS
    tq = A_BLOCK
    assert A_WINDOW <= tq and n_lat >= 3 * tq
    return pl.pallas_call(
        functools.partial(_window_kernel, group=group, n_lat=n_lat, n_ctx=n_ctx, window=A_WINDOW),
        out_shape=jax.ShapeDtypeStruct((n_lat, A_HEADS * HEAD_DIM), BF16),
        grid=(A_KV_HEADS, n_lat // tq),
        in_specs=[pl.BlockSpec(memory_space=pltpu.SMEM),
                  pl.BlockSpec((tq, group * HEAD_DIM), lambda g, n: (n, g)),
                  pl.BlockSpec((rows, HEAD_DIM), lambda g, n: (0, A_K0 + g)),
                  pl.BlockSpec((rows, HEAD_DIM), lambda g, n: (0, A_V0 + g))],
        out_specs=pl.BlockSpec((tq, group * HEAD_DIM), lambda g, n: (n, g)),
        compiler_params=_params("parallel", "arbitrary"),
        name="window_attention",
    )(sink, main, main, main)


def _nbr_kernel(q_ref, k_ref, v_ref, b_ref, o_ref, *, rows_per_step, n_grid_rows, n_lat, n_ctx):
    rb = pl.program_id(1)
    nb_keys = NA_KH * GRID_W
    q = q_ref[...]
    kc, vc = k_ref[n_lat:n_lat + n_ctx, :], v_ref[n_lat:n_lat + n_ctx, :]
    s_cx = _dot_nt(q, kc)
    m_cx = jnp.max(s_cx, axis=-1, keepdims=True)
    for j in range(rows_per_step):
        sl = slice(j * GRID_W, (j + 1) * GRID_W)
        r = rb * rows_per_step + j
        rs = jnp.clip(r - NA_KH // 2, 0, n_grid_rows - NA_KH)
        k0 = pl.multiple_of(rs * GRID_W, GRID_W)
        s_nb = _dot_nt(q[sl], k_ref[pl.ds(k0, nb_keys), :]) + b_ref[r - rs]
        m = jnp.maximum(jnp.max(s_nb, axis=-1, keepdims=True), m_cx[sl])
        p_nb, p_cx = jnp.exp(s_nb - m), jnp.exp(s_cx[sl] - m)
        denom = jnp.sum(p_nb, axis=-1, keepdims=True) + jnp.sum(p_cx, axis=-1, keepdims=True)
        out = _dot(p_nb.astype(BF16), v_ref[pl.ds(k0, nb_keys), :]) + _dot(p_cx.astype(BF16), vc)
        o_ref[sl, :] = (out * (1.0 / denom)).astype(o_ref.dtype)


def _nbr_bias_table(rel_bias):
    v = jnp.arange(NA_KH)[:, None, None, None]
    c = jnp.arange(GRID_W)[None, :, None, None]
    kr = jnp.arange(NA_KH)[None, None, :, None]
    kc = jnp.arange(GRID_W)[None, None, None, :]
    roff = jnp.broadcast_to(kr - v + (NA_KH - 1), (NA_KH, GRID_W, NA_KH, GRID_W))
    cstart = jnp.clip(c - NA_KW // 2, 0, GRID_W - NA_KW)
    inside = jnp.broadcast_to((kc >= cstart) & (kc < cstart + NA_KW), roff.shape)
    coff = jnp.broadcast_to(jnp.clip(kc - c + (NA_KW - 1), 0, 2 * NA_KW - 2), roff.shape)
    bias = jnp.where(inside[None], rel_bias.astype(F32)[:, roff, coff], NEG_INF)
    return bias.reshape(rel_bias.shape[0], NA_KH, GRID_W, NA_KH * GRID_W)


def _nbr_attention(main, rel_bias, n_lat, n_ctx):
    rows = main.shape[0]
    n_grid_rows = n_lat // GRID_W
    assert n_grid_rows >= NA_KH
    rps = _pick(n_grid_rows, (8, 4, 2, 1))
    table = _nbr_bias_table(rel_bias)
    return pl.pallas_call(
        functools.partial(_nbr_kernel, rows_per_step=rps, n_grid_rows=n_grid_rows, n_lat=n_lat, n_ctx=n_ctx),
        out_shape=jax.ShapeDtypeStruct((n_lat, D_HEADS * HEAD_DIM), BF16),
        grid=(D_HEADS, n_grid_rows // rps),
        in_specs=[pl.BlockSpec((rps * GRID_W, HEAD_DIM), lambda h, rb: (rb, D_Q0 + h)),
                  pl.BlockSpec((rows, HEAD_DIM), lambda h, rb: (0, D_K0 + h)),
                  pl.BlockSpec((rows, HEAD_DIM), lambda h, rb: (0, D_V0 + h)),
                  pl.BlockSpec((None, NA_KH, GRID_W, NA_KH * GRID_W), lambda h, rb: (h, 0, 0, 0))],
        out_specs=pl.BlockSpec((rps * GRID_W, HEAD_DIM), lambda h, rb: (rb, h)),
        compiler_params=_params("parallel", "arbitrary"),
        name="nbr_attention",
    )(main, main, main, table)


def _ctx_attn_kernel(sink_ref, q_ref, k_ref, v_ref, o_ref):
    sink = sink_ref[pl.program_id(0)]
    s = _dot_nt(q_ref[...], k_ref[...])
    m = jnp.maximum(jnp.max(s, axis=-1, keepdims=True), sink)
    p = jnp.exp(s - m)
    denom = jnp.sum(p, axis=-1, keepdims=True) + jnp.exp(sink - m)
    o_ref[...] = (_dot(p.astype(BF16), v_ref[...]) * (1.0 / denom)).astype(o_ref.dtype)


def _ctx_attention(q_arr, k_arr, v_arr, sink, *, n_heads, group, dk, dv, q_blk0, k_blk0, v_blk0, n_lat, n_ctx):
    rb = n_lat // n_ctx
    return pl.pallas_call(
        _ctx_attn_kernel,
        out_shape=jax.ShapeDtypeStruct((n_ctx, n_heads * dv), BF16),
        grid=(n_heads,),
        in_specs=[pl.BlockSpec(memory_space=pltpu.SMEM),
                  pl.BlockSpec((n_ctx, dk), lambda h: (rb, q_blk0 + h)),
                  pl.BlockSpec((n_ctx, dk), lambda h: (rb, k_blk0 + h // group)),
                  pl.BlockSpec((n_ctx, dv), lambda h: (rb, v_blk0 + h // group))],
        out_specs=pl.BlockSpec((n_ctx, dv), lambda h: (0, h)),
        compiler_params=_params("parallel"),
        name="ctx_attention",
    )(sink, q_arr, k_arr, v_arr)


def _relayout_w_in(w, q_rank, kv_rank, total_cols):
    a_cols = (A_HEADS + 2 * A_KV_HEADS) * HEAD_DIM
    b_cols = (B_HEADS + 2 * B_KV_HEADS) * HEAD_DIM
    c_cols = q_rank + kv_rank + C_ROPE_DIM
    o1, o2, o3 = a_cols, a_cols + b_cols, a_cols + b_cols + c_cols
    pad = total_cols - (w.shape[1] - C_ROPE_DIM) - C_ROPE_DIM
    return jnp.concatenate([w[:, :o2], w[:, o3:], w[:, o2:o3],
                            jnp.zeros((w.shape[0], pad), w.dtype)], axis=1).astype(BF16)


def _relayout_w_q_up(w):
    r = w.shape[0]
    w = w.reshape(r, C_HEADS, C_NOPE_DIM + C_ROPE_DIM)
    w = jnp.pad(w, ((0, 0), (0, 0), (0, C_QK_PAD - C_NOPE_DIM - C_ROPE_DIM)))
    return w.reshape(r, C_HEADS * C_QK_PAD).astype(BF16)


def _relayout_w_kv_up(w):
    r = w.shape[0]
    w = w.reshape(r, C_HEADS, C_NOPE_DIM + C_V_DIM)
    return jnp.concatenate([w[:, :, :C_NOPE_DIM].reshape(r, -1), w[:, :, C_NOPE_DIM:].reshape(r, -1)],
                           axis=1).astype(BF16)


def kernel(x, c, ctx, c_ctx, w_mod_down, w_mod_up, norm_ffn1, ffn1_w_gu, ffn1_w_down, norm_mix, w_in,
           a_sink, b_q_norm, b_k_norm, c_q_norm, c_kv_norm, c_w_q_up, c_w_kv_up, d_rel_bias, w_out,
           norm_ffn2, ffn2_w_gu, ffn2_w_down, final_norm):
    bsz, n_lat, d = x.shape
    n_ctx = ctx.shape[1]
    n_layers = w_in.shape[0]
    q_rank, kv_rank = c_q_norm.shape[1], c_kv_norm.shape[1]
    assert bsz == 1 and n_lat % GRID_W == 0 and n_lat % n_ctx == 0 and n_ctx % LANES == 0
    rows = n_lat + n_ctx
    tr = n_ctx
    tm_all = _pick(rows, (768, 512, 256, 128))
    tm_lat = _pick(n_lat, (1024, 512, 256, 128))
    in_cols = MAIN_COLS + q_rank + kv_rank + LANES
    in_cols = -(-in_cols // 512) * 512

    mod = _modulation(c, c_ctx, w_mod_down, w_mod_up)
    tables = _rope_tables(n_lat, n_ctx)
    no_sink = jnp.full((max(B_HEADS, C_HEADS, D_HEADS),), NEG_INF, F32)

    xs = jnp.concatenate([x[0], ctx[0]], axis=0)

    def ffn(xs, gain, w_gu, w_down, mod_l, k0, n_rows, tm):
        h = _norm_mod(xs, gain, mod_l, k0, k0 + 1, n_lat, tr)
        act = _gate_up(h, w_gu.astype(BF16), n_rows, tm)
        return _residual_matmul(act, w_down.astype(BF16), xs, mod_l, k0 + 2, 0.5, n_rows, n_lat, tm)

    for l in range(n_layers):
        need_ctx = l < n_layers - 1
        mod_l = mod[l]
        xs = ffn(xs, norm_ffn1[l], ffn1_w_gu[l], ffn1_w_down[l], mod_l, 0, rows, tm_all)

        h = _norm_mod(xs, norm_mix[l], mod_l, 3, 4, n_lat, tr)
        p = _matmul(h, _relayout_w_in(w_in[l], q_rank, kv_rank, in_cols), tm_all, 512, F32)
        main, qa, kva, kpe = _prep(p, tables, b_q_norm[l], b_k_norm[l], c_q_norm[l], c_kv_norm[l], tr)
        qc, kc, vc = _mla_expand(qa, kva, kpe, _relayout_w_q_up(c_w_q_up[l]), _relayout_w_kv_up(c_w_kv_up[l]),
                                 tables[2], tables[3], tr)

        oa = _window_attention(main, a_sink[l], n_lat, n_ctx)
        ob = _flash(main, main, main, n_kv=B_KV_HEADS, group=B_HEADS // B_KV_HEADS, dk=HEAD_DIM, dv=HEAD_DIM,
                    q_blk0=B_Q0 // (B_HEADS // B_KV_HEADS), k_blk0=B_K0, v_blk0=B_V0,
                    tq=_pick(n_lat, (256, 128)), tk=_pick(n_lat, (512, 256, 128)), n_lat=n_lat, n_ctx=n_ctx)
        oc = _flash(qc, kc, vc, n_kv=C_HEADS, group=1, dk=C_QK_PAD, dv=C_V_DIM, q_blk0=0, k_blk0=0, v_blk0=0,
                    tq=_pick(n_lat, (512, 256, 128)), tk=_pick(n_lat, (512, 256, 128)), n_lat=n_lat, n_ctx=n_ctx)
        od = _nbr_attention(main, d_rel_bias[l], n_lat, n_ctx)
        o = jnp.concatenate([oa, ob, oc, od], axis=1)

        if need_ctx:
            ctx_kw = dict(n_lat=n_lat, n_ctx=n_ctx)
            oac = _ctx_attention(main, main, main, a_sink[l], n_heads=A_HEADS, group=A_HEADS // A_KV_HEADS,
                                 dk=HEAD_DIM, dv=HEAD_DIM, q_blk0=A_Q0, k_blk0=A_K0, v_blk0=A_V0, **ctx_kw)
            obc = _ctx_attention(main, main, main, no_sink, n_heads=B_HEADS, group=B_HEADS // B_KV_HEADS,
                                 dk=HEAD_DIM, dv=HEAD_DIM, q_blk0=B_Q0, k_blk0=B_K0, v_blk0=B_V0, **ctx_kw)
            occ = _ctx_attention(qc, kc, vc, no_sink, n_heads=C_HEADS, group=1, dk=C_QK_PAD, dv=C_V_DIM,
                                 q_blk0=0, k_blk0=0, v_blk0=0, **ctx_kw)
            odc = _ctx_attention(main, main, main, no_sink, n_heads=D_HEADS, group=1, dk=HEAD_DIM, dv=HEAD_DIM,
                                 q_blk0=D_Q0, k_blk0=D_K0, v_blk0=D_V0, **ctx_kw)
            o = jnp.concatenate([o, jnp.concatenate([oac, obc, occ, odc], axis=1)], axis=0)
            n_rows, tm = rows, tm_all
        else:
            n_rows, tm = n_lat, tm_lat

        xs = _residual_matmul(o, w_out[l].astype(BF16), xs, mod_l, 5, 1.0, n_rows, n_lat, tm)
        xs = ffn(xs, norm_ffn2[l], ffn2_w_gu[l], ffn2_w_down[l], mod_l, 6, n_rows, tm)

    return _final_norm(xs[:n_lat], final_norm, tr)[None]
```

```python
import functools

import jax
import jax.numpy as jnp
from jax import lax
from jax.experimental import pallas as pl
from jax.experimental.pallas import tpu as pltpu

F32 = jnp.float32
BF16 = jnp.bfloat16

GRID_W = 64
HEAD_DIM = 128
ROPE_THETA = 10000.0
NORM_EPS = 1e-6
NEG_INF = -1e30
LOG2E = 1.4426950408889634
N_MOD = 9
A_HEADS, A_KV_HEADS, A_WINDOW = 8, 2, 128
B_HEADS, B_KV_HEADS = 8, 2
C_HEADS, C_NOPE_DIM, C_ROPE_DIM, C_V_DIM = 8, 128, 64, 128
D_HEADS, NA_KH, NA_KW = 8, 8, 16

LANES = 128
VMEM_LIMIT_BYTES = 56 * 1024 * 1024
C_QK_PAD = 2 * LANES

A_Q0, A_K0, A_V0 = 0, 8, 10
B_Q0, B_K0, B_V0 = 12, 20, 22
D_Q0, D_K0, D_V0 = 24, 32, 40
MAIN_COLS = 48 * LANES


def _params(*sem):
    return pltpu.CompilerParams(dimension_semantics=sem, vmem_limit_bytes=VMEM_LIMIT_BYTES)


def _pick(n, prefs):
    for p in prefs:
        if n % p == 0:
            return p
    raise ValueError(f"no tile in {prefs} divides {n}")


def _dot(a, b):
    return jnp.dot(a, b, preferred_element_type=F32)


def _dot_nt(a, b):
    return lax.dot_general(a, b, (((1,), (1,)), ((), ())), preferred_element_type=F32)


def _silu(x):
    return x / (1.0 + jnp.exp(-x))


def _rope(x, cos, sin_signed, half):
    n = x.shape[-1]
    lane = lax.broadcasted_iota(jnp.int32, x.shape, x.ndim - 1)
    first = jnp.bitwise_and(lane, 2 * half - 1) < half
    rot = jnp.where(first, pltpu.roll(x, n - half, x.ndim - 1), pltpu.roll(x, half, x.ndim - 1))
    return x * cos + rot * sin_signed


def _mod_down_kernel(s_ref, w_ref, o_ref, acc_ref):
    k = pl.program_id(1)

    @pl.when(k == 0)
    def _():
        acc_ref[...] = jnp.zeros_like(acc_ref)

    acc_ref[...] += _dot(_silu(s_ref[...]).astype(BF16), w_ref[...].astype(BF16))

    @pl.when(k == pl.num_programs(1) - 1)
    def _():
        o_ref[...] = acc_ref[...]


def _mod_up_kernel(t_ref, w_ref, o_ref):
    o_ref[...] = _dot(t_ref[...].astype(BF16), w_ref[...].astype(BF16))


def _modulation(c, c_ctx, w_down, w_up):
    n_layers, d, rank = w_down.shape
    n_out = w_up.shape[2]
    s = jnp.zeros((8, d), F32).at[0].set(c[0]).at[1].set(c_ctx)
    tk = _pick(d, (1024, 512, 256, 128))
    t = pl.pallas_call(
        _mod_down_kernel,
        out_shape=jax.ShapeDtypeStruct((n_layers, 8, rank), F32),
        grid=(n_layers, d // tk),
        in_specs=[pl.BlockSpec((8, tk), lambda l, k: (0, k)),
                  pl.BlockSpec((None, tk, rank), lambda l, k: (l, k, 0))],
        out_specs=pl.BlockSpec((None, 8, rank), lambda l, k: (l, 0, 0)),
        scratch_shapes=[pltpu.VMEM((8, rank), F32)],
        compiler_params=_params("parallel", "arbitrary"),
        name="mod_down",
    )(s, w_down)
    tn = _pick(n_out, (2048, 1024, 512, 256, 128))
    m = pl.pallas_call(
        _mod_up_kernel,
        out_shape=jax.ShapeDtypeStruct((n_layers, 8, n_out), F32),
        grid=(n_layers, n_out // tn),
        in_specs=[pl.BlockSpec((None, 8, rank), lambda l, j: (l, 0, 0)),
                  pl.BlockSpec((None, rank, tn), lambda l, j: (l, 0, j))],
        out_specs=pl.BlockSpec((None, 8, tn), lambda l, j: (l, 0, j)),
        compiler_params=_params("parallel", "parallel"),
        name="mod_up",
    )(t, w_up)
    return m[:, :2, :].reshape(n_layers, 2, N_MOD, 1, d)


def _norm_mod_kernel(x_ref, gain_ref, shift_ref, scale_ref, o_ref):
    x = x_ref[...]
    y = x * lax.rsqrt(jnp.mean(x * x, axis=-1, keepdims=True) + NORM_EPS) * gain_ref[...]
    o_ref[...] = (y * (1.0 + scale_ref[...]) + shift_ref[...]).astype(o_ref.dtype)


def _norm_mod(x, gain, mod_l, k_shift, k_scale, n_lat, tr):
    rows, d = x.shape
    n_lat_blocks = n_lat // tr

    def mod_spec(k):
        return pl.BlockSpec((None, None, 1, d),
                            lambda i: (jnp.where(i >= n_lat_blocks, 1, 0), k, 0, 0))

    return pl.pallas_call(
        _norm_mod_kernel,
        out_shape=jax.ShapeDtypeStruct((rows, d), BF16),
        grid=(rows // tr,),
        in_specs=[pl.BlockSpec((tr, d), lambda i: (i, 0)),
                  pl.BlockSpec((1, d), lambda i: (0, 0)),
                  mod_spec(k_shift), mod_spec(k_scale)],
        out_specs=pl.BlockSpec((tr, d), lambda i: (i, 0)),
        compiler_params=_params("parallel"),
        name="norm_mod",
    )(x, gain.reshape(1, d), mod_l, mod_l)


def _final_norm_kernel(x_ref, gain_ref, o_ref):
    x = x_ref[...]
    o_ref[...] = x * lax.rsqrt(jnp.mean(x * x, axis=-1, keepdims=True) + NORM_EPS) * gain_ref[...]


def _final_norm(x, gain, tr):
    rows, d = x.shape
    return pl.pallas_call(
        _final_norm_kernel,
        out_shape=jax.ShapeDtypeStruct((rows, d), F32),
        grid=(rows // tr,),
        in_specs=[pl.BlockSpec((tr, d), lambda i: (i, 0)),
                  pl.BlockSpec((1, d), lambda i: (0, 0))],
        out_specs=pl.BlockSpec((tr, d), lambda i: (i, 0)),
        compiler_params=_params("parallel"),
        name="final_norm",
    )(x, gain.reshape(1, d))


def _gate_up_kernel(h_ref, wg_ref, wu_ref, o_ref):
    h = h_ref[...]
    g = _dot(h, wg_ref[...])
    u = _dot(h, wu_ref[...])
    o_ref[...] = (_silu(g) * u).astype(o_ref.dtype)


def _gate_up(h, w_gu, n_rows, tm):
    d = h.shape[1]
    f = w_gu.shape[1] // 2
    tn = _pick(f, (512, 256, 128))
    nj = f // tn
    return pl.pallas_call(
        _gate_up_kernel,
        out_shape=jax.ShapeDtypeStruct((n_rows, f), BF16),
        grid=(n_rows // tm, nj),
        in_specs=[pl.BlockSpec((tm, d), lambda i, j: (i, 0)),
                  pl.BlockSpec((d, tn), lambda i, j: (0, j)),
                  pl.BlockSpec((d, tn), lambda i, j: (0, j + nj))],
        out_specs=pl.BlockSpec((tm, tn), lambda i, j: (i, j)),
        compiler_params=_params("parallel", "arbitrary"),
        name="gate_up",
    )(h, w_gu, w_gu)


def _residual_kernel(a_ref, w_ref, x_ref, g_ref, o_ref, *, coef, n_lat, tm):
    y = _dot(a_ref[...], w_ref[...])
    row = pl.program_id(0) * tm + lax.broadcasted_iota(jnp.int32, (tm, 1), 0)
    gate = jnp.where(row < n_lat, g_ref[0], g_ref[1])
    o_ref[...] = x_ref[...] + (coef * gate) * y


def _residual_matmul(a, w, x, mod_l, k_gate, coef, n_rows, n_lat, tm):
    kdim = a.shape[1]
    d = w.shape[1]
    tn = _pick(d, (512, 256, 128))
    return pl.pallas_call(
        functools.partial(_residual_kernel, coef=coef, n_lat=n_lat, tm=tm),
        out_shape=jax.ShapeDtypeStruct((n_rows, d), F32),
        grid=(n_rows // tm, d // tn),
        in_specs=[pl.BlockSpec((tm, kdim), lambda i, j: (i, 0)),
                  pl.BlockSpec((kdim, tn), lambda i, j: (0, j)),
                  pl.BlockSpec((tm, tn), lambda i, j: (i, j)),
                  pl.BlockSpec((2, None, 1, tn), lambda i, j: (0, k_gate, 0, j))],
        out_specs=pl.BlockSpec((tm, tn), lambda i, j: (i, j)),
        compiler_params=_params("parallel", "arbitrary"),
        name="residual_matmul",
    )(a, w, x, mod_l)


def _matmul_kernel(a_ref, w_ref, o_ref):
    o_ref[...] = _dot(a_ref[...], w_ref[...]).astype(o_ref.dtype)


def _matmul(a, w, tm, tn, out_dtype):
    rows, kdim = a.shape
    n = w.shape[1]
    return pl.pallas_call(
        _matmul_kernel,
        out_shape=jax.ShapeDtypeStruct((rows, n), out_dtype),
        grid=(rows // tm, n // tn),
        in_specs=[pl.BlockSpec((tm, kdim), lambda i, j: (i, 0)),
                  pl.BlockSpec((kdim, tn), lambda i, j: (0, j))],
        out_specs=pl.BlockSpec((tm, tn), lambda i, j: (i, j)),
        compiler_params=_params("parallel", "arbitrary"),
        name="matmul",
    )(a, w)


def _prep_kernel(p_ref, cos_ref, sin_ref, cosc_ref, sinc_ref, gq_ref, gk_ref, gcq_ref, gckv_ref,
                 main_ref, qa_ref, kva_ref, kpe_ref, *, q_rank, kv_rank):
    cos, sin = cos_ref[...], sin_ref[...]
    scale = HEAD_DIM ** -0.5 * LOG2E

    def blk(b):
        return p_ref[:, b * LANES:(b + 1) * LANES]

    def put(b, val):
        main_ref[:, b * LANES:(b + 1) * LANES] = val.astype(main_ref.dtype)

    def head_norm(x, g):
        return x * lax.rsqrt(jnp.mean(x * x, axis=-1, keepdims=True) + NORM_EPS) * g

    for h in range(A_HEADS):
        put(A_Q0 + h, _rope(blk(A_Q0 + h), cos, sin, 32) * scale)
    for h in range(A_KV_HEADS):
        put(A_K0 + h, _rope(blk(A_K0 + h), cos, sin, 32))
        put(A_V0 + h, blk(A_V0 + h))
    for h in range(B_HEADS):
        put(B_Q0 + h, _rope(head_norm(blk(B_Q0 + h), gq_ref[...]), cos, sin, 32) * scale)
    for h in range(B_KV_HEADS):
        put(B_K0 + h, _rope(head_norm(blk(B_K0 + h), gk_ref[...]), cos, sin, 32))
        put(B_V0 + h, blk(B_V0 + h))
    for h in range(D_HEADS):
        put(D_Q0 + h, blk(D_Q0 + h) * scale)
        put(D_K0 + h, blk(D_K0 + h))
        put(D_V0 + h, blk(D_V0 + h))

    def low_rank_norm(col0, width, g_ref, o_ref):
        x = p_ref[:, col0:col0 + width]
        y = x * lax.rsqrt(jnp.mean(x * x, axis=-1, keepdims=True) + NORM_EPS) * g_ref[...]
        o_ref[...] = y.astype(o_ref.dtype)

    low_rank_norm(MAIN_COLS, q_rank, gcq_ref, qa_ref)
    low_rank_norm(MAIN_COLS + q_rank, kv_rank, gckv_ref, kva_ref)
    kpe = p_ref[:, MAIN_COLS + q_rank + kv_rank:MAIN_COLS + q_rank + kv_rank + LANES]
    kpe_ref[...] = _rope(kpe, cosc_ref[...], sinc_ref[...], 16).astype(kpe_ref.dtype)


def _prep(p, tables, g_q, g_k, g_cq, g_ckv, tr):
    rows, cols = p.shape
    q_rank, kv_rank = g_cq.shape[0], g_ckv.shape[0]
    row_spec = lambda w: pl.BlockSpec((tr, w), lambda i: (i, 0))
    vec_spec = lambda w: pl.BlockSpec((1, w), lambda i: (0, 0))
    return pl.pallas_call(
        functools.partial(_prep_kernel, q_rank=q_rank, kv_rank=kv_rank),
        out_shape=(jax.ShapeDtypeStruct((rows, MAIN_COLS), BF16),
                   jax.ShapeDtypeStruct((rows, q_rank), BF16),
                   jax.ShapeDtypeStruct((rows, kv_rank), BF16),
                   jax.ShapeDtypeStruct((rows, LANES), BF16)),
        grid=(rows // tr,),
        in_specs=[row_spec(cols), row_spec(LANES), row_spec(LANES), row_spec(LANES), row_spec(LANES),
                  vec_spec(LANES), vec_spec(LANES), vec_spec(q_rank), vec_spec(kv_rank)],
        out_specs=(row_spec(MAIN_COLS), row_spec(q_rank), row_spec(kv_rank), row_spec(LANES)),
        compiler_params=_params("parallel"),
        name="prep",
    )(p, *tables, g_q.reshape(1, -1), g_k.reshape(1, -1), g_cq.reshape(1, -1), g_ckv.reshape(1, -1))


def _rope_tables(n_lat, n_ctx):
    t = jnp.arange(n_lat, dtype=jnp.int32)
    row, col = (t // GRID_W).astype(F32), (t % GRID_W).astype(F32)
    lane = jnp.arange(LANES)

    def table(dim, live):
        half = dim // 2
        pair = half // 2
        inv = ROPE_THETA ** (-jnp.arange(0, half, 2, dtype=F32) / half)
        inv_lane = inv[lane % pair]
        pos = jnp.where(((lane // half) % 2 == 0)[None, :], row[:, None], col[:, None])
        ang = pos * inv_lane[None, :]
        sign = jnp.where((lane % half) < pair, -1.0, 1.0).astype(F32)
        on = (lane < live)[None, :]
        cos = jnp.where(on, jnp.cos(ang), 1.0)
        sin = jnp.where(on, jnp.sin(ang) * sign[None, :], 0.0)
        pad = lambda a, v: jnp.concatenate([a, jnp.full((n_ctx, LANES), v, F32)], axis=0)
        return pad(cos, 1.0), pad(sin, 0.0)

    cos, sin = table(HEAD_DIM, LANES)
    cosc, sinc = table(C_ROPE_DIM, C_ROPE_DIM)
    return cos, sin, cosc, sinc


def _cq_up_kernel(a_ref, w_ref, cos_ref, sin_ref, o_ref, *, scale):
    y = _dot(a_ref[...], w_ref[...])
    cos, sin = cos_ref[...], sin_ref[...]
    for h in range(C_HEADS):
        c0 = h * C_QK_PAD
        o_ref[:, c0:c0 + LANES] = (y[:, c0:c0 + LANES] * scale).astype(o_ref.dtype)
        pe = _rope(y[:, c0 + LANES:c0 + 2 * LANES], cos, sin, 16)
        o_ref[:, c0 + LANES:c0 + 2 * LANES] = (pe * scale).astype(o_ref.dtype)


def _ckv_up_kernel(a_ref, w_ref, kpe_ref, k_ref, v_ref):
    y = _dot(a_ref[...], w_ref[...])
    kpe = kpe_ref[...]
    for h in range(C_HEADS):
        k_ref[:, h * C_QK_PAD:h * C_QK_PAD + LANES] = y[:, h * LANES:(h + 1) * LANES].astype(k_ref.dtype)
        k_ref[:, h * C_QK_PAD + LANES:(h + 1) * C_QK_PAD] = kpe
    v_ref[...] = y[:, C_HEADS * LANES:].astype(v_ref.dtype)


def _mla_expand(qa, kva, kpe, w_q, w_kv, cosc, sinc, tm):
    rows, q_rank = qa.shape
    kv_rank = kva.shape[1]
    scale = (C_NOPE_DIM + C_ROPE_DIM) ** -0.5 * LOG2E
    row_spec = lambda w: pl.BlockSpec((tm, w), lambda i: (i, 0))
    full_spec = lambda a: pl.BlockSpec(a.shape, lambda i: (0, 0))
    qc = pl.pallas_call(
        functools.partial(_cq_up_kernel, scale=scale),
        out_shape=jax.ShapeDtypeStruct((rows, C_HEADS * C_QK_PAD), BF16),
        grid=(rows // tm,),
        in_specs=[row_spec(q_rank), full_spec(w_q), row_spec(LANES), row_spec(LANES)],
        out_specs=row_spec(C_HEADS * C_QK_PAD),
        compiler_params=_params("parallel"),
        name="mla_q_up",
    )(qa, w_q, cosc, sinc)
    kc, vc = pl.pallas_call(
        _ckv_up_kernel,
        out_shape=(jax.ShapeDtypeStruct((rows, C_HEADS * C_QK_PAD), BF16),
                   jax.ShapeDtypeStruct((rows, C_HEADS * C_V_DIM), BF16)),
        grid=(rows // tm,),
        in_specs=[row_spec(kv_rank), full_spec(w_kv), row_spec(LANES)],
        out_specs=(row_spec(C_HEADS * C_QK_PAD), row_spec(C_HEADS * C_V_DIM)),
        compiler_params=_params("parallel"),
        name="mla_kv_up",
    )(kva, w_kv, kpe)
    return qc, kc, vc


def _flash_kernel(q_ref, k_ref, v_ref, o_ref, m_sc, l_sc, acc_sc, *, group, dk, tc, tk, n_lat, n_ctx):
    tq = q_ref.shape[0]
    chains = [(g, r) for g in range(group) for r in range(tq // tc)]

    def chunk(k, v, first):
        n_blocks = k.shape[0] // LANES
        for ci, (g, r) in enumerate(chains):
            s = _dot_nt(q_ref[r * tc:(r + 1) * tc, g * dk:(g + 1) * dk], k)
            blocks = [s[:, b * LANES:(b + 1) * LANES] for b in range(n_blocks)]
            mx = blocks[0]
            for blk in blocks[1:]:
                mx = jnp.maximum(mx, blk)
            m_new = jnp.broadcast_to(jnp.max(mx, axis=-1, keepdims=True), (tc, LANES))
            if not first:
                m_prev = m_sc[ci]
                m_new = jnp.maximum(m_prev, m_new)
                alpha = jnp.exp2(m_prev - m_new)
            ps = [jnp.exp2(blk - m_new) for blk in blocks]
            l_new = ps[0]
            for p in ps[1:]:
                l_new = l_new + p
            pv = _dot(jnp.concatenate([p.astype(BF16) for p in ps], axis=1), v)
            if first:
                l_sc[ci] = l_new
                acc_sc[ci] = pv
            else:
                l_sc[ci] = alpha * l_sc[ci] + l_new
                acc_sc[ci] = alpha * acc_sc[ci] + pv
            m_sc[ci] = m_new

    chunk(k_ref[n_lat:n_lat + n_ctx, :], v_ref[n_lat:n_lat + n_ctx, :], True)

    def step(c, carry):
        start = pl.multiple_of(c * tk, tk)
        chunk(k_ref[pl.ds(start, tk), :], v_ref[pl.ds(start, tk), :], False)
        return carry

    lax.fori_loop(0, n_lat // tk, step, 0)
    for ci, (g, r) in enumerate(chains):
        inv = 1.0 / jnp.sum(l_sc[ci], axis=-1, keepdims=True)
        o_ref[r * tc:(r + 1) * tc, g * LANES:(g + 1) * LANES] = (acc_sc[ci] * inv).astype(o_ref.dtype)


def _flash(q_arr, k_arr, v_arr, *, n_kv, group, dk, q_blk0, k_blk0, v_blk0, tq, tc, tk, n_lat, n_ctx):
    rows = k_arr.shape[0]
    n_chains = group * (tq // tc)
    stat = pltpu.VMEM((n_chains, tc, LANES), F32)
    return pl.pallas_call(
        functools.partial(_flash_kernel, group=group, dk=dk, tc=tc, tk=tk, n_lat=n_lat, n_ctx=n_ctx),
        out_shape=jax.ShapeDtypeStruct((n_lat, n_kv * group * LANES), BF16),
        grid=(n_kv, n_lat // tq),
        in_specs=[pl.BlockSpec((tq, group * dk), lambda g, i: (i, q_blk0 + g)),
                  pl.BlockSpec((rows, dk), lambda g, i: (0, k_blk0 + g)),
                  pl.BlockSpec((rows, LANES), lambda g, i: (0, v_blk0 + g))],
        out_specs=pl.BlockSpec((tq, group * LANES), lambda g, i: (i, g)),
        scratch_shapes=[stat, stat, stat],
        compiler_params=_params("parallel", "arbitrary"),
        name="flash",
    )(q_arr, k_arr, v_arr)


def _lane_blocks(s):
    return [s[:, b * LANES:(b + 1) * LANES] for b in range(s.shape[1] // LANES)]


def _softmax_pv(segments, sink=None):
    blocks = [blk for blks, _ in segments for blk in blks]
    rows = blocks[0].shape[0]
    mx = blocks[0]
    for blk in blocks[1:]:
        mx = jnp.maximum(mx, blk)
    m = jnp.broadcast_to(jnp.max(mx, axis=-1, keepdims=True), (rows, LANES))
    if sink is not None:
        m = jnp.maximum(m, sink)
    out = l = None
    for blks, v in segments:
        ps = [jnp.exp2(blk - m) for blk in blks]
        for p in ps:
            l = p if l is None else l + p
        pv = _dot(jnp.concatenate([p.astype(BF16) for p in ps], axis=1), v)
        out = pv if out is None else out + pv
    if sink is not None:
        lane = lax.broadcasted_iota(jnp.int32, (rows, LANES), 1)
        l = l + jnp.where(lane == 0, jnp.exp2(sink - m), 0.0)
    return out * (1.0 / jnp.sum(l, axis=-1, keepdims=True))


def _window_kernel(sink_ref, q_ref, k_ref, v_ref, o_ref, *, group, tq, n_lat, n_ctx, window):
    g, n = pl.program_id(0), pl.program_id(1)
    span = tq + 2 * LANES
    kc, vc = k_ref[n_lat:n_lat + n_ctx, :], v_ref[n_lat:n_lat + n_ctx, :]
    row = jnp.bitwise_and(lax.broadcasted_iota(jnp.int32, (group * tq, LANES), 0), tq - 1)
    lane_minus_row = lax.broadcasted_iota(jnp.int32, (group * tq, LANES), 1) - row
    sink = jnp.concatenate([jnp.full((tq, LANES), sink_ref[g * group + i], F32) for i in range(group)], axis=0)
    n_sub = q_ref.shape[0] // tq
    for sub in range(n_sub):
        rows = slice(sub * tq, (sub + 1) * tq)
        q0 = (n * n_sub + sub) * tq
        start = pl.multiple_of(jnp.clip(q0 - LANES, 0, n_lat - span), LANES)
        kw, vw = k_ref[pl.ds(start, span), :], v_ref[pl.ds(start, span), :]
        q = jnp.concatenate([q_ref[rows, i * HEAD_DIM:(i + 1) * HEAD_DIM] for i in range(group)], axis=0)
        s_w = [jnp.where(jnp.abs(lane_minus_row + (start - q0 + b * LANES)) <= window, blk, NEG_INF)
               for b, blk in enumerate(_lane_blocks(_dot_nt(q, kw)))]
        out = _softmax_pv([(s_w, vw), (_lane_blocks(_dot_nt(q, kc)), vc)], sink=sink)
        for i in range(group):
            o_ref[rows, i * HEAD_DIM:(i + 1) * HEAD_DIM] = out[i * tq:(i + 1) * tq].astype(o_ref.dtype)


def _window_attention(main, sink, n_lat, n_ctx):
    rows = main.shape[0]
    group = A_HEADS // A_KV_HEADS
    tq = _pick(n_lat, (256, 128))
    assert A_WINDOW <= LANES and n_lat >= tq + 2 * LANES
    n_sub = _pick(n_lat // tq, (2, 1))
    return pl.pallas_call(
        functools.partial(_window_kernel, group=group, tq=tq, n_lat=n_lat, n_ctx=n_ctx, window=A_WINDOW),
        out_shape=jax.ShapeDtypeStruct((n_lat, A_HEADS * HEAD_DIM), BF16),
        grid=(A_KV_HEADS, n_lat // (n_sub * tq)),
        in_specs=[pl.BlockSpec(memory_space=pltpu.SMEM),
                  pl.BlockSpec((n_sub * tq, group * HEAD_DIM), lambda g, n: (n, g)),
                  pl.BlockSpec((rows, HEAD_DIM), lambda g, n: (0, A_K0 + g)),
                  pl.BlockSpec((rows, HEAD_DIM), lambda g, n: (0, A_V0 + g))],
        out_specs=pl.BlockSpec((n_sub * tq, group * HEAD_DIM), lambda g, n: (n, g)),
        compiler_params=_params("parallel", "arbitrary"),
        name="window_attention",
    )(sink, main, main, main)


NBR_Q_ROWS = 8
NBR_KEY_ROWS = 16


def _nbr_key_start(rb, n_grid_rows):
    return jnp.clip(rb * NBR_Q_ROWS - NA_KH // 2, 0, n_grid_rows - NBR_KEY_ROWS)


def _nbr_kernel(q_ref, k_ref, v_ref, b_ref, o_ref, *, chain_rows, n_grid_rows, n_lat, n_ctx):
    k0 = pl.multiple_of(_nbr_key_start(pl.program_id(1), n_grid_rows) * GRID_W, (NA_KH // 2) * GRID_W)
    n_keys = NBR_KEY_ROWS * GRID_W
    kw, vw = k_ref[pl.ds(k0, n_keys), :], v_ref[pl.ds(k0, n_keys), :]
    kc, vc = k_ref[n_lat:n_lat + n_ctx, :], v_ref[n_lat:n_lat + n_ctx, :]
    for sub in range(q_ref.shape[0] // chain_rows):
        rows = slice(sub * chain_rows, (sub + 1) * chain_rows)
        q = q_ref[rows, :]
        s_nb = _dot_nt(q, kw) + b_ref[rows, :]
        out = _softmax_pv([(_lane_blocks(s_nb), vw), (_lane_blocks(_dot_nt(q, kc)), vc)])
        o_ref[rows, :] = out.astype(o_ref.dtype)


def _nbr_bias_table(rel_bias):
    n_heads = rel_bias.shape[0]
    off = jnp.array([0, NA_KH // 2, NA_KH], jnp.int32)[:, None, None]
    j = jnp.arange(NBR_Q_ROWS)[None, :, None]
    kr = jnp.arange(NBR_KEY_ROWS)[None, None, :]
    centred = j - NA_KH // 2
    rs = jnp.stack([jnp.maximum(centred[0], 0), centred[0], jnp.minimum(centred[0], 0)])
    key_row = kr - off
    row_ok = (key_row >= rs) & (key_row < rs + NA_KH)
    row_sel = jax.nn.one_hot(key_row - j + (NA_KH - 1), 2 * NA_KH - 1, dtype=F32)
    c = jnp.arange(GRID_W)[:, None]
    kc = jnp.arange(GRID_W)[None, :]
    cstart = jnp.clip(c - NA_KW // 2, 0, GRID_W - NA_KW)
    col_ok = (kc >= cstart) & (kc < cstart + NA_KW)
    col_sel = jax.nn.one_hot(kc - c + (NA_KW - 1), 2 * NA_KW - 1, dtype=F32)
    bias = jnp.einsum("vjka,hab,cqb->vhjckq", row_sel, rel_bias.astype(F32) * LOG2E, col_sel,
                      precision=lax.Precision.HIGHEST)
    ok = row_ok[:, None, :, None, :, None] & col_ok[None, None, None, :, None, :]
    bias = jnp.where(ok, bias, NEG_INF)
    return bias.reshape(3, n_heads, NBR_Q_ROWS * GRID_W, NBR_KEY_ROWS * GRID_W)


def _nbr_attention(main, rel_bias, n_lat, n_ctx):
    rows = main.shape[0]
    n_grid_rows = n_lat // GRID_W
    assert n_grid_rows % NBR_Q_ROWS == 0 and n_grid_rows >= NBR_KEY_ROWS
    assert NBR_KEY_ROWS >= NBR_Q_ROWS + NA_KH and NA_KH // 2 * 2 == NA_KH
    n_blocks = n_grid_rows // NBR_Q_ROWS
    tq = NBR_Q_ROWS * GRID_W

    def variant(rb):
        return (rb * NBR_Q_ROWS - _nbr_key_start(rb, n_grid_rows)) // (NA_KH // 2)

    return pl.pallas_call(
        functools.partial(_nbr_kernel, chain_rows=tq // 2, n_grid_rows=n_grid_rows, n_lat=n_lat, n_ctx=n_ctx),
        out_shape=jax.ShapeDtypeStruct((n_lat, D_HEADS * HEAD_DIM), BF16),
        grid=(D_HEADS, n_blocks),
        in_specs=[pl.BlockSpec((tq, HEAD_DIM), lambda h, rb: (rb, D_Q0 + h)),
                  pl.BlockSpec((rows, HEAD_DIM), lambda h, rb: (0, D_K0 + h)),
                  pl.BlockSpec((rows, HEAD_DIM), lambda h, rb: (0, D_V0 + h)),
                  pl.BlockSpec((None, None, tq, NBR_KEY_ROWS * GRID_W), lambda h, rb: (variant(rb), h, 0, 0))],
        out_specs=pl.BlockSpec((tq, HEAD_DIM), lambda h, rb: (rb, h)),
        compiler_params=_params("parallel", "arbitrary"),
        name="nbr_attention",
    )(main, main, main, _nbr_bias_table(rel_bias))


def _ctx_attn_kernel(sink_ref, q_ref, k_ref, v_ref, o_ref):
    sink = sink_ref[pl.program_id(0)]
    s = _dot_nt(q_ref[...], k_ref[...])
    m = jnp.maximum(jnp.max(s, axis=-1, keepdims=True), sink)
    p = jnp.exp2(s - m)
    denom = jnp.sum(p, axis=-1, keepdims=True) + jnp.exp2(sink - m)
    o_ref[...] = (_dot(p.astype(BF16), v_ref[...]) * (1.0 / denom)).astype(o_ref.dtype)


def _ctx_attention(q_arr, k_arr, v_arr, sink, *, n_heads, group, dk, dv, q_blk0, k_blk0, v_blk0, n_lat, n_ctx):
    rb = n_lat // n_ctx
    return pl.pallas_call(
        _ctx_attn_kernel,
        out_shape=jax.ShapeDtypeStruct((n_ctx, n_heads * dv), BF16),
        grid=(n_heads,),
        in_specs=[pl.BlockSpec(memory_space=pltpu.SMEM),
                  pl.BlockSpec((n_ctx, dk), lambda h: (rb, q_blk0 + h)),
                  pl.BlockSpec((n_ctx, dk), lambda h: (rb, k_blk0 + h // group)),
                  pl.BlockSpec((n_ctx, dv), lambda h: (rb, v_blk0 + h // group))],
        out_specs=pl.BlockSpec((n_ctx, dv), lambda h: (0, h)),
        compiler_params=_params("parallel"),
        name="ctx_attention",
    )(sink, q_arr, k_arr, v_arr)


def _relayout_w_in(w, q_rank, kv_rank, total_cols):
    a_cols = (A_HEADS + 2 * A_KV_HEADS) * HEAD_DIM
    b_cols = (B_HEADS + 2 * B_KV_HEADS) * HEAD_DIM
    c_cols = q_rank + kv_rank + C_ROPE_DIM
    o1, o2, o3 = a_cols, a_cols + b_cols, a_cols + b_cols + c_cols
    pad = total_cols - (w.shape[1] - C_ROPE_DIM) - C_ROPE_DIM
    return jnp.concatenate([w[:, :o2], w[:, o3:], w[:, o2:o3],
                            jnp.zeros((w.shape[0], pad), w.dtype)], axis=1).astype(BF16)


def _relayout_w_q_up(w):
    r = w.shape[0]
    w = w.reshape(r, C_HEADS, C_NOPE_DIM + C_ROPE_DIM)
    w = jnp.pad(w, ((0, 0), (0, 0), (0, C_QK_PAD - C_NOPE_DIM - C_ROPE_DIM)))
    return w.reshape(r, C_HEADS * C_QK_PAD).astype(BF16)


def _relayout_w_kv_up(w):
    r = w.shape[0]
    w = w.reshape(r, C_HEADS, C_NOPE_DIM + C_V_DIM)
    return jnp.concatenate([w[:, :, :C_NOPE_DIM].reshape(r, -1), w[:, :, C_NOPE_DIM:].reshape(r, -1)],
                           axis=1).astype(BF16)


def kernel(x, c, ctx, c_ctx, w_mod_down, w_mod_up, norm_ffn1, ffn1_w_gu, ffn1_w_down, norm_mix, w_in,
           a_sink, b_q_norm, b_k_norm, c_q_norm, c_kv_norm, c_w_q_up, c_w_kv_up, d_rel_bias, w_out,
           norm_ffn2, ffn2_w_gu, ffn2_w_down, final_norm):
    bsz, n_lat, d = x.shape
    n_ctx = ctx.shape[1]
    n_layers = w_in.shape[0]
    q_rank, kv_rank = c_q_norm.shape[1], c_kv_norm.shape[1]
    assert bsz == 1 and n_lat % GRID_W == 0 and n_lat % n_ctx == 0 and n_ctx % LANES == 0
    rows = n_lat + n_ctx
    tr = n_ctx
    tm_all = _pick(rows, (768, 512, 256, 128))
    tm_lat = _pick(n_lat, (1024, 512, 256, 128))
    in_cols = MAIN_COLS + q_rank + kv_rank + LANES
    in_cols = -(-in_cols // 512) * 512
    flash_tc = _pick(n_lat, (1024, 512, 256, 128))
    flash_tk = _pick(n_lat, (1024, 512, 256, 128))

    mod = _modulation(c, c_ctx, w_mod_down, w_mod_up)
    tables = _rope_tables(n_lat, n_ctx)
    no_sink = jnp.full((max(B_HEADS, C_HEADS, D_HEADS),), NEG_INF, F32)

    xs = jnp.concatenate([x[0], ctx[0]], axis=0)

    def ffn(xs, gain, w_gu, w_down, mod_l, k0, n_rows, tm):
        h = _norm_mod(xs, gain, mod_l, k0, k0 + 1, n_lat, tr)
        act = _gate_up(h, w_gu.astype(BF16), n_rows, tm)
        return _residual_matmul(act, w_down.astype(BF16), xs, mod_l, k0 + 2, 0.5, n_rows, n_lat, tm)

    for l in range(n_layers):
        need_ctx = l < n_layers - 1
        mod_l = mod[l]
        xs = ffn(xs, norm_ffn1[l], ffn1_w_gu[l], ffn1_w_down[l], mod_l, 0, rows, tm_all)

        h = _norm_mod(xs, norm_mix[l], mod_l, 3, 4, n_lat, tr)
        p = _matmul(h, _relayout_w_in(w_in[l], q_rank, kv_rank, in_cols), tm_all, 512, F32)
        main, qa, kva, kpe = _prep(p, tables, b_q_norm[l], b_k_norm[l], c_q_norm[l], c_kv_norm[l], tr)
        qc, kc, vc = _mla_expand(qa, kva, kpe, _relayout_w_q_up(c_w_q_up[l]), _relayout_w_kv_up(c_w_kv_up[l]),
                                 tables[2], tables[3], tr)

        sink = a_sink[l] * LOG2E
        oa = _window_attention(main, sink, n_lat, n_ctx)
        ob = _flash(main, main, main, n_kv=B_KV_HEADS, group=B_HEADS // B_KV_HEADS, dk=HEAD_DIM,
                    q_blk0=B_Q0 // (B_HEADS // B_KV_HEADS), k_blk0=B_K0, v_blk0=B_V0,
                    tq=flash_tc, tc=flash_tc, tk=flash_tk, n_lat=n_lat, n_ctx=n_ctx)
        oc = _flash(qc, kc, vc, n_kv=C_HEADS, group=1, dk=C_QK_PAD, q_blk0=0, k_blk0=0, v_blk0=0,
                    tq=_pick(n_lat, (2 * flash_tc, flash_tc)), tc=flash_tc, tk=flash_tk, n_lat=n_lat, n_ctx=n_ctx)
        od = _nbr_attention(main, d_rel_bias[l], n_lat, n_ctx)
        o = jnp.concatenate([oa, ob, oc, od], axis=1)

        if need_ctx:
            ctx_kw = dict(n_lat=n_lat, n_ctx=n_ctx)
            oac = _ctx_attention(main, main, main, sink, n_heads=A_HEADS, group=A_HEADS // A_KV_HEADS,
                                 dk=HEAD_DIM, dv=HEAD_DIM, q_blk0=A_Q0, k_blk0=A_K0, v_blk0=A_V0, **ctx_kw)
            obc = _ctx_attention(main, main, main, no_sink, n_heads=B_HEADS, group=B_HEADS // B_KV_HEADS,
                                 dk=HEAD_DIM, dv=HEAD_DIM, q_blk0=B_Q0, k_blk0=B_K0, v_blk0=B_V0, **ctx_kw)
            occ = _ctx_attention(qc, kc, vc, no_sink, n_heads=C_HEADS, group=1, dk=C_QK_PAD, dv=C_V_DIM,
                                 q_blk0=0, k_blk0=0, v_blk0=0, **ctx_kw)
            odc = _ctx_attention(main, main, main, no_sink, n_heads=D_HEADS, group=1, dk=HEAD_DIM, dv=HEAD_DIM,
                                 q_blk0=D_Q0, k_blk0=D_K0, v_blk0=D_V0, **ctx_kw)
            o = jnp.concatenate([o, jnp.concatenate([oac, obc, occ, odc], axis=1)], axis=0)
            n_rows, tm = rows, tm_all
        else:
            n_rows, tm = n_lat, tm_lat

        xs = _residual_matmul(o, w_out[l].astype(BF16), xs, mod_l, 5, 1.0, n_rows, n_lat, tm)
        xs = ffn(xs, norm_ffn2[l], ffn2_w_gu[l], ffn2_w_down[l], mod_l, 6, n_rows, tm)

    return _final_norm(xs[:n_lat], final_norm, tr)[None]
```

```python
import functools

import jax
import jax.numpy as jnp
from jax import lax
from jax.experimental import pallas as pl
from jax.experimental.pallas import tpu as pltpu

F32 = jnp.float32
BF16 = jnp.bfloat16

GRID_W = 64
HEAD_DIM = 128
ROPE_THETA = 10000.0
NORM_EPS = 1e-6
NEG_INF = -1e30
LOG2E = 1.4426950408889634
N_MOD = 9
A_HEADS, A_KV_HEADS, A_WINDOW = 8, 2, 128
B_HEADS, B_KV_HEADS = 8, 2
C_HEADS, C_NOPE_DIM, C_ROPE_DIM, C_V_DIM = 8, 128, 64, 128
D_HEADS, NA_KH, NA_KW = 8, 8, 16

LANES = 128
VMEM_LIMIT_BYTES = 56 * 1024 * 1024
C_QK_PAD = 2 * LANES

A_Q0, A_K0, A_V0 = 0, 8, 10
B_Q0, B_K0, B_V0 = 12, 20, 22
D_Q0, D_K0, D_V0 = 24, 32, 40
MAIN_COLS = 48 * LANES
MIX_A0, MIX_B0, MIX_C0, MIX_D0 = 0, 8, 16, 24
MIX_COLS = 32 * LANES
_BUF_SPEC = pl.BlockSpec(memory_space=pl.ANY)


def _params(*sem):
    return pltpu.CompilerParams(dimension_semantics=sem, vmem_limit_bytes=VMEM_LIMIT_BYTES)


def _pick(n, prefs):
    for p in prefs:
        if n % p == 0:
            return p
    raise ValueError(f"no tile in {prefs} divides {n}")


def _dot(a, b):
    return jnp.dot(a, b, preferred_element_type=F32)


def _dot_nt(a, b):
    return lax.dot_general(a, b, (((1,), (1,)), ((), ())), preferred_element_type=F32)


def _silu(x):
    return x / (1.0 + jnp.exp(-x))


def _rope(x, cos, sin_signed, half):
    n = x.shape[-1]
    lane = lax.broadcasted_iota(jnp.int32, x.shape, x.ndim - 1)
    first = jnp.bitwise_and(lane, 2 * half - 1) < half
    rot = jnp.where(first, pltpu.roll(x, n - half, x.ndim - 1), pltpu.roll(x, half, x.ndim - 1))
    return x * cos + rot * sin_signed


def _mod_down_kernel(s_ref, w_ref, o_ref, acc_ref):
    k = pl.program_id(1)

    @pl.when(k == 0)
    def _():
        acc_ref[...] = jnp.zeros_like(acc_ref)

    acc_ref[...] += _dot(_silu(s_ref[...]).astype(BF16), w_ref[...].astype(BF16))

    @pl.when(k == pl.num_programs(1) - 1)
    def _():
        o_ref[...] = acc_ref[...]


def _mod_up_kernel(t_ref, w_ref, o_ref):
    o_ref[...] = _dot(t_ref[...].astype(BF16), w_ref[...].astype(BF16))


def _modulation(c, c_ctx, w_down, w_up):
    n_layers, d, rank = w_down.shape
    n_out = w_up.shape[2]
    s = jnp.zeros((8, d), F32).at[0].set(c[0]).at[1].set(c_ctx)
    tk = _pick(d, (1024, 512, 256, 128))
    t = pl.pallas_call(
        _mod_down_kernel,
        out_shape=jax.ShapeDtypeStruct((n_layers, 8, rank), F32),
        grid=(n_layers, d // tk),
        in_specs=[pl.BlockSpec((8, tk), lambda l, k: (0, k)),
                  pl.BlockSpec((None, tk, rank), lambda l, k: (l, k, 0))],
        out_specs=pl.BlockSpec((None, 8, rank), lambda l, k: (l, 0, 0)),
        scratch_shapes=[pltpu.VMEM((8, rank), F32)],
        compiler_params=_params("parallel", "arbitrary"),
        name="mod_down",
    )(s, w_down)
    tn = _pick(n_out, (2048, 1024, 512, 256, 128))
    m = pl.pallas_call(
        _mod_up_kernel,
        out_shape=jax.ShapeDtypeStruct((n_layers, 8, n_out), F32),
        grid=(n_layers, n_out // tn),
        in_specs=[pl.BlockSpec((None, 8, rank), lambda l, j: (l, 0, 0)),
                  pl.BlockSpec((None, rank, tn), lambda l, j: (l, 0, j))],
        out_specs=pl.BlockSpec((None, 8, tn), lambda l, j: (l, 0, j)),
        compiler_params=_params("parallel", "parallel"),
        name="mod_up",
    )(t, w_up)
    return m[:, :2, :].reshape(n_layers, 2, N_MOD, 1, d)


def _norm_mod_kernel(x_ref, gain_ref, shift_ref, scale_ref, o_ref):
    x = x_ref[...]
    y = x * lax.rsqrt(jnp.mean(x * x, axis=-1, keepdims=True) + NORM_EPS) * gain_ref[...]
    o_ref[...] = (y * (1.0 + scale_ref[...]) + shift_ref[...]).astype(o_ref.dtype)


def _norm_mod(x, gain, mod_l, k_shift, k_scale, n_lat, tr):
    rows, d = x.shape
    n_lat_blocks = n_lat // tr

    def mod_spec(k):
        return pl.BlockSpec((None, None, 1, d),
                            lambda i: (jnp.where(i >= n_lat_blocks, 1, 0), k, 0, 0))

    return pl.pallas_call(
        _norm_mod_kernel,
        out_shape=jax.ShapeDtypeStruct((rows, d), BF16),
        grid=(rows // tr,),
        in_specs=[pl.BlockSpec((tr, d), lambda i: (i, 0)),
                  pl.BlockSpec((1, d), lambda i: (0, 0)),
                  mod_spec(k_shift), mod_spec(k_scale)],
        out_specs=pl.BlockSpec((tr, d), lambda i: (i, 0)),
        compiler_params=_params("parallel"),
        name="norm_mod",
    )(x, gain.reshape(1, d), mod_l, mod_l)


def _final_norm_kernel(x_ref, gain_ref, o_ref):
    x = x_ref[...]
    o_ref[...] = x * lax.rsqrt(jnp.mean(x * x, axis=-1, keepdims=True) + NORM_EPS) * gain_ref[...]


def _final_norm(x, gain, tr):
    rows, d = x.shape
    return pl.pallas_call(
        _final_norm_kernel,
        out_shape=jax.ShapeDtypeStruct((rows, d), F32),
        grid=(rows // tr,),
        in_specs=[pl.BlockSpec((tr, d), lambda i: (i, 0)),
                  pl.BlockSpec((1, d), lambda i: (0, 0))],
        out_specs=pl.BlockSpec((tr, d), lambda i: (i, 0)),
        compiler_params=_params("parallel"),
        name="final_norm",
    )(x, gain.reshape(1, d))


def _gate_up_kernel(h_ref, wg_ref, wu_ref, o_ref):
    h = h_ref[...]
    g = _dot(h, wg_ref[...])
    u = _dot(h, wu_ref[...])
    o_ref[...] = (_silu(g) * u).astype(o_ref.dtype)


def _gate_up(h, w_gu, layer, n_rows, tm):
    d = h.shape[1]
    f = w_gu.shape[2] // 2
    tn = _pick(f, (512, 256, 128))
    nj = f // tn
    return pl.pallas_call(
        _gate_up_kernel,
        out_shape=jax.ShapeDtypeStruct((n_rows, f), BF16),
        grid=(n_rows // tm, nj),
        in_specs=[pl.BlockSpec((tm, d), lambda i, j: (i, 0)),
                  pl.BlockSpec((None, d, tn), lambda i, j: (layer, 0, j)),
                  pl.BlockSpec((None, d, tn), lambda i, j: (layer, 0, j + nj))],
        out_specs=pl.BlockSpec((tm, tn), lambda i, j: (i, j)),
        compiler_params=_params("parallel", "arbitrary"),
        name="gate_up",
    )(h, w_gu, w_gu)


def _residual_kernel(a_ref, w_ref, x_ref, g_ref, o_ref, *, coef, n_lat, tm):
    y = _dot(a_ref[...], w_ref[...])
    row = pl.program_id(0) * tm + lax.broadcasted_iota(jnp.int32, (tm, 1), 0)
    gate = jnp.where(row < n_lat, g_ref[0], g_ref[1])
    o_ref[...] = x_ref[...] + (coef * gate) * y


def _residual_matmul(a, w, layer, x, mod_l, k_gate, coef, n_rows, n_lat, tm):
    kdim = a.shape[1]
    d = w.shape[2]
    tn = _pick(d, (512, 256, 128))
    return pl.pallas_call(
        functools.partial(_residual_kernel, coef=coef, n_lat=n_lat, tm=tm),
        out_shape=jax.ShapeDtypeStruct((n_rows, d), F32),
        grid=(n_rows // tm, d // tn),
        in_specs=[pl.BlockSpec((tm, kdim), lambda i, j: (i, 0)),
                  pl.BlockSpec((None, kdim, tn), lambda i, j: (layer, 0, j)),
                  pl.BlockSpec((tm, tn), lambda i, j: (i, j)),
                  pl.BlockSpec((2, None, 1, tn), lambda i, j: (0, k_gate, 0, j))],
        out_specs=pl.BlockSpec((tm, tn), lambda i, j: (i, j)),
        compiler_params=_params("parallel", "arbitrary"),
        name="residual_matmul",
    )(a, w, x, mod_l)


def _matmul_kernel(a_ref, w_ref, o_ref):
    o_ref[...] = _dot(a_ref[...], w_ref[...]).astype(o_ref.dtype)


def _matmul(a, w, layer, tm, tn, out_dtype):
    rows, kdim = a.shape
    n = w.shape[2]
    return pl.pallas_call(
        _matmul_kernel,
        out_shape=jax.ShapeDtypeStruct((rows, n), out_dtype),
        grid=(rows // tm, n // tn),
        in_specs=[pl.BlockSpec((tm, kdim), lambda i, j: (i, 0)),
                  pl.BlockSpec((None, kdim, tn), lambda i, j: (layer, 0, j))],
        out_specs=pl.BlockSpec((tm, tn), lambda i, j: (i, j)),
        compiler_params=_params("parallel", "arbitrary"),
        name="matmul",
    )(a, w)


def _prep_kernel(p_ref, cos_ref, sin_ref, cosc_ref, sinc_ref, gq_ref, gk_ref, gcq_ref, gckv_ref,
                 main_ref, qa_ref, kva_ref, kpe_ref, *, q_rank, kv_rank):
    cos, sin = cos_ref[...], sin_ref[...]
    scale = HEAD_DIM ** -0.5 * LOG2E

    def blk(b):
        return p_ref[:, b * LANES:(b + 1) * LANES]

    def put(b, val):
        main_ref[:, b * LANES:(b + 1) * LANES] = val.astype(main_ref.dtype)

    def head_norm(x, g):
        return x * lax.rsqrt(jnp.mean(x * x, axis=-1, keepdims=True) + NORM_EPS) * g

    for h in range(A_HEADS):
        put(A_Q0 + h, _rope(blk(A_Q0 + h), cos, sin, 32) * scale)
    for h in range(A_KV_HEADS):
        put(A_K0 + h, _rope(blk(A_K0 + h), cos, sin, 32))
        put(A_V0 + h, blk(A_V0 + h))
    for h in range(B_HEADS):
        put(B_Q0 + h, _rope(head_norm(blk(B_Q0 + h), gq_ref[...]), cos, sin, 32) * scale)
    for h in range(B_KV_HEADS):
        put(B_K0 + h, _rope(head_norm(blk(B_K0 + h), gk_ref[...]), cos, sin, 32))
        put(B_V0 + h, blk(B_V0 + h))
    for h in range(D_HEADS):
        put(D_Q0 + h, blk(D_Q0 + h) * scale)
        put(D_K0 + h, blk(D_K0 + h))
        put(D_V0 + h, blk(D_V0 + h))

    def low_rank_norm(col0, width, g_ref, o_ref):
        x = p_ref[:, col0:col0 + width]
        y = x * lax.rsqrt(jnp.mean(x * x, axis=-1, keepdims=True) + NORM_EPS) * g_ref[...]
        o_ref[...] = y.astype(o_ref.dtype)

    low_rank_norm(MAIN_COLS, q_rank, gcq_ref, qa_ref)
    low_rank_norm(MAIN_COLS + q_rank, kv_rank, gckv_ref, kva_ref)
    kpe = p_ref[:, MAIN_COLS + q_rank + kv_rank:MAIN_COLS + q_rank + kv_rank + LANES]
    kpe_ref[...] = _rope(kpe, cosc_ref[...], sinc_ref[...], 16).astype(kpe_ref.dtype)


def _prep(p, tables, g_q, g_k, g_cq, g_ckv, tr):
    rows, cols = p.shape
    q_rank, kv_rank = g_cq.shape[0], g_ckv.shape[0]
    row_spec = lambda w: pl.BlockSpec((tr, w), lambda i: (i, 0))
    vec_spec = lambda w: pl.BlockSpec((1, w), lambda i: (0, 0))
    return pl.pallas_call(
        functools.partial(_prep_kernel, q_rank=q_rank, kv_rank=kv_rank),
        out_shape=(jax.ShapeDtypeStruct((rows, MAIN_COLS), BF16),
                   jax.ShapeDtypeStruct((rows, q_rank), BF16),
                   jax.ShapeDtypeStruct((rows, kv_rank), BF16),
                   jax.ShapeDtypeStruct((rows, LANES), BF16)),
        grid=(rows // tr,),
        in_specs=[row_spec(cols), row_spec(LANES), row_spec(LANES), row_spec(LANES), row_spec(LANES),
                  vec_spec(LANES), vec_spec(LANES), vec_spec(q_rank), vec_spec(kv_rank)],
        out_specs=(row_spec(MAIN_COLS), row_spec(q_rank), row_spec(kv_rank), row_spec(LANES)),
        compiler_params=_params("parallel"),
        name="prep",
    )(p, *tables, g_q.reshape(1, -1), g_k.reshape(1, -1), g_cq.reshape(1, -1), g_ckv.reshape(1, -1))


def _rope_tables(n_lat, n_ctx):
    t = jnp.arange(n_lat, dtype=jnp.int32)
    row, col = (t // GRID_W).astype(F32), (t % GRID_W).astype(F32)
    lane = jnp.arange(LANES)

    def table(dim, live):
        half = dim // 2
        pair = half // 2
        inv = ROPE_THETA ** (-jnp.arange(0, half, 2, dtype=F32) / half)
        inv_lane = inv[lane % pair]
        pos = jnp.where(((lane // half) % 2 == 0)[None, :], row[:, None], col[:, None])
        ang = pos * inv_lane[None, :]
        sign = jnp.where((lane % half) < pair, -1.0, 1.0).astype(F32)
        on = (lane < live)[None, :]
        cos = jnp.where(on, jnp.cos(ang), 1.0)
        sin = jnp.where(on, jnp.sin(ang) * sign[None, :], 0.0)
        pad = lambda a, v: jnp.concatenate([a, jnp.full((n_ctx, LANES), v, F32)], axis=0)
        return pad(cos, 1.0), pad(sin, 0.0)

    cos, sin = table(HEAD_DIM, LANES)
    cosc, sinc = table(C_ROPE_DIM, C_ROPE_DIM)
    return cos, sin, cosc, sinc


def _cq_up_kernel(a_ref, w_ref, cos_ref, sin_ref, o_ref, *, scale):
    y = _dot(a_ref[...], w_ref[...])
    cos, sin = cos_ref[...], sin_ref[...]
    for h in range(C_HEADS):
        c0 = h * C_QK_PAD
        o_ref[:, c0:c0 + LANES] = (y[:, c0:c0 + LANES] * scale).astype(o_ref.dtype)
        pe = _rope(y[:, c0 + LANES:c0 + 2 * LANES], cos, sin, 16)
        o_ref[:, c0 + LANES:c0 + 2 * LANES] = (pe * scale).astype(o_ref.dtype)


def _ckv_up_kernel(a_ref, w_ref, kpe_ref, k_ref, v_ref):
    y = _dot(a_ref[...], w_ref[...])
    kpe = kpe_ref[...]
    for h in range(C_HEADS):
        k_ref[:, h * C_QK_PAD:h * C_QK_PAD + LANES] = y[:, h * LANES:(h + 1) * LANES].astype(k_ref.dtype)
        k_ref[:, h * C_QK_PAD + LANES:(h + 1) * C_QK_PAD] = kpe
    v_ref[...] = y[:, C_HEADS * LANES:].astype(v_ref.dtype)


def _mla_expand(qa, kva, kpe, w_q, w_kv, layer, cosc, sinc, tm):
    rows, q_rank = qa.shape
    kv_rank = kva.shape[1]
    scale = (C_NOPE_DIM + C_ROPE_DIM) ** -0.5 * LOG2E
    row_spec = lambda w: pl.BlockSpec((tm, w), lambda i: (i, 0))
    layer_spec = lambda w: pl.BlockSpec((None,) + w.shape[1:], lambda i: (layer, 0, 0))
    qc = pl.pallas_call(
        functools.partial(_cq_up_kernel, scale=scale),
        out_shape=jax.ShapeDtypeStruct((rows, C_HEADS * C_QK_PAD), BF16),
        grid=(rows // tm,),
        in_specs=[row_spec(q_rank), layer_spec(w_q), row_spec(LANES), row_spec(LANES)],
        out_specs=row_spec(C_HEADS * C_QK_PAD),
        compiler_params=_params("parallel"),
        name="mla_q_up",
    )(qa, w_q, cosc, sinc)
    kc, vc = pl.pallas_call(
        _ckv_up_kernel,
        out_shape=(jax.ShapeDtypeStruct((rows, C_HEADS * C_QK_PAD), BF16),
                   jax.ShapeDtypeStruct((rows, C_HEADS * C_V_DIM), BF16)),
        grid=(rows // tm,),
        in_specs=[row_spec(kv_rank), layer_spec(w_kv), row_spec(LANES)],
        out_specs=(row_spec(C_HEADS * C_QK_PAD), row_spec(C_HEADS * C_V_DIM)),
        compiler_params=_params("parallel"),
        name="mla_kv_up",
    )(kva, w_kv, kpe)
    return qc, kc, vc


def _flash_kernel(q_ref, k_ref, v_ref, buf_ref, o_ref, m_sc, l_sc, acc_sc, *, group, dk, tc, tk, n_lat, n_ctx):
    del buf_ref
    tq = q_ref.shape[0]
    chains = [(g, r) for g in range(group) for r in range(tq // tc)]

    def chunk(k, v, first):
        n_blocks = k.shape[0] // LANES
        for ci, (g, r) in enumerate(chains):
            s = _dot_nt(q_ref[r * tc:(r + 1) * tc, g * dk:(g + 1) * dk], k)
            blocks = [s[:, b * LANES:(b + 1) * LANES] for b in range(n_blocks)]
            mx = blocks[0]
            for blk in blocks[1:]:
                mx = jnp.maximum(mx, blk)
            m_new = jnp.broadcast_to(jnp.max(mx, axis=-1, keepdims=True), (tc, LANES))
            if not first:
                m_prev = m_sc[ci]
                m_new = jnp.maximum(m_prev, m_new)
                alpha = jnp.exp2(m_prev - m_new)
            ps = [jnp.exp2(blk - m_new) for blk in blocks]
            l_new = ps[0]
            for p in ps[1:]:
                l_new = l_new + p
            pv = _dot(jnp.concatenate([p.astype(BF16) for p in ps], axis=1), v)
            if first:
                l_sc[ci] = l_new
                acc_sc[ci] = pv
            else:
                l_sc[ci] = alpha * l_sc[ci] + l_new
                acc_sc[ci] = alpha * acc_sc[ci] + pv
            m_sc[ci] = m_new

    chunk(k_ref[n_lat:n_lat + n_ctx, :], v_ref[n_lat:n_lat + n_ctx, :], True)

    def step(c, carry):
        start = pl.multiple_of(c * tk, tk)
        chunk(k_ref[pl.ds(start, tk), :], v_ref[pl.ds(start, tk), :], False)
        return carry

    lax.fori_loop(0, n_lat // tk, step, 0)
    for ci, (g, r) in enumerate(chains):
        inv = 1.0 / jnp.sum(l_sc[ci], axis=-1, keepdims=True)
        o_ref[r * tc:(r + 1) * tc, g * LANES:(g + 1) * LANES] = (acc_sc[ci] * inv).astype(o_ref.dtype)


def _flash(q_arr, k_arr, v_arr, o_buf, *, n_kv, group, dk, q_blk0, k_blk0, v_blk0, o_blk0, tq, tc, tk, n_lat,
           n_ctx):
    rows = k_arr.shape[0]
    n_chains = group * (tq // tc)
    stat = pltpu.VMEM((n_chains, tc, LANES), F32)
    return pl.pallas_call(
        functools.partial(_flash_kernel, group=group, dk=dk, tc=tc, tk=tk, n_lat=n_lat, n_ctx=n_ctx),
        out_shape=jax.ShapeDtypeStruct(o_buf.shape, o_buf.dtype),
        grid=(n_kv, n_lat // tq),
        in_specs=[pl.BlockSpec((tq, group * dk), lambda g, i: (i, q_blk0 + g)),
                  pl.BlockSpec((rows, dk), lambda g, i: (0, k_blk0 + g)),
                  pl.BlockSpec((rows, LANES), lambda g, i: (0, v_blk0 + g)),
                  _BUF_SPEC],
        out_specs=pl.BlockSpec((tq, group * LANES), lambda g, i: (i, o_blk0 + g)),
        scratch_shapes=[stat, stat, stat],
        input_output_aliases={3: 0},
        compiler_params=_params("parallel", "arbitrary"),
        name="flash",
    )(q_arr, k_arr, v_arr, o_buf)


def _lane_blocks(s):
    return [s[:, b * LANES:(b + 1) * LANES] for b in range(s.shape[1] // LANES)]


def _softmax_pv(segments, sink=None):
    blocks = [blk for blks, _ in segments for blk in blks]
    rows = blocks[0].shape[0]
    mx = blocks[0]
    for blk in blocks[1:]:
        mx = jnp.maximum(mx, blk)
    m = jnp.broadcast_to(jnp.max(mx, axis=-1, keepdims=True), (rows, LANES))
    if sink is not None:
        m = jnp.maximum(m, sink)
    out = l = None
    for blks, v in segments:
        ps = [jnp.exp2(blk - m) for blk in blks]
        for p in ps:
            l = p if l is None else l + p
        pv = _dot(jnp.concatenate([p.astype(BF16) for p in ps], axis=1), v)
        out = pv if out is None else out + pv
    if sink is not None:
        lane = lax.broadcasted_iota(jnp.int32, (rows, LANES), 1)
        l = l + jnp.where(lane == 0, jnp.exp2(sink - m), 0.0)
    return out * (1.0 / jnp.sum(l, axis=-1, keepdims=True))


def _window_kernel(sink_ref, q_ref, k_ref, v_ref, buf_ref, o_ref, *, group, tq, n_lat, n_ctx, window):
    del buf_ref
    g, n = pl.program_id(0), pl.program_id(1)
    span = tq + 2 * LANES
    kc, vc = k_ref[n_lat:n_lat + n_ctx, :], v_ref[n_lat:n_lat + n_ctx, :]
    row = jnp.bitwise_and(lax.broadcasted_iota(jnp.int32, (group * tq, LANES), 0), tq - 1)
    lane_minus_row = lax.broadcasted_iota(jnp.int32, (group * tq, LANES), 1) - row
    sink = jnp.concatenate([jnp.full((tq, LANES), sink_ref[g * group + i], F32) for i in range(group)], axis=0)
    n_sub = q_ref.shape[0] // tq
    for sub in range(n_sub):
        rows = slice(sub * tq, (sub + 1) * tq)
        q0 = (n * n_sub + sub) * tq
        start = pl.multiple_of(jnp.clip(q0 - LANES, 0, n_lat - span), LANES)
        kw, vw = k_ref[pl.ds(start, span), :], v_ref[pl.ds(start, span), :]
        q = jnp.concatenate([q_ref[rows, i * HEAD_DIM:(i + 1) * HEAD_DIM] for i in range(group)], axis=0)
        s_w = [jnp.where(jnp.abs(lane_minus_row + (start - q0 + b * LANES)) <= window, blk, NEG_INF)
               for b, blk in enumerate(_lane_blocks(_dot_nt(q, kw)))]
        out = _softmax_pv([(s_w, vw), (_lane_blocks(_dot_nt(q, kc)), vc)], sink=sink)
        for i in range(group):
            o_ref[rows, i * HEAD_DIM:(i + 1) * HEAD_DIM] = out[i * tq:(i + 1) * tq].astype(o_ref.dtype)


def _window_attention(main, sink, o_buf, n_lat, n_ctx):
    rows = main.shape[0]
    group = A_HEADS // A_KV_HEADS
    tq = _pick(n_lat, (256, 128))
    assert A_WINDOW <= LANES and n_lat >= tq + 2 * LANES
    n_sub = _pick(n_lat // tq, (2, 1))
    return pl.pallas_call(
        functools.partial(_window_kernel, group=group, tq=tq, n_lat=n_lat, n_ctx=n_ctx, window=A_WINDOW),
        out_shape=jax.ShapeDtypeStruct(o_buf.shape, o_buf.dtype),
        grid=(A_KV_HEADS, n_lat // (n_sub * tq)),
        in_specs=[pl.BlockSpec(memory_space=pltpu.SMEM),
                  pl.BlockSpec((n_sub * tq, group * HEAD_DIM), lambda g, n: (n, g)),
                  pl.BlockSpec((rows, HEAD_DIM), lambda g, n: (0, A_K0 + g)),
                  pl.BlockSpec((rows, HEAD_DIM), lambda g, n: (0, A_V0 + g)),
                  _BUF_SPEC],
        out_specs=pl.BlockSpec((n_sub * tq, group * HEAD_DIM), lambda g, n: (n, MIX_A0 // group + g)),
        input_output_aliases={4: 0},
        compiler_params=_params("parallel", "arbitrary"),
        name="window_attention",
    )(sink, main, main, main, o_buf)


NBR_Q_ROWS = 8
NBR_KEY_ROWS = 16


def _nbr_key_start(rb, n_grid_rows):
    return jnp.clip(rb * NBR_Q_ROWS - NA_KH // 2, 0, n_grid_rows - NBR_KEY_ROWS)


def _nbr_kernel(q_ref, k_ref, v_ref, b_ref, buf_ref, o_ref, *, chain_rows, n_grid_rows, n_lat, n_ctx):
    del buf_ref
    k0 = pl.multiple_of(_nbr_key_start(pl.program_id(1), n_grid_rows) * GRID_W, (NA_KH // 2) * GRID_W)
    n_keys = NBR_KEY_ROWS * GRID_W
    kw, vw = k_ref[pl.ds(k0, n_keys), :], v_ref[pl.ds(k0, n_keys), :]
    kc, vc = k_ref[n_lat:n_lat + n_ctx, :], v_ref[n_lat:n_lat + n_ctx, :]
    for sub in range(q_ref.shape[0] // chain_rows):
        rows = slice(sub * chain_rows, (sub + 1) * chain_rows)
        q = q_ref[rows, :]
        s_nb = _dot_nt(q, kw) + b_ref[rows, :]
        out = _softmax_pv([(_lane_blocks(s_nb), vw), (_lane_blocks(_dot_nt(q, kc)), vc)])
        o_ref[rows, :] = out.astype(o_ref.dtype)


def _nbr_bias_table(rel_bias):
    n_heads = rel_bias.shape[0]
    off = jnp.array([0, NA_KH // 2, NA_KH], jnp.int32)[:, None, None]
    j = jnp.arange(NBR_Q_ROWS)[None, :, None]
    kr = jnp.arange(NBR_KEY_ROWS)[None, None, :]
    centred = j - NA_KH // 2
    rs = jnp.stack([jnp.maximum(centred[0], 0), centred[0], jnp.minimum(centred[0], 0)])
    key_row = kr - off
    row_ok = (key_row >= rs) & (key_row < rs + NA_KH)
    row_sel = jax.nn.one_hot(key_row - j + (NA_KH - 1), 2 * NA_KH - 1, dtype=F32)
    c = jnp.arange(GRID_W)[:, None]
    kc = jnp.arange(GRID_W)[None, :]
    cstart = jnp.clip(c - NA_KW // 2, 0, GRID_W - NA_KW)
    col_ok = (kc >= cstart) & (kc < cstart + NA_KW)
    col_sel = jax.nn.one_hot(kc - c + (NA_KW - 1), 2 * NA_KW - 1, dtype=F32)
    bias = jnp.einsum("vjka,hab,cqb->vhjckq", row_sel, rel_bias.astype(F32) * LOG2E, col_sel,
                      precision=lax.Precision.HIGHEST)
    ok = row_ok[:, None, :, None, :, None] & col_ok[None, None, None, :, None, :]
    bias = jnp.where(ok, bias, NEG_INF)
    return bias.reshape(3, n_heads, NBR_Q_ROWS * GRID_W, NBR_KEY_ROWS * GRID_W)


def _nbr_attention(main, rel_bias, o_buf, n_lat, n_ctx):
    rows = main.shape[0]
    n_grid_rows = n_lat // GRID_W
    assert n_grid_rows % NBR_Q_ROWS == 0 and n_grid_rows >= NBR_KEY_ROWS
    assert NBR_KEY_ROWS >= NBR_Q_ROWS + NA_KH and NA_KH // 2 * 2 == NA_KH
    n_blocks = n_grid_rows // NBR_Q_ROWS
    tq = NBR_Q_ROWS * GRID_W

    def variant(rb):
        return (rb * NBR_Q_ROWS - _nbr_key_start(rb, n_grid_rows)) // (NA_KH // 2)

    return pl.pallas_call(
        functools.partial(_nbr_kernel, chain_rows=tq // 2, n_grid_rows=n_grid_rows, n_lat=n_lat, n_ctx=n_ctx),
        out_shape=jax.ShapeDtypeStruct(o_buf.shape, o_buf.dtype),
        grid=(D_HEADS, n_blocks),
        in_specs=[pl.BlockSpec((tq, HEAD_DIM), lambda h, rb: (rb, D_Q0 + h)),
                  pl.BlockSpec((rows, HEAD_DIM), lambda h, rb: (0, D_K0 + h)),
                  pl.BlockSpec((rows, HEAD_DIM), lambda h, rb: (0, D_V0 + h)),
                  pl.BlockSpec((None, None, tq, NBR_KEY_ROWS * GRID_W), lambda h, rb: (variant(rb), h, 0, 0)),
                  _BUF_SPEC],
        out_specs=pl.BlockSpec((tq, HEAD_DIM), lambda h, rb: (rb, MIX_D0 + h)),
        input_output_aliases={4: 0},
        compiler_params=_params("parallel", "arbitrary"),
        name="nbr_attention",
    )(main, main, main, _nbr_bias_table(rel_bias), o_buf)


def _ctx_attn_kernel(sink_ref, q_ref, k_ref, v_ref, buf_ref, o_ref):
    del buf_ref
    sink = sink_ref[pl.program_id(0)]
    s = _dot_nt(q_ref[...], k_ref[...])
    m = jnp.maximum(jnp.max(s, axis=-1, keepdims=True), sink)
    p = jnp.exp2(s - m)
    denom = jnp.sum(p, axis=-1, keepdims=True) + jnp.exp2(sink - m)
    o_ref[...] = (_dot(p.astype(BF16), v_ref[...]) * (1.0 / denom)).astype(o_ref.dtype)


def _ctx_attention(q_arr, k_arr, v_arr, sink, o_buf, *, n_heads, group, dk, q_blk0, k_blk0, v_blk0, o_blk0, n_lat,
                   n_ctx):
    rb = n_lat // n_ctx
    return pl.pallas_call(
        _ctx_attn_kernel,
        out_shape=jax.ShapeDtypeStruct(o_buf.shape, o_buf.dtype),
        grid=(n_heads,),
        in_specs=[pl.BlockSpec(memory_space=pltpu.SMEM),
                  pl.BlockSpec((n_ctx, dk), lambda h: (rb, q_blk0 + h)),
                  pl.BlockSpec((n_ctx, dk), lambda h: (rb, k_blk0 + h // group)),
                  pl.BlockSpec((n_ctx, LANES), lambda h: (rb, v_blk0 + h // group)),
                  _BUF_SPEC],
        out_specs=pl.BlockSpec((n_ctx, LANES), lambda h: (rb, o_blk0 + h)),
        input_output_aliases={4: 0},
        compiler_params=_params("parallel"),
        name="ctx_attention",
    )(sink, q_arr, k_arr, v_arr, o_buf)


def _relayout_w_in(w, q_rank, kv_rank, total_cols):
    a_cols = (A_HEADS + 2 * A_KV_HEADS) * HEAD_DIM
    b_cols = (B_HEADS + 2 * B_KV_HEADS) * HEAD_DIM
    c_cols = q_rank + kv_rank + C_ROPE_DIM
    o2, o3 = a_cols + b_cols, a_cols + b_cols + c_cols
    pad = jnp.zeros(w.shape[:-1] + (total_cols - w.shape[-1],), BF16)
    w = w.astype(BF16)
    return jnp.concatenate([w[..., :o2], w[..., o3:], w[..., o2:o3], pad], axis=-1)


def _relayout_w_q_up(w):
    lead = w.shape[:-1]
    w = w.astype(BF16).reshape(lead + (C_HEADS, C_NOPE_DIM + C_ROPE_DIM))
    w = jnp.pad(w, ((0, 0),) * (len(lead) + 1) + ((0, C_QK_PAD - C_NOPE_DIM - C_ROPE_DIM),))
    return w.reshape(lead + (C_HEADS * C_QK_PAD,))


def _relayout_w_kv_up(w):
    lead = w.shape[:-1]
    w = w.astype(BF16).reshape(lead + (C_HEADS, C_NOPE_DIM + C_V_DIM))
    return jnp.concatenate([w[..., :C_NOPE_DIM].reshape(lead + (-1,)), w[..., C_NOPE_DIM:].reshape(lead + (-1,))],
                           axis=-1)


def kernel(x, c, ctx, c_ctx, w_mod_down, w_mod_up, norm_ffn1, ffn1_w_gu, ffn1_w_down, norm_mix, w_in,
           a_sink, b_q_norm, b_k_norm, c_q_norm, c_kv_norm, c_w_q_up, c_w_kv_up, d_rel_bias, w_out,
           norm_ffn2, ffn2_w_gu, ffn2_w_down, final_norm):
    bsz, n_lat, d = x.shape
    n_ctx = ctx.shape[1]
    n_layers = w_in.shape[0]
    q_rank, kv_rank = c_q_norm.shape[1], c_kv_norm.shape[1]
    assert bsz == 1 and n_lat % GRID_W == 0 and n_lat % n_ctx == 0 and n_ctx % LANES == 0
    rows = n_lat + n_ctx
    tr = n_ctx
    tm_all = _pick(rows, (768, 512, 256, 128))
    tm_lat = _pick(n_lat, (1024, 512, 256, 128))
    in_cols = MAIN_COLS + q_rank + kv_rank + LANES
    in_cols = -(-in_cols // 512) * 512
    flash_tc = _pick(n_lat, (1024, 512, 256, 128))
    flash_tk = _pick(n_lat, (1024, 512, 256, 128))

    mod = _modulation(c, c_ctx, w_mod_down, w_mod_up)
    tables = _rope_tables(n_lat, n_ctx)
    no_sink = jnp.full((max(B_HEADS, C_HEADS, D_HEADS),), NEG_INF, F32)

    w_gu1, w_dn1 = ffn1_w_gu.astype(BF16), ffn1_w_down.astype(BF16)
    w_gu2, w_dn2 = ffn2_w_gu.astype(BF16), ffn2_w_down.astype(BF16)
    w_in_b = _relayout_w_in(w_in, q_rank, kv_rank, in_cols)
    w_out_b = w_out.astype(BF16)
    w_q_up_b, w_kv_up_b = _relayout_w_q_up(c_w_q_up), _relayout_w_kv_up(c_w_kv_up)

    xs = jnp.concatenate([x[0], ctx[0]], axis=0)

    def ffn(xs, gain, w_gu, w_down, l, mod_l, k0, n_rows, tm):
        h = _norm_mod(xs, gain, mod_l, k0, k0 + 1, n_lat, tr)
        act = _gate_up(h, w_gu, l, n_rows, tm)
        return _residual_matmul(act, w_down, l, xs, mod_l, k0 + 2, 0.5, n_rows, n_lat, tm)

    for l in range(n_layers):
        need_ctx = l < n_layers - 1
        n_rows, tm = (rows, tm_all) if need_ctx else (n_lat, tm_lat)
        mod_l = mod[l]
        xs = ffn(xs, norm_ffn1[l], w_gu1, w_dn1, l, mod_l, 0, rows, tm_all)

        h = _norm_mod(xs, norm_mix[l], mod_l, 3, 4, n_lat, tr)
        p = _matmul(h, w_in_b, l, tm_all, 512, F32)
        main, qa, kva, kpe = _prep(p, tables, b_q_norm[l], b_k_norm[l], c_q_norm[l], c_kv_norm[l], tr)
        qc, kc, vc = _mla_expand(qa, kva, kpe, w_q_up_b, w_kv_up_b, l, tables[2], tables[3], tr)

        sink = a_sink[l] * LOG2E
        lat = dict(n_lat=n_lat, n_ctx=n_ctx)
        o = jnp.zeros((n_rows, MIX_COLS), BF16)
        o = _window_attention(main, sink, o, **lat)
        o = _flash(main, main, main, o, n_kv=B_KV_HEADS, group=B_HEADS // B_KV_HEADS, dk=HEAD_DIM,
                   q_blk0=B_Q0 // (B_HEADS // B_KV_HEADS), k_blk0=B_K0, v_blk0=B_V0,
                   o_blk0=MIX_B0 // (B_HEADS // B_KV_HEADS), tq=flash_tc, tc=flash_tc, tk=flash_tk, **lat)
        o = _flash(qc, kc, vc, o, n_kv=C_HEADS, group=1, dk=C_QK_PAD, q_blk0=0, k_blk0=0, v_blk0=0, o_blk0=MIX_C0,
                   tq=_pick(n_lat, (2 * flash_tc, flash_tc)), tc=flash_tc, tk=flash_tk, **lat)
        o = _nbr_attention(main, d_rel_bias[l], o, **lat)
        if need_ctx:
            o = _ctx_attention(main, main, main, sink, o, n_heads=A_HEADS, group=A_HEADS // A_KV_HEADS,
                               dk=HEAD_DIM, q_blk0=A_Q0, k_blk0=A_K0, v_blk0=A_V0, o_blk0=MIX_A0, **lat)
            o = _ctx_attention(main, main, main, no_sink, o, n_heads=B_HEADS, group=B_HEADS // B_KV_HEADS,
                               dk=HEAD_DIM, q_blk0=B_Q0, k_blk0=B_K0, v_blk0=B_V0, o_blk0=MIX_B0, **lat)
            o = _ctx_attention(qc, kc, vc, no_sink, o, n_heads=C_HEADS, group=1, dk=C_QK_PAD,
                               q_blk0=0, k_blk0=0, v_blk0=0, o_blk0=MIX_C0, **lat)
            o = _ctx_attention(main, main, main, no_sink, o, n_heads=D_HEADS, group=1, dk=HEAD_DIM,
                               q_blk0=D_Q0, k_blk0=D_K0, v_blk0=D_V0, o_blk0=MIX_D0, **lat)

        xs = _residual_matmul(o, w_out_b, l, xs, mod_l, 5, 1.0, n_rows, n_lat, tm)
        xs = ffn(xs, norm_ffn2[l], w_gu2, w_dn2, l, mod_l, 6, n_rows, tm)

    return _final_norm(xs[:n_lat], final_norm, tr)[None]
```

```python
import functools

import jax
import jax.numpy as jnp
from jax import lax
from jax.experimental import pallas as pl
from jax.experimental.pallas import tpu as pltpu

F32 = jnp.float32
BF16 = jnp.bfloat16

GRID_W = 64
HEAD_DIM = 128
ROPE_THETA = 10000.0
NORM_EPS = 1e-6
NEG_INF = -1e30
LOG2E = 1.4426950408889634
N_MOD = 9
A_HEADS, A_KV_HEADS, A_WINDOW = 8, 2, 128
B_HEADS, B_KV_HEADS = 8, 2
C_HEADS, C_NOPE_DIM, C_ROPE_DIM, C_V_DIM = 8, 128, 64, 128
D_HEADS, NA_KH, NA_KW = 8, 8, 16

LANES = 128
VMEM_LIMIT_BYTES = 56 * 1024 * 1024
C_QK_PAD = 2 * LANES

A_Q0, A_K0, A_V0 = 0, 8, 10
B_Q0, B_K0, B_V0 = 12, 20, 22
D_Q0, D_K0, D_V0 = 24, 32, 40
MAIN_COLS = 48 * LANES
MIX_A0, MIX_B0, MIX_C0, MIX_D0 = 0, 8, 16, 24
MIX_COLS = 32 * LANES
_BUF_SPEC = pl.BlockSpec(memory_space=pl.ANY)


def _params(*sem):
    return pltpu.CompilerParams(dimension_semantics=sem, vmem_limit_bytes=VMEM_LIMIT_BYTES)


def _pick(n, prefs):
    for p in prefs:
        if n % p == 0:
            return p
    raise ValueError(f"no tile in {prefs} divides {n}")


def _dot(a, b):
    return jnp.dot(a, b, preferred_element_type=F32)


def _dot_nt(a, b):
    return lax.dot_general(a, b, (((1,), (1,)), ((), ())), preferred_element_type=F32)


def _silu(x):
    return x / (1.0 + jnp.exp(-x))


def _rope(x, cos, sin_signed, half):
    n = x.shape[-1]
    lane = lax.broadcasted_iota(jnp.int32, x.shape, x.ndim - 1)
    first = jnp.bitwise_and(lane, 2 * half - 1) < half
    rot = jnp.where(first, pltpu.roll(x, n - half, x.ndim - 1), pltpu.roll(x, half, x.ndim - 1))
    return x * cos + rot * sin_signed


def _mod_down_kernel(s_ref, w_ref, o_ref, acc_ref):
    k = pl.program_id(1)

    @pl.when(k == 0)
    def _():
        acc_ref[...] = jnp.zeros_like(acc_ref)

    acc_ref[...] += _dot(_silu(s_ref[...]).astype(BF16), w_ref[...].astype(BF16))

    @pl.when(k == pl.num_programs(1) - 1)
    def _():
        o_ref[...] = acc_ref[...]


def _mod_up_kernel(t_ref, w_ref, o_ref):
    o_ref[...] = _dot(t_ref[...].astype(BF16), w_ref[...].astype(BF16))


def _modulation(c, c_ctx, w_down, w_up):
    n_layers, d, rank = w_down.shape
    n_out = w_up.shape[2]
    s = jnp.zeros((8, d), F32).at[0].set(c[0]).at[1].set(c_ctx)
    tk = _pick(d, (1024, 512, 256, 128))
    t = pl.pallas_call(
        _mod_down_kernel,
        out_shape=jax.ShapeDtypeStruct((n_layers, 8, rank), F32),
        grid=(n_layers, d // tk),
        in_specs=[pl.BlockSpec((8, tk), lambda l, k: (0, k)),
                  pl.BlockSpec((None, tk, rank), lambda l, k: (l, k, 0))],
        out_specs=pl.BlockSpec((None, 8, rank), lambda l, k: (l, 0, 0)),
        scratch_shapes=[pltpu.VMEM((8, rank), F32)],
        compiler_params=_params("parallel", "arbitrary"),
        name="mod_down",
    )(s, w_down)
    tn = _pick(n_out, (2048, 1024, 512, 256, 128))
    m = pl.pallas_call(
        _mod_up_kernel,
        out_shape=jax.ShapeDtypeStruct((n_layers, 8, n_out), F32),
        grid=(n_layers, n_out // tn),
        in_specs=[pl.BlockSpec((None, 8, rank), lambda l, j: (l, 0, 0)),
                  pl.BlockSpec((None, rank, tn), lambda l, j: (l, 0, j))],
        out_specs=pl.BlockSpec((None, 8, tn), lambda l, j: (l, 0, j)),
        compiler_params=_params("parallel", "parallel"),
        name="mod_up",
    )(t, w_up)
    return m[:, :2, :].reshape(n_layers, 2, N_MOD, 1, d)


def _norm_mod_kernel(x_ref, gain_ref, shift_ref, scale_ref, o_ref):
    x = x_ref[...]
    y = x * lax.rsqrt(jnp.mean(x * x, axis=-1, keepdims=True) + NORM_EPS) * gain_ref[...]
    o_ref[...] = (y * (1.0 + scale_ref[...]) + shift_ref[...]).astype(o_ref.dtype)


def _norm_mod(x, gain, mod_l, k_shift, k_scale, n_lat, tr):
    rows, d = x.shape
    n_lat_blocks = n_lat // tr

    def mod_spec(k):
        return pl.BlockSpec((None, None, 1, d),
                            lambda i: (jnp.where(i >= n_lat_blocks, 1, 0), k, 0, 0))

    return pl.pallas_call(
        _norm_mod_kernel,
        out_shape=jax.ShapeDtypeStruct((rows, d), BF16),
        grid=(rows // tr,),
        in_specs=[pl.BlockSpec((tr, d), lambda i: (i, 0)),
                  pl.BlockSpec((1, d), lambda i: (0, 0)),
                  mod_spec(k_shift), mod_spec(k_scale)],
        out_specs=pl.BlockSpec((tr, d), lambda i: (i, 0)),
        compiler_params=_params("parallel"),
        name="norm_mod",
    )(x, gain.reshape(1, d), mod_l, mod_l)


def _final_norm_kernel(x_ref, gain_ref, o_ref):
    x = x_ref[...]
    o_ref[...] = x * lax.rsqrt(jnp.mean(x * x, axis=-1, keepdims=True) + NORM_EPS) * gain_ref[...]


def _final_norm(x, gain, tr):
    rows, d = x.shape
    return pl.pallas_call(
        _final_norm_kernel,
        out_shape=jax.ShapeDtypeStruct((rows, d), F32),
        grid=(rows // tr,),
        in_specs=[pl.BlockSpec((tr, d), lambda i: (i, 0)),
                  pl.BlockSpec((1, d), lambda i: (0, 0))],
        out_specs=pl.BlockSpec((tr, d), lambda i: (i, 0)),
        compiler_params=_params("parallel"),
        name="final_norm",
    )(x, gain.reshape(1, d))


def _gate_up_kernel(h_ref, wg_ref, wu_ref, o_ref):
    h = h_ref[...]
    g = _dot(h, wg_ref[...].astype(BF16))
    u = _dot(h, wu_ref[...].astype(BF16))
    o_ref[...] = (_silu(g) * u).astype(o_ref.dtype)


def _gate_up(h, w_gu, layer, n_rows, tm):
    d = h.shape[1]
    f = w_gu.shape[2] // 2
    tn = _pick(f, (256, 128))
    nj = f // tn
    return pl.pallas_call(
        _gate_up_kernel,
        out_shape=jax.ShapeDtypeStruct((n_rows, f), BF16),
        grid=(n_rows // tm, nj),
        in_specs=[pl.BlockSpec((tm, d), lambda i, j: (i, 0)),
                  pl.BlockSpec((None, d, tn), lambda i, j: (layer, 0, j)),
                  pl.BlockSpec((None, d, tn), lambda i, j: (layer, 0, j + nj))],
        out_specs=pl.BlockSpec((tm, tn), lambda i, j: (i, j)),
        compiler_params=_params("parallel", "arbitrary"),
        name="gate_up",
    )(h, w_gu, w_gu)


def _residual_kernel(a_ref, w_ref, x_ref, g_ref, o_ref, *, coef, n_lat, tm):
    y = _dot(a_ref[...], w_ref[...].astype(BF16))
    row = pl.program_id(0) * tm + lax.broadcasted_iota(jnp.int32, (tm, 1), 0)
    gate = jnp.where(row < n_lat, g_ref[0], g_ref[1])
    o_ref[...] = x_ref[...] + (coef * gate) * y


def _residual_matmul(a, w, layer, x, mod_l, k_gate, coef, n_rows, n_lat, tm):
    kdim = a.shape[1]
    d = w.shape[2]
    tn = _pick(d, (256, 128))
    return pl.pallas_call(
        functools.partial(_residual_kernel, coef=coef, n_lat=n_lat, tm=tm),
        out_shape=jax.ShapeDtypeStruct((n_rows, d), F32),
        grid=(n_rows // tm, d // tn),
        in_specs=[pl.BlockSpec((tm, kdim), lambda i, j: (i, 0)),
                  pl.BlockSpec((None, kdim, tn), lambda i, j: (layer, 0, j)),
                  pl.BlockSpec((tm, tn), lambda i, j: (i, j)),
                  pl.BlockSpec((2, None, 1, tn), lambda i, j: (0, k_gate, 0, j))],
        out_specs=pl.BlockSpec((tm, tn), lambda i, j: (i, j)),
        compiler_params=_params("parallel", "arbitrary"),
        name="residual_matmul",
    )(a, w, x, mod_l)


IN_TILE_BLOCKS = 4
_MAIN_OPS = (("rope_scale",) * A_HEADS + ("rope",) * A_KV_HEADS + ("copy",) * A_KV_HEADS
             + ("norm_q_rope_scale",) * B_HEADS + ("norm_k_rope",) * B_KV_HEADS + ("copy",) * B_KV_HEADS
             + ("scale",) * D_HEADS + ("copy",) * (2 * D_HEADS))


def _head_norm(x, g):
    return x * lax.rsqrt(jnp.mean(x * x, axis=-1, keepdims=True) + NORM_EPS) * g


def _in_proj_kernel(h_ref, w_ref, cos_ref, sin_ref, gq_ref, gk_ref, o_ref):
    j = pl.program_id(1)
    y = _dot(h_ref[...], w_ref[...])
    scale = HEAD_DIM ** -0.5 * LOG2E

    def apply(op, x):
        if op in ("norm_q_rope_scale", "norm_k_rope"):
            x = _head_norm(x, (gq_ref if op == "norm_q_rope_scale" else gk_ref)[...])
        if "rope" in op:
            x = _rope(x, cos_ref[...], sin_ref[...], 32)
        return x * scale if "scale" in op else x

    n_tiles = len(_MAIN_OPS) // IN_TILE_BLOCKS
    recipes = [_MAIN_OPS[t * IN_TILE_BLOCKS:(t + 1) * IN_TILE_BLOCKS] for t in range(n_tiles)]
    for recipe in sorted(set(recipes)):
        tiles = [t for t in range(n_tiles) if recipes[t] == recipe]
        hit = functools.reduce(jnp.logical_or, [j == t for t in tiles])

        @pl.when(hit)
        def _(recipe=recipe):
            for b, op in enumerate(recipe):
                cols = slice(b * LANES, (b + 1) * LANES)
                o_ref[:, cols] = apply(op, y[:, cols]).astype(o_ref.dtype)


def _in_proj(h, w_main, layer, tables, g_q, g_k, tm):
    rows, d = h.shape
    tn = IN_TILE_BLOCKS * LANES
    assert w_main.shape[2] == MAIN_COLS and MAIN_COLS % tn == 0
    return pl.pallas_call(
        _in_proj_kernel,
        out_shape=jax.ShapeDtypeStruct((rows, MAIN_COLS), BF16),
        grid=(rows // tm, MAIN_COLS // tn),
        in_specs=[pl.BlockSpec((tm, d), lambda i, j: (i, 0)),
                  pl.BlockSpec((None, d, tn), lambda i, j: (layer, 0, j)),
                  pl.BlockSpec((tm, LANES), lambda i, j: (i, 0)),
                  pl.BlockSpec((tm, LANES), lambda i, j: (i, 0)),
                  pl.BlockSpec((1, LANES), lambda i, j: (0, 0)),
                  pl.BlockSpec((1, LANES), lambda i, j: (0, 0))],
        out_specs=pl.BlockSpec((tm, tn), lambda i, j: (i, j)),
        compiler_params=_params("parallel", "arbitrary"),
        name="in_proj",
    )(h, w_main, tables[0], tables[1], g_q.reshape(1, -1), g_k.reshape(1, -1))


def _mla_down_kernel(h_ref, w_ref, cosc_ref, sinc_ref, gq_ref, gkv_ref, qa_ref, kva_ref, kpe_ref):
    q_rank, kv_rank = qa_ref.shape[1], kva_ref.shape[1]
    y = _dot(h_ref[...], w_ref[...])
    qa_ref[...] = _head_norm(y[:, :q_rank], gq_ref[...]).astype(qa_ref.dtype)
    kva_ref[...] = _head_norm(y[:, q_rank:q_rank + kv_rank], gkv_ref[...]).astype(kva_ref.dtype)
    kpe = y[:, q_rank + kv_rank:q_rank + kv_rank + LANES]
    kpe_ref[...] = _rope(kpe, cosc_ref[...], sinc_ref[...], 16).astype(kpe_ref.dtype)


def _mla_down(h, w_c, layer, tables, g_cq, g_ckv, tm):
    rows, d = h.shape
    q_rank, kv_rank = g_cq.shape[0], g_ckv.shape[0]
    cols = w_c.shape[2]
    row_spec = lambda w: pl.BlockSpec((tm, w), lambda i: (i, 0))
    vec_spec = lambda w: pl.BlockSpec((1, w), lambda i: (0, 0))
    return pl.pallas_call(
        _mla_down_kernel,
        out_shape=(jax.ShapeDtypeStruct((rows, q_rank), BF16),
                   jax.ShapeDtypeStruct((rows, kv_rank), BF16),
                   jax.ShapeDtypeStruct((rows, LANES), BF16)),
        grid=(rows // tm,),
        in_specs=[row_spec(d), pl.BlockSpec((None, d, cols), lambda i: (layer, 0, 0)),
                  row_spec(LANES), row_spec(LANES), vec_spec(q_rank), vec_spec(kv_rank)],
        out_specs=(row_spec(q_rank), row_spec(kv_rank), row_spec(LANES)),
        compiler_params=_params("parallel"),
        name="mla_down",
    )(h, w_c, tables[2], tables[3], g_cq.reshape(1, -1), g_ckv.reshape(1, -1))


def _rope_tables(n_lat, n_ctx):
    t = jnp.arange(n_lat, dtype=jnp.int32)
    row, col = (t // GRID_W).astype(F32), (t % GRID_W).astype(F32)
    lane = jnp.arange(LANES)

    def table(dim, live):
        half = dim // 2
        pair = half // 2
        inv = ROPE_THETA ** (-jnp.arange(0, half, 2, dtype=F32) / half)
        inv_lane = inv[lane % pair]
        pos = jnp.where(((lane // half) % 2 == 0)[None, :], row[:, None], col[:, None])
        ang = pos * inv_lane[None, :]
        sign = jnp.where((lane % half) < pair, -1.0, 1.0).astype(F32)
        on = (lane < live)[None, :]
        cos = jnp.where(on, jnp.cos(ang), 1.0)
        sin = jnp.where(on, jnp.sin(ang) * sign[None, :], 0.0)
        pad = lambda a, v: jnp.concatenate([a, jnp.full((n_ctx, LANES), v, F32)], axis=0)
        return pad(cos, 1.0), pad(sin, 0.0)

    cos, sin = table(HEAD_DIM, LANES)
    cosc, sinc = table(C_ROPE_DIM, C_ROPE_DIM)
    return cos, sin, cosc, sinc


def _cq_up_kernel(a_ref, w_ref, cos_ref, sin_ref, o_ref, *, scale):
    y = _dot(a_ref[...], w_ref[...])
    cos, sin = cos_ref[...], sin_ref[...]
    for h in range(C_HEADS):
        c0 = h * C_QK_PAD
        o_ref[:, c0:c0 + LANES] = (y[:, c0:c0 + LANES] * scale).astype(o_ref.dtype)
        pe = _rope(y[:, c0 + LANES:c0 + 2 * LANES], cos, sin, 16)
        o_ref[:, c0 + LANES:c0 + 2 * LANES] = (pe * scale).astype(o_ref.dtype)


def _ckv_up_kernel(a_ref, w_ref, kpe_ref, k_ref, v_ref):
    y = _dot(a_ref[...], w_ref[...])
    kpe = kpe_ref[...]
    for h in range(C_HEADS):
        k_ref[:, h * C_QK_PAD:h * C_QK_PAD + LANES] = y[:, h * LANES:(h + 1) * LANES].astype(k_ref.dtype)
        k_ref[:, h * C_QK_PAD + LANES:(h + 1) * C_QK_PAD] = kpe
    v_ref[...] = y[:, C_HEADS * LANES:].astype(v_ref.dtype)


def _mla_expand(qa, kva, kpe, w_q, w_kv, layer, cosc, sinc, tm):
    rows, q_rank = qa.shape
    kv_rank = kva.shape[1]
    scale = (C_NOPE_DIM + C_ROPE_DIM) ** -0.5 * LOG2E
    row_spec = lambda w: pl.BlockSpec((tm, w), lambda i: (i, 0))
    layer_spec = lambda w: pl.BlockSpec((None,) + w.shape[1:], lambda i: (layer, 0, 0))
    qc = pl.pallas_call(
        functools.partial(_cq_up_kernel, scale=scale),
        out_shape=jax.ShapeDtypeStruct((rows, C_HEADS * C_QK_PAD), BF16),
        grid=(rows // tm,),
        in_specs=[row_spec(q_rank), layer_spec(w_q), row_spec(LANES), row_spec(LANES)],
        out_specs=row_spec(C_HEADS * C_QK_PAD),
        compiler_params=_params("parallel"),
        name="mla_q_up",
    )(qa, w_q, cosc, sinc)
    kc, vc = pl.pallas_call(
        _ckv_up_kernel,
        out_shape=(jax.ShapeDtypeStruct((rows, C_HEADS * C_QK_PAD), BF16),
                   jax.ShapeDtypeStruct((rows, C_HEADS * C_V_DIM), BF16)),
        grid=(rows // tm,),
        in_specs=[row_spec(kv_rank), layer_spec(w_kv), row_spec(LANES)],
        out_specs=(row_spec(C_HEADS * C_QK_PAD), row_spec(C_HEADS * C_V_DIM)),
        compiler_params=_params("parallel"),
        name="mla_kv_up",
    )(kva, w_kv, kpe)
    return qc, kc, vc


def _flash_kernel(q_ref, k_ref, v_ref, buf_ref, o_ref, m_sc, l_sc, acc_sc, *, group, dk, tc, tk, n_lat, n_ctx):
    del buf_ref
    tq = q_ref.shape[0]
    chains = [(g, r) for g in range(group) for r in range(tq // tc)]

    def chunk(k, v, first):
        n_blocks = k.shape[0] // LANES
        for ci, (g, r) in enumerate(chains):
            s = _dot_nt(q_ref[r * tc:(r + 1) * tc, g * dk:(g + 1) * dk], k)
            blocks = [s[:, b * LANES:(b + 1) * LANES] for b in range(n_blocks)]
            mx = blocks[0]
            for blk in blocks[1:]:
                mx = jnp.maximum(mx, blk)
            m_new = jnp.broadcast_to(jnp.max(mx, axis=-1, keepdims=True), (tc, LANES))
            if not first:
                m_prev = m_sc[ci]
                m_new = jnp.maximum(m_prev, m_new)
                alpha = jnp.exp2(m_prev - m_new)
            ps = [jnp.exp2(blk - m_new) for blk in blocks]
            l_new = ps[0]
            for p in ps[1:]:
                l_new = l_new + p
            pv = _dot(jnp.concatenate([p.astype(BF16) for p in ps], axis=1), v)
            if first:
                l_sc[ci] = l_new
                acc_sc[ci] = pv
            else:
                l_sc[ci] = alpha * l_sc[ci] + l_new
                acc_sc[ci] = alpha * acc_sc[ci] + pv
            m_sc[ci] = m_new

    chunk(k_ref[n_lat:n_lat + n_ctx, :], v_ref[n_lat:n_lat + n_ctx, :], True)

    def step(c, carry):
        start = pl.multiple_of(c * tk, tk)
        chunk(k_ref[pl.ds(start, tk), :], v_ref[pl.ds(start, tk), :], False)
        return carry

    lax.fori_loop(0, n_lat // tk, step, 0)
    for ci, (g, r) in enumerate(chains):
        inv = 1.0 / jnp.sum(l_sc[ci], axis=-1, keepdims=True)
        o_ref[r * tc:(r + 1) * tc, g * LANES:(g + 1) * LANES] = (acc_sc[ci] * inv).astype(o_ref.dtype)


def _flash(q_arr, k_arr, v_arr, o_buf, *, n_kv, group, dk, q_blk0, k_blk0, v_blk0, o_blk0, tq, tc, tk, n_lat,
           n_ctx):
    rows = k_arr.shape[0]
    n_chains = group * (tq // tc)
    stat = pltpu.VMEM((n_chains, tc, LANES), F32)
    return pl.pallas_call(
        functools.partial(_flash_kernel, group=group, dk=dk, tc=tc, tk=tk, n_lat=n_lat, n_ctx=n_ctx),
        out_shape=jax.ShapeDtypeStruct(o_buf.shape, o_buf.dtype),
        grid=(n_kv, n_lat // tq),
        in_specs=[pl.BlockSpec((tq, group * dk), lambda g, i: (i, q_blk0 + g)),
                  pl.BlockSpec((rows, dk), lambda g, i: (0, k_blk0 + g)),
                  pl.BlockSpec((rows, LANES), lambda g, i: (0, v_blk0 + g)),
                  _BUF_SPEC],
        out_specs=pl.BlockSpec((tq, group * LANES), lambda g, i: (i, o_blk0 + g)),
        scratch_shapes=[stat, stat, stat],
        input_output_aliases={3: 0},
        compiler_params=_params("parallel", "arbitrary"),
        name="flash",
    )(q_arr, k_arr, v_arr, o_buf)


def _lane_blocks(s):
    return [s[:, b * LANES:(b + 1) * LANES] for b in range(s.shape[1] // LANES)]


def _softmax_pv(segments, sink=None):
    blocks = [blk for blks, _ in segments for blk in blks]
    rows = blocks[0].shape[0]
    mx = blocks[0]
    for blk in blocks[1:]:
        mx = jnp.maximum(mx, blk)
    m = jnp.broadcast_to(jnp.max(mx, axis=-1, keepdims=True), (rows, LANES))
    if sink is not None:
        m = jnp.maximum(m, sink)
    out = l = None
    for blks, v in segments:
        ps = [jnp.exp2(blk - m) for blk in blks]
        for p in ps:
            l = p if l is None else l + p
        pv = _dot(jnp.concatenate([p.astype(BF16) for p in ps], axis=1), v)
        out = pv if out is None else out + pv
    if sink is not None:
        lane = lax.broadcasted_iota(jnp.int32, (rows, LANES), 1)
        l = l + jnp.where(lane == 0, jnp.exp2(sink - m), 0.0)
    return out * (1.0 / jnp.sum(l, axis=-1, keepdims=True))


def _window_kernel(sink_ref, q_ref, k_ref, v_ref, buf_ref, o_ref, *, group, tq, n_lat, n_ctx, window):
    del buf_ref
    g, n = pl.program_id(0), pl.program_id(1)
    span = tq + 2 * LANES
    kc, vc = k_ref[n_lat:n_lat + n_ctx, :], v_ref[n_lat:n_lat + n_ctx, :]
    row = jnp.bitwise_and(lax.broadcasted_iota(jnp.int32, (group * tq, LANES), 0), tq - 1)
    lane_minus_row = lax.broadcasted_iota(jnp.int32, (group * tq, LANES), 1) - row
    sink = jnp.concatenate([jnp.full((tq, LANES), sink_ref[g * group + i], F32) for i in range(group)], axis=0)
    n_sub = q_ref.shape[0] // tq
    for sub in range(n_sub):
        rows = slice(sub * tq, (sub + 1) * tq)
        q0 = (n * n_sub + sub) * tq
        start = pl.multiple_of(jnp.clip(q0 - LANES, 0, n_lat - span), LANES)
        kw, vw = k_ref[pl.ds(start, span), :], v_ref[pl.ds(start, span), :]
        q = jnp.concatenate([q_ref[rows, i * HEAD_DIM:(i + 1) * HEAD_DIM] for i in range(group)], axis=0)
        s_w = [jnp.where(jnp.abs(lane_minus_row + (start - q0 + b * LANES)) <= window, blk, NEG_INF)
               for b, blk in enumerate(_lane_blocks(_dot_nt(q, kw)))]
        out = _softmax_pv([(s_w, vw), (_lane_blocks(_dot_nt(q, kc)), vc)], sink=sink)
        for i in range(group):
            o_ref[rows, i * HEAD_DIM:(i + 1) * HEAD_DIM] = out[i * tq:(i + 1) * tq].astype(o_ref.dtype)


def _window_attention(main, sink, o_buf, n_lat, n_ctx):
    rows = main.shape[0]
    group = A_HEADS // A_KV_HEADS
    tq = _pick(n_lat, (256, 128))
    assert A_WINDOW <= LANES and n_lat >= tq + 2 * LANES
    n_sub = _pick(n_lat // tq, (2, 1))
    return pl.pallas_call(
        functools.partial(_window_kernel, group=group, tq=tq, n_lat=n_lat, n_ctx=n_ctx, window=A_WINDOW),
        out_shape=jax.ShapeDtypeStruct(o_buf.shape, o_buf.dtype),
        grid=(A_KV_HEADS, n_lat // (n_sub * tq)),
        in_specs=[pl.BlockSpec(memory_space=pltpu.SMEM),
                  pl.BlockSpec((n_sub * tq, group * HEAD_DIM), lambda g, n: (n, g)),
                  pl.BlockSpec((rows, HEAD_DIM), lambda g, n: (0, A_K0 + g)),
                  pl.BlockSpec((rows, HEAD_DIM), lambda g, n: (0, A_V0 + g)),
                  _BUF_SPEC],
        out_specs=pl.BlockSpec((n_sub * tq, group * HEAD_DIM), lambda g, n: (n, MIX_A0 // group + g)),
        input_output_aliases={4: 0},
        compiler_params=_params("parallel", "arbitrary"),
        name="window_attention",
    )(sink, main, main, main, o_buf)


NBR_Q_ROWS = 8
NBR_KEY_ROWS = 16


def _nbr_key_start(rb, n_grid_rows):
    return jnp.clip(rb * NBR_Q_ROWS - NA_KH // 2, 0, n_grid_rows - NBR_KEY_ROWS)


def _nbr_kernel(q_ref, k_ref, v_ref, b_ref, buf_ref, o_ref, *, chain_rows, n_grid_rows, n_lat, n_ctx):
    del buf_ref
    k0 = pl.multiple_of(_nbr_key_start(pl.program_id(1), n_grid_rows) * GRID_W, (NA_KH // 2) * GRID_W)
    n_keys = NBR_KEY_ROWS * GRID_W
    kw, vw = k_ref[pl.ds(k0, n_keys), :], v_ref[pl.ds(k0, n_keys), :]
    kc, vc = k_ref[n_lat:n_lat + n_ctx, :], v_ref[n_lat:n_lat + n_ctx, :]
    rows_per_chain = chain_rows // GRID_W
    for sub in range(q_ref.shape[0] // chain_rows):
        rows = slice(sub * chain_rows, (sub + 1) * chain_rows)
        q = q_ref[rows, :]
        s_nb = [blk + jnp.concatenate([b_ref[sub * rows_per_chain + jj, kp] for jj in range(rows_per_chain)], axis=0)
                for kp, blk in enumerate(_lane_blocks(_dot_nt(q, kw)))]
        out = _softmax_pv([(s_nb, vw), (_lane_blocks(_dot_nt(q, kc)), vc)])
        o_ref[rows, :] = out.astype(o_ref.dtype)


def _nbr_bias_table(rel_bias):
    assert 2 * GRID_W == LANES
    n_heads = rel_bias.shape[0]
    hp = lax.Precision.HIGHEST
    off = jnp.array([0, NA_KH // 2, NA_KH], jnp.int32)[:, None, None]
    j = jnp.arange(NBR_Q_ROWS)[None, :, None]
    kr = jnp.arange(NBR_KEY_ROWS)[None, None, :]
    centred = j - NA_KH // 2
    rs = jnp.stack([jnp.maximum(centred[0], 0), centred[0], jnp.minimum(centred[0], 0)])
    key_row = kr - off
    row_ok = (key_row >= rs) & (key_row < rs + NA_KH)
    row_sel = jax.nn.one_hot(key_row - j + (NA_KH - 1), 2 * NA_KH - 1, dtype=F32) * row_ok[..., None]
    feat = jnp.einsum("vjka,hab->vhjkb", row_sel, rel_bias.astype(F32) * LOG2E, precision=hp)
    bad = jnp.broadcast_to((~row_ok).astype(F32)[:, None, :, :, None], feat.shape[:-1] + (1,))
    feat = jnp.concatenate([feat, bad], axis=-1)
    feat = feat.reshape(feat.shape[:3] + (NBR_KEY_ROWS // 2, 2 * feat.shape[-1]))
    feat = jnp.concatenate([feat, jnp.ones(feat.shape[:-1] + (1,), F32)], axis=-1)
    c = jnp.arange(GRID_W)[:, None]
    kc = jnp.arange(GRID_W)[None, :]
    cstart = jnp.clip(c - NA_KW // 2, 0, GRID_W - NA_KW)
    col_ok = (kc >= cstart) & (kc < cstart + NA_KW)
    col_sel = jax.nn.one_hot(kc - c + (NA_KW - 1), 2 * NA_KW - 1, dtype=F32)
    per_row = jnp.concatenate([col_sel.transpose(0, 2, 1), jnp.full((GRID_W, 1, GRID_W), NEG_INF, F32)], axis=1)
    sel = jnp.einsum("pq,cxk->cpxqk", jnp.eye(2, dtype=F32), per_row).reshape(GRID_W, 4 * NA_KW, LANES)
    col_bad = jnp.tile(jnp.where(col_ok, 0.0, NEG_INF).astype(F32), (1, 2))[:, None, :]
    sel = jnp.concatenate([sel, col_bad], axis=1)
    return jnp.einsum("vhjkx,cxl->vhjkcl", feat, sel, precision=hp)


def _nbr_attention(main, rel_bias, o_buf, n_lat, n_ctx):
    rows = main.shape[0]
    n_grid_rows = n_lat // GRID_W
    assert n_grid_rows % NBR_Q_ROWS == 0 and n_grid_rows >= NBR_KEY_ROWS
    assert NBR_KEY_ROWS >= NBR_Q_ROWS + NA_KH and NA_KH // 2 * 2 == NA_KH
    n_blocks = n_grid_rows // NBR_Q_ROWS
    tq = NBR_Q_ROWS * GRID_W

    def variant(rb):
        return (rb * NBR_Q_ROWS - _nbr_key_start(rb, n_grid_rows)) // (NA_KH // 2)

    return pl.pallas_call(
        functools.partial(_nbr_kernel, chain_rows=tq // 2, n_grid_rows=n_grid_rows, n_lat=n_lat, n_ctx=n_ctx),
        out_shape=jax.ShapeDtypeStruct(o_buf.shape, o_buf.dtype),
        grid=(D_HEADS, n_blocks),
        in_specs=[pl.BlockSpec((tq, HEAD_DIM), lambda h, rb: (rb, D_Q0 + h)),
                  pl.BlockSpec((rows, HEAD_DIM), lambda h, rb: (0, D_K0 + h)),
                  pl.BlockSpec((rows, HEAD_DIM), lambda h, rb: (0, D_V0 + h)),
                  pl.BlockSpec((None, None, NBR_Q_ROWS, NBR_KEY_ROWS // 2, GRID_W, LANES),
                               lambda h, rb: (variant(rb), h, 0, 0, 0, 0)),
                  _BUF_SPEC],
        out_specs=pl.BlockSpec((tq, HEAD_DIM), lambda h, rb: (rb, MIX_D0 + h)),
        input_output_aliases={4: 0},
        compiler_params=_params("parallel", "arbitrary"),
        name="nbr_attention",
    )(main, main, main, _nbr_bias_table(rel_bias), o_buf)


def _ctx_attn_kernel(sink_ref, q_ref, k_ref, v_ref, buf_ref, o_ref):
    del buf_ref
    sink = sink_ref[pl.program_id(0)]
    s = _dot_nt(q_ref[...], k_ref[...])
    m = jnp.maximum(jnp.max(s, axis=-1, keepdims=True), sink)
    p = jnp.exp2(s - m)
    denom = jnp.sum(p, axis=-1, keepdims=True) + jnp.exp2(sink - m)
    o_ref[...] = (_dot(p.astype(BF16), v_ref[...]) * (1.0 / denom)).astype(o_ref.dtype)


def _ctx_attention(q_arr, k_arr, v_arr, sink, o_buf, *, n_heads, group, dk, q_blk0, k_blk0, v_blk0, o_blk0, n_lat,
                   n_ctx):
    rb = n_lat // n_ctx
    return pl.pallas_call(
        _ctx_attn_kernel,
        out_shape=jax.ShapeDtypeStruct(o_buf.shape, o_buf.dtype),
        grid=(n_heads,),
        in_specs=[pl.BlockSpec(memory_space=pltpu.SMEM),
                  pl.BlockSpec((n_ctx, dk), lambda h: (rb, q_blk0 + h)),
                  pl.BlockSpec((n_ctx, dk), lambda h: (rb, k_blk0 + h // group)),
                  pl.BlockSpec((n_ctx, LANES), lambda h: (rb, v_blk0 + h // group)),
                  _BUF_SPEC],
        out_specs=pl.BlockSpec((n_ctx, LANES), lambda h: (rb, o_blk0 + h)),
        input_output_aliases={4: 0},
        compiler_params=_params("parallel"),
        name="ctx_attention",
    )(sink, q_arr, k_arr, v_arr, o_buf)


def _relayout_w_in(w, q_rank, kv_rank):
    a_cols = (A_HEADS + 2 * A_KV_HEADS) * HEAD_DIM
    b_cols = (B_HEADS + 2 * B_KV_HEADS) * HEAD_DIM
    c_cols = q_rank + kv_rank + C_ROPE_DIM
    o2, o3 = a_cols + b_cols, a_cols + b_cols + c_cols
    w_main = jnp.concatenate([w[..., :o2], w[..., o3:]], axis=-1).astype(BF16)
    c_pad = q_rank + kv_rank + LANES - c_cols
    w_c = jnp.pad(w[..., o2:o3], ((0, 0),) * (w.ndim - 1) + ((0, c_pad),)).astype(BF16)
    return w_main, w_c


def _relayout_w_q_up(w):
    lead = w.shape[:-1]
    w = w.astype(BF16).reshape(lead + (C_HEADS, C_NOPE_DIM + C_ROPE_DIM))
    w = jnp.pad(w, ((0, 0),) * (len(lead) + 1) + ((0, C_QK_PAD - C_NOPE_DIM - C_ROPE_DIM),))
    return w.reshape(lead + (C_HEADS * C_QK_PAD,))


def _relayout_w_kv_up(w):
    lead = w.shape[:-1]
    w = w.astype(BF16).reshape(lead + (C_HEADS, C_NOPE_DIM + C_V_DIM))
    return jnp.concatenate([w[..., :C_NOPE_DIM].reshape(lead + (-1,)), w[..., C_NOPE_DIM:].reshape(lead + (-1,))],
                           axis=-1)


def kernel(x, c, ctx, c_ctx, w_mod_down, w_mod_up, norm_ffn1, ffn1_w_gu, ffn1_w_down, norm_mix, w_in,
           a_sink, b_q_norm, b_k_norm, c_q_norm, c_kv_norm, c_w_q_up, c_w_kv_up, d_rel_bias, w_out,
           norm_ffn2, ffn2_w_gu, ffn2_w_down, final_norm):
    bsz, n_lat, d = x.shape
    n_ctx = ctx.shape[1]
    n_layers = w_in.shape[0]
    q_rank, kv_rank = c_q_norm.shape[1], c_kv_norm.shape[1]
    assert bsz == 1 and n_lat % GRID_W == 0 and n_lat % n_ctx == 0 and n_ctx % LANES == 0
    rows = n_lat + n_ctx
    tr = n_ctx
    tm_all = _pick(rows, (1408, 768, 512, 256, 128))
    tm_lat = _pick(n_lat, (1024, 512, 256, 128))
    tm_mla = _pick(rows, (768, 512, 256, 128))
    flash_tc = _pick(n_lat, (1024, 512, 256, 128))
    flash_tk = _pick(n_lat, (1024, 512, 256, 128))

    mod = _modulation(c, c_ctx, w_mod_down, w_mod_up)
    tables = _rope_tables(n_lat, n_ctx)
    no_sink = jnp.full((max(B_HEADS, C_HEADS, D_HEADS),), NEG_INF, F32)

    w_in_main, w_in_c = _relayout_w_in(w_in, q_rank, kv_rank)
    w_q_up_b, w_kv_up_b = _relayout_w_q_up(c_w_q_up), _relayout_w_kv_up(c_w_kv_up)

    xs = jnp.concatenate([x[0], ctx[0]], axis=0)

    def ffn(xs, gain, w_gu, w_down, l, mod_l, k0, n_rows, tm):
        h = _norm_mod(xs, gain, mod_l, k0, k0 + 1, n_lat, tr)
        act = _gate_up(h, w_gu, l, n_rows, tm)
        return _residual_matmul(act, w_down, l, xs, mod_l, k0 + 2, 0.5, n_rows, n_lat, tm)

    for l in range(n_layers):
        need_ctx = l < n_layers - 1
        n_rows, tm = (rows, tm_all) if need_ctx else (n_lat, tm_lat)
        mod_l = mod[l]
        xs = ffn(xs, norm_ffn1[l], ffn1_w_gu, ffn1_w_down, l, mod_l, 0, rows, tm_all)

        h = _norm_mod(xs, norm_mix[l], mod_l, 3, 4, n_lat, tr)
        main = _in_proj(h, w_in_main, l, tables, b_q_norm[l], b_k_norm[l], tm_all)
        qa, kva, kpe = _mla_down(h, w_in_c, l, tables, c_q_norm[l], c_kv_norm[l], tm_mla)
        qc, kc, vc = _mla_expand(qa, kva, kpe, w_q_up_b, w_kv_up_b, l, tables[2], tables[3], tr)

        sink = a_sink[l] * LOG2E
        lat = dict(n_lat=n_lat, n_ctx=n_ctx)
        o = jnp.zeros((n_rows, MIX_COLS), BF16)
        o = _window_attention(main, sink, o, **lat)
        o = _flash(main, main, main, o, n_kv=B_KV_HEADS, group=B_HEADS // B_KV_HEADS, dk=HEAD_DIM,
                   q_blk0=B_Q0 // (B_HEADS // B_KV_HEADS), k_blk0=B_K0, v_blk0=B_V0,
                   o_blk0=MIX_B0 // (B_HEADS // B_KV_HEADS), tq=flash_tc, tc=flash_tc, tk=flash_tk, **lat)
        o = _flash(qc, kc, vc, o, n_kv=C_HEADS, group=1, dk=C_QK_PAD, q_blk0=0, k_blk0=0, v_blk0=0, o_blk0=MIX_C0,
                   tq=_pick(n_lat, (2 * flash_tc, flash_tc)), tc=flash_tc, tk=flash_tk, **lat)
        o = _nbr_attention(main, d_rel_bias[l], o, **lat)
        if need_ctx:
            o = _ctx_attention(main, main, main, sink, o, n_heads=A_HEADS, group=A_HEADS // A_KV_HEADS,
                               dk=HEAD_DIM, q_blk0=A_Q0, k_blk0=A_K0, v_blk0=A_V0, o_blk0=MIX_A0, **lat)
            o = _ctx_attention(main, main, main, no_sink, o, n_heads=B_HEADS, group=B_HEADS // B_KV_HEADS,
                               dk=HEAD_DIM, q_blk0=B_Q0, k_blk0=B_K0, v_blk0=B_V0, o_blk0=MIX_B0, **lat)
            o = _ctx_attention(qc, kc, vc, no_sink, o, n_heads=C_HEADS, group=1, dk=C_QK_PAD,
                               q_blk0=0, k_blk0=0, v_blk0=0, o_blk0=MIX_C0, **lat)
            o = _ctx_attention(main, main, main, no_sink, o, n_heads=D_HEADS, group=1, dk=HEAD_DIM,
                               q_blk0=D_Q0, k_blk0=D_K0, v_blk0=D_V0, o_blk0=MIX_D0, **lat)

        xs = _residual_matmul(o, w_out, l, xs, mod_l, 5, 1.0, n_rows, n_lat, tm)
        xs = ffn(xs, norm_ffn2[l], ffn2_w_gu, ffn2_w_down, l, mod_l, 6, n_rows, tm)

    return _final_norm(xs[:n_lat], final_norm, tr)[None]
```

```python
import functools

import jax
import jax.numpy as jnp
from jax import lax
from jax.experimental import pallas as pl
from jax.experimental.pallas import tpu as pltpu

F32 = jnp.float32
BF16 = jnp.bfloat16

GRID_W = 64
HEAD_DIM = 128
ROPE_THETA = 10000.0
NORM_EPS = 1e-6
NEG_INF = -1e30
LOG2E = 1.4426950408889634
N_MOD = 9
A_HEADS, A_KV_HEADS, A_WINDOW = 8, 2, 128
B_HEADS, B_KV_HEADS = 8, 2
C_HEADS, C_NOPE_DIM, C_ROPE_DIM, C_V_DIM = 8, 128, 64, 128
D_HEADS, NA_KH, NA_KW = 8, 8, 16

LANES = 128
VMEM_LIMIT_BYTES = 56 * 1024 * 1024
C_QK_PAD = 2 * LANES

A_Q0, A_K0, A_V0 = 0, 8, 10
B_Q0, B_K0, B_V0 = 12, 20, 22
D_Q0, D_K0, D_V0 = 24, 32, 40
MAIN_COLS = 48 * LANES
MIX_A0, MIX_B0, MIX_C0, MIX_D0 = 0, 8, 16, 24
MIX_COLS = 32 * LANES
_BUF_SPEC = pl.BlockSpec(memory_space=pl.ANY)


def _params(*sem):
    return pltpu.CompilerParams(dimension_semantics=sem, vmem_limit_bytes=VMEM_LIMIT_BYTES)


def _pick(n, prefs):
    for p in prefs:
        if n % p == 0:
            return p
    raise ValueError(f"no tile in {prefs} divides {n}")


def _dot(a, b):
    return jnp.dot(a, b, preferred_element_type=F32)


def _dot_nt(a, b):
    return lax.dot_general(a, b, (((1,), (1,)), ((), ())), preferred_element_type=F32)


def _silu(x):
    return x / (1.0 + jnp.exp(-x))


def _rope(x, cos, sin_signed, half):
    n = x.shape[-1]
    lane = lax.broadcasted_iota(jnp.int32, x.shape, x.ndim - 1)
    first = jnp.bitwise_and(lane, 2 * half - 1) < half
    rot = jnp.where(first, pltpu.roll(x, n - half, x.ndim - 1), pltpu.roll(x, half, x.ndim - 1))
    return x * cos + rot * sin_signed


def _mod_down_kernel(s_ref, w_ref, o_ref, acc_ref):
    k = pl.program_id(1)

    @pl.when(k == 0)
    def _():
        acc_ref[...] = jnp.zeros_like(acc_ref)

    acc_ref[...] += _dot(_silu(s_ref[...]).astype(BF16), w_ref[...].astype(BF16))

    @pl.when(k == pl.num_programs(1) - 1)
    def _():
        o_ref[...] = acc_ref[...]


def _mod_up_kernel(t_ref, w_ref, o_ref):
    o_ref[...] = _dot(t_ref[...].astype(BF16), w_ref[...].astype(BF16))


def _modulation(c, c_ctx, w_down, w_up):
    n_layers, d, rank = w_down.shape
    n_out = w_up.shape[2]
    s = jnp.zeros((8, d), F32).at[0].set(c[0]).at[1].set(c_ctx)
    tk = _pick(d, (1024, 512, 256, 128))
    t = pl.pallas_call(
        _mod_down_kernel,
        out_shape=jax.ShapeDtypeStruct((n_layers, 8, rank), F32),
        grid=(n_layers, d // tk),
        in_specs=[pl.BlockSpec((8, tk), lambda l, k: (0, k)),
                  pl.BlockSpec((None, tk, rank), lambda l, k: (l, k, 0))],
        out_specs=pl.BlockSpec((None, 8, rank), lambda l, k: (l, 0, 0)),
        scratch_shapes=[pltpu.VMEM((8, rank), F32)],
        compiler_params=_params("parallel", "arbitrary"),
        name="mod_down",
    )(s, w_down)
    tn = _pick(n_out, (2048, 1024, 512, 256, 128))
    m = pl.pallas_call(
        _mod_up_kernel,
        out_shape=jax.ShapeDtypeStruct((n_layers, 8, n_out), F32),
        grid=(n_layers, n_out // tn),
        in_specs=[pl.BlockSpec((None, 8, rank), lambda l, j: (l, 0, 0)),
                  pl.BlockSpec((None, rank, tn), lambda l, j: (l, 0, j))],
        out_specs=pl.BlockSpec((None, 8, tn), lambda l, j: (l, 0, j)),
        compiler_params=_params("parallel", "parallel"),
        name="mod_up",
    )(t, w_up)
    return m[:, :2, :].reshape(n_layers, 2, N_MOD, 1, d)


def _norm_mod_kernel(x_ref, gain_ref, shift_ref, scale_ref, o_ref):
    x = x_ref[...]
    y = x * lax.rsqrt(jnp.mean(x * x, axis=-1, keepdims=True) + NORM_EPS) * gain_ref[...]
    o_ref[...] = (y * (1.0 + scale_ref[...]) + shift_ref[...]).astype(o_ref.dtype)


def _norm_mod(x, gain, mod_l, k_shift, k_scale, n_lat, tr):
    rows, d = x.shape
    n_lat_blocks = n_lat // tr

    def mod_spec(k):
        return pl.BlockSpec((None, None, 1, d),
                            lambda i: (jnp.where(i >= n_lat_blocks, 1, 0), k, 0, 0))

    return pl.pallas_call(
        _norm_mod_kernel,
        out_shape=jax.ShapeDtypeStruct((rows, d), BF16),
        grid=(rows // tr,),
        in_specs=[pl.BlockSpec((tr, d), lambda i: (i, 0)),
                  pl.BlockSpec((1, d), lambda i: (0, 0)),
                  mod_spec(k_shift), mod_spec(k_scale)],
        out_specs=pl.BlockSpec((tr, d), lambda i: (i, 0)),
        compiler_params=_params("parallel"),
        name="norm_mod",
    )(x, gain.reshape(1, d), mod_l, mod_l)


def _final_norm_kernel(x_ref, gain_ref, o_ref):
    x = x_ref[...]
    o_ref[...] = x * lax.rsqrt(jnp.mean(x * x, axis=-1, keepdims=True) + NORM_EPS) * gain_ref[...]


def _final_norm(x, gain, tr):
    rows, d = x.shape
    return pl.pallas_call(
        _final_norm_kernel,
        out_shape=jax.ShapeDtypeStruct((rows, d), F32),
        grid=(rows // tr,),
        in_specs=[pl.BlockSpec((tr, d), lambda i: (i, 0)),
                  pl.BlockSpec((1, d), lambda i: (0, 0))],
        out_specs=pl.BlockSpec((tr, d), lambda i: (i, 0)),
        compiler_params=_params("parallel"),
        name="final_norm",
    )(x, gain.reshape(1, d))


def _gate_up_kernel(h_ref, wg_ref, wu_ref, o_ref):
    h = h_ref[...]
    g = _dot(h, wg_ref[...].astype(BF16))
    u = _dot(h, wu_ref[...].astype(BF16))
    o_ref[...] = (_silu(g) * u).astype(o_ref.dtype)


def _gate_up(h, w_gu, layer, n_rows, tm):
    d = h.shape[1]
    f = w_gu.shape[2] // 2
    tn = _pick(f, (256, 128))
    nj = f // tn
    return pl.pallas_call(
        _gate_up_kernel,
        out_shape=jax.ShapeDtypeStruct((n_rows, f), BF16),
        grid=(n_rows // tm, nj),
        in_specs=[pl.BlockSpec((tm, d), lambda i, j: (i, 0)),
                  pl.BlockSpec((None, d, tn), lambda i, j: (layer, 0, j)),
                  pl.BlockSpec((None, d, tn), lambda i, j: (layer, 0, j + nj))],
        out_specs=pl.BlockSpec((tm, tn), lambda i, j: (i, j)),
        compiler_params=_params("parallel", "arbitrary"),
        name="gate_up",
    )(h, w_gu, w_gu)


def _residual_kernel(a_ref, w_ref, x_ref, g_ref, o_ref, *, coef, n_lat, tm):
    y = _dot(a_ref[...], w_ref[...])
    row = pl.program_id(0) * tm + lax.broadcasted_iota(jnp.int32, (tm, 1), 0)
    gate = jnp.where(row < n_lat, g_ref[0], g_ref[1])
    o_ref[...] = x_ref[...] + (coef * gate) * y


def _residual_matmul(a, w, layer, x, mod_l, k_gate, coef, n_rows, n_lat, tm):
    kdim = a.shape[1]
    d = w.shape[2]
    tn = _pick(d, (512, 256, 128))
    return pl.pallas_call(
        functools.partial(_residual_kernel, coef=coef, n_lat=n_lat, tm=tm),
        out_shape=jax.ShapeDtypeStruct((n_rows, d), F32),
        grid=(n_rows // tm, d // tn),
        in_specs=[pl.BlockSpec((tm, kdim), lambda i, j: (i, 0)),
                  pl.BlockSpec((None, kdim, tn), lambda i, j: (layer, 0, j)),
                  pl.BlockSpec((tm, tn), lambda i, j: (i, j)),
                  pl.BlockSpec((2, None, 1, tn), lambda i, j: (0, k_gate, 0, j))],
        out_specs=pl.BlockSpec((tm, tn), lambda i, j: (i, j)),
        compiler_params=_params("parallel", "arbitrary"),
        name="residual_matmul",
    )(a, w, x, mod_l)


IN_TILE_BLOCKS = 4
_MAIN_OPS = (("rope_scale",) * A_HEADS + ("rope",) * A_KV_HEADS + ("copy",) * A_KV_HEADS
             + ("norm_q_rope_scale",) * B_HEADS + ("norm_k_rope",) * B_KV_HEADS + ("copy",) * B_KV_HEADS
             + ("scale",) * D_HEADS + ("copy",) * (2 * D_HEADS))


def _head_norm(x, g):
    return x * lax.rsqrt(jnp.mean(x * x, axis=-1, keepdims=True) + NORM_EPS) * g


def _in_proj_kernel(h_ref, w_ref, cos_ref, sin_ref, gq_ref, gk_ref, o_ref):
    j = pl.program_id(1)
    y = _dot(h_ref[...], w_ref[...])
    scale = HEAD_DIM ** -0.5 * LOG2E

    def apply(op, x):
        if op in ("norm_q_rope_scale", "norm_k_rope"):
            x = _head_norm(x, (gq_ref if op == "norm_q_rope_scale" else gk_ref)[...])
        if "rope" in op:
            x = _rope(x, cos_ref[...], sin_ref[...], 32)
        return x * scale if "scale" in op else x

    n_tiles = len(_MAIN_OPS) // IN_TILE_BLOCKS
    recipes = [_MAIN_OPS[t * IN_TILE_BLOCKS:(t + 1) * IN_TILE_BLOCKS] for t in range(n_tiles)]
    for recipe in sorted(set(recipes)):
        tiles = [t for t in range(n_tiles) if recipes[t] == recipe]
        hit = functools.reduce(jnp.logical_or, [j == t for t in tiles])

        @pl.when(hit)
        def _(recipe=recipe):
            for b, op in enumerate(recipe):
                cols = slice(b * LANES, (b + 1) * LANES)
                o_ref[:, cols] = apply(op, y[:, cols]).astype(o_ref.dtype)


def _in_proj(h, w_main, layer, tables, g_q, g_k, tm):
    rows, d = h.shape
    tn = IN_TILE_BLOCKS * LANES
    assert w_main.shape[2] == MAIN_COLS and MAIN_COLS % tn == 0
    return pl.pallas_call(
        _in_proj_kernel,
        out_shape=jax.ShapeDtypeStruct((rows, MAIN_COLS), BF16),
        grid=(rows // tm, MAIN_COLS // tn),
        in_specs=[pl.BlockSpec((tm, d), lambda i, j: (i, 0)),
                  pl.BlockSpec((None, d, tn), lambda i, j: (layer, 0, j)),
                  pl.BlockSpec((tm, LANES), lambda i, j: (i, 0)),
                  pl.BlockSpec((tm, LANES), lambda i, j: (i, 0)),
                  pl.BlockSpec((1, LANES), lambda i, j: (0, 0)),
                  pl.BlockSpec((1, LANES), lambda i, j: (0, 0))],
        out_specs=pl.BlockSpec((tm, tn), lambda i, j: (i, j)),
        compiler_params=_params("parallel", "arbitrary"),
        name="in_proj",
    )(h, w_main, tables[0], tables[1], g_q.reshape(1, -1), g_k.reshape(1, -1))


def _mla_down_kernel(h_ref, w_ref, cosc_ref, sinc_ref, gq_ref, gkv_ref, qa_ref, kva_ref, kpe_ref):
    q_rank, kv_rank = qa_ref.shape[1], kva_ref.shape[1]
    y = _dot(h_ref[...], w_ref[...])
    qa_ref[...] = _head_norm(y[:, :q_rank], gq_ref[...]).astype(qa_ref.dtype)
    kva_ref[...] = _head_norm(y[:, q_rank:q_rank + kv_rank], gkv_ref[...]).astype(kva_ref.dtype)
    kpe = y[:, q_rank + kv_rank:q_rank + kv_rank + LANES]
    kpe_ref[...] = _rope(kpe, cosc_ref[...], sinc_ref[...], 16).astype(kpe_ref.dtype)


def _mla_down(h, w_c, layer, tables, g_cq, g_ckv, tm):
    rows, d = h.shape
    q_rank, kv_rank = g_cq.shape[0], g_ckv.shape[0]
    cols = w_c.shape[2]
    row_spec = lambda w: pl.BlockSpec((tm, w), lambda i: (i, 0))
    vec_spec = lambda w: pl.BlockSpec((1, w), lambda i: (0, 0))
    return pl.pallas_call(
        _mla_down_kernel,
        out_shape=(jax.ShapeDtypeStruct((rows, q_rank), BF16),
                   jax.ShapeDtypeStruct((rows, kv_rank), BF16),
                   jax.ShapeDtypeStruct((rows, LANES), BF16)),
        grid=(rows // tm,),
        in_specs=[row_spec(d), pl.BlockSpec((None, d, cols), lambda i: (layer, 0, 0)),
                  row_spec(LANES), row_spec(LANES), vec_spec(q_rank), vec_spec(kv_rank)],
        out_specs=(row_spec(q_rank), row_spec(kv_rank), row_spec(LANES)),
        compiler_params=_params("parallel"),
        name="mla_down",
    )(h, w_c, tables[2], tables[3], g_cq.reshape(1, -1), g_ckv.reshape(1, -1))


def _rope_tables(n_lat, n_ctx):
    t = jnp.arange(n_lat, dtype=jnp.int32)
    row, col = (t // GRID_W).astype(F32), (t % GRID_W).astype(F32)
    lane = jnp.arange(LANES)

    def table(dim, live):
        half = dim // 2
        pair = half // 2
        inv = ROPE_THETA ** (-jnp.arange(0, half, 2, dtype=F32) / half)
        inv_lane = inv[lane % pair]
        pos = jnp.where(((lane // half) % 2 == 0)[None, :], row[:, None], col[:, None])
        ang = pos * inv_lane[None, :]
        sign = jnp.where((lane % half) < pair, -1.0, 1.0).astype(F32)
        on = (lane < live)[None, :]
        cos = jnp.where(on, jnp.cos(ang), 1.0)
        sin = jnp.where(on, jnp.sin(ang) * sign[None, :], 0.0)
        pad = lambda a, v: jnp.concatenate([a, jnp.full((n_ctx, LANES), v, F32)], axis=0)
        return pad(cos, 1.0), pad(sin, 0.0)

    cos, sin = table(HEAD_DIM, LANES)
    cosc, sinc = table(C_ROPE_DIM, C_ROPE_DIM)
    return cos, sin, cosc, sinc


def _cq_up_kernel(a_ref, w_ref, cos_ref, sin_ref, o_ref, *, scale):
    y = _dot(a_ref[...], w_ref[...])
    cos, sin = cos_ref[...], sin_ref[...]
    for h in range(C_HEADS):
        c0 = h * C_QK_PAD
        o_ref[:, c0:c0 + LANES] = (y[:, c0:c0 + LANES] * scale).astype(o_ref.dtype)
        pe = _rope(y[:, c0 + LANES:c0 + 2 * LANES], cos, sin, 16)
        o_ref[:, c0 + LANES:c0 + 2 * LANES] = (pe * scale).astype(o_ref.dtype)


def _ckv_up_kernel(a_ref, w_ref, kpe_ref, k_ref, v_ref):
    y = _dot(a_ref[...], w_ref[...])
    kpe = kpe_ref[...]
    for h in range(C_HEADS):
        k_ref[:, h * C_QK_PAD:h * C_QK_PAD + LANES] = y[:, h * LANES:(h + 1) * LANES].astype(k_ref.dtype)
        k_ref[:, h * C_QK_PAD + LANES:(h + 1) * C_QK_PAD] = kpe
    v_ref[...] = y[:, C_HEADS * LANES:].astype(v_ref.dtype)


def _mla_expand(qa, kva, kpe, w_q, w_kv, layer, cosc, sinc, tm):
    rows, q_rank = qa.shape
    kv_rank = kva.shape[1]
    scale = (C_NOPE_DIM + C_ROPE_DIM) ** -0.5 * LOG2E
    row_spec = lambda w: pl.BlockSpec((tm, w), lambda i: (i, 0))
    layer_spec = lambda w: pl.BlockSpec((None,) + w.shape[1:], lambda i: (layer, 0, 0))
    qc = pl.pallas_call(
        functools.partial(_cq_up_kernel, scale=scale),
        out_shape=jax.ShapeDtypeStruct((rows, C_HEADS * C_QK_PAD), BF16),
        grid=(rows // tm,),
        in_specs=[row_spec(q_rank), layer_spec(w_q), row_spec(LANES), row_spec(LANES)],
        out_specs=row_spec(C_HEADS * C_QK_PAD),
        compiler_params=_params("parallel"),
        name="mla_q_up",
    )(qa, w_q, cosc, sinc)
    kc, vc = pl.pallas_call(
        _ckv_up_kernel,
        out_shape=(jax.ShapeDtypeStruct((rows, C_HEADS * C_QK_PAD), BF16),
                   jax.ShapeDtypeStruct((rows, C_HEADS * C_V_DIM), BF16)),
        grid=(rows // tm,),
        in_specs=[row_spec(kv_rank), layer_spec(w_kv), row_spec(LANES)],
        out_specs=(row_spec(C_HEADS * C_QK_PAD), row_spec(C_HEADS * C_V_DIM)),
        compiler_params=_params("parallel"),
        name="mla_kv_up",
    )(kva, w_kv, kpe)
    return qc, kc, vc


def _flash_kernel(q_ref, k_ref, v_ref, buf_ref, o_ref, m_sc, l_sc, acc_sc, *, group, dk, tc, tk, n_lat, n_ctx):
    del buf_ref
    tq = q_ref.shape[0]
    chains = [(g, r) for g in range(group) for r in range(tq // tc)]

    def chunk(k, v, first):
        n_blocks = k.shape[0] // LANES
        for ci, (g, r) in enumerate(chains):
            s = _dot_nt(q_ref[r * tc:(r + 1) * tc, g * dk:(g + 1) * dk], k)
            blocks = [s[:, b * LANES:(b + 1) * LANES] for b in range(n_blocks)]
            mx = blocks[0]
            for blk in blocks[1:]:
                mx = jnp.maximum(mx, blk)
            m_new = jnp.broadcast_to(jnp.max(mx, axis=-1, keepdims=True), (tc, LANES))
            if not first:
                m_prev = m_sc[ci]
                m_new = jnp.maximum(m_prev, m_new)
                alpha = jnp.exp2(m_prev - m_new)
            ps = [jnp.exp2(blk - m_new) for blk in blocks]
            l_new = ps[0]
            for p in ps[1:]:
                l_new = l_new + p
            pv = _dot(jnp.concatenate([p.astype(BF16) for p in ps], axis=1), v)
            if first:
                l_sc[ci] = l_new
                acc_sc[ci] = pv
            else:
                l_sc[ci] = alpha * l_sc[ci] + l_new
                acc_sc[ci] = alpha * acc_sc[ci] + pv
            m_sc[ci] = m_new

    chunk(k_ref[n_lat:n_lat + n_ctx, :], v_ref[n_lat:n_lat + n_ctx, :], True)

    def step(c, carry):
        start = pl.multiple_of(c * tk, tk)
        chunk(k_ref[pl.ds(start, tk), :], v_ref[pl.ds(start, tk), :], False)
        return carry

    lax.fori_loop(0, n_lat // tk, step, 0)
    for ci, (g, r) in enumerate(chains):
        inv = 1.0 / jnp.sum(l_sc[ci], axis=-1, keepdims=True)
        o_ref[r * tc:(r + 1) * tc, g * LANES:(g + 1) * LANES] = (acc_sc[ci] * inv).astype(o_ref.dtype)


def _flash(q_arr, k_arr, v_arr, o_buf, *, n_kv, group, dk, q_blk0, k_blk0, v_blk0, o_blk0, tq, tc, tk, n_lat,
           n_ctx):
    rows = k_arr.shape[0]
    n_chains = group * (tq // tc)
    stat = pltpu.VMEM((n_chains, tc, LANES), F32)
    return pl.pallas_call(
        functools.partial(_flash_kernel, group=group, dk=dk, tc=tc, tk=tk, n_lat=n_lat, n_ctx=n_ctx),
        out_shape=jax.ShapeDtypeStruct(o_buf.shape, o_buf.dtype),
        grid=(n_kv, n_lat // tq),
        in_specs=[pl.BlockSpec((tq, group * dk), lambda g, i: (i, q_blk0 + g)),
                  pl.BlockSpec((rows, dk), lambda g, i: (0, k_blk0 + g)),
                  pl.BlockSpec((rows, LANES), lambda g, i: (0, v_blk0 + g)),
                  _BUF_SPEC],
        out_specs=pl.BlockSpec((tq, group * LANES), lambda g, i: (i, o_blk0 + g)),
        scratch_shapes=[stat, stat, stat],
        input_output_aliases={3: 0},
        compiler_params=_params("parallel", "arbitrary"),
        name="flash",
    )(q_arr, k_arr, v_arr, o_buf)


def _lane_blocks(s):
    return [s[:, b * LANES:(b + 1) * LANES] for b in range(s.shape[1] // LANES)]


def _softmax_pv(segments, sink=None):
    blocks = [blk for blks, _ in segments for blk in blks]
    rows = blocks[0].shape[0]
    mx = blocks[0]
    for blk in blocks[1:]:
        mx = jnp.maximum(mx, blk)
    m = jnp.broadcast_to(jnp.max(mx, axis=-1, keepdims=True), (rows, LANES))
    if sink is not None:
        m = jnp.maximum(m, sink)
    out = l = None
    for blks, v in segments:
        ps = [jnp.exp2(blk - m) for blk in blks]
        for p in ps:
            l = p if l is None else l + p
        pv = _dot(jnp.concatenate([p.astype(BF16) for p in ps], axis=1), v)
        out = pv if out is None else out + pv
    if sink is not None:
        lane = lax.broadcasted_iota(jnp.int32, (rows, LANES), 1)
        l = l + jnp.where(lane == 0, jnp.exp2(sink - m), 0.0)
    return out * (1.0 / jnp.sum(l, axis=-1, keepdims=True))


def _window_kernel(sink_ref, q_ref, k_ref, v_ref, buf_ref, o_ref, *, group, tq, n_lat, n_ctx, window):
    del buf_ref
    g, n = pl.program_id(0), pl.program_id(1)
    span = tq + 2 * LANES
    kc, vc = k_ref[n_lat:n_lat + n_ctx, :], v_ref[n_lat:n_lat + n_ctx, :]
    row = jnp.bitwise_and(lax.broadcasted_iota(jnp.int32, (group * tq, LANES), 0), tq - 1)
    lane_minus_row = lax.broadcasted_iota(jnp.int32, (group * tq, LANES), 1) - row
    sink = jnp.concatenate([jnp.full((tq, LANES), sink_ref[g * group + i], F32) for i in range(group)], axis=0)
    n_sub = q_ref.shape[0] // tq
    for sub in range(n_sub):
        rows = slice(sub * tq, (sub + 1) * tq)
        q0 = (n * n_sub + sub) * tq
        start = pl.multiple_of(jnp.clip(q0 - LANES, 0, n_lat - span), LANES)
        kw, vw = k_ref[pl.ds(start, span), :], v_ref[pl.ds(start, span), :]
        q = jnp.concatenate([q_ref[rows, i * HEAD_DIM:(i + 1) * HEAD_DIM] for i in range(group)], axis=0)
        s_w = [jnp.where(jnp.abs(lane_minus_row + (start - q0 + b * LANES)) <= window, blk, NEG_INF)
               for b, blk in enumerate(_lane_blocks(_dot_nt(q, kw)))]
        out = _softmax_pv([(s_w, vw), (_lane_blocks(_dot_nt(q, kc)), vc)], sink=sink)
        for i in range(group):
            o_ref[rows, i * HEAD_DIM:(i + 1) * HEAD_DIM] = out[i * tq:(i + 1) * tq].astype(o_ref.dtype)


def _window_attention(main, sink, o_buf, n_lat, n_ctx):
    rows = main.shape[0]
    group = A_HEADS // A_KV_HEADS
    tq = _pick(n_lat, (256, 128))
    assert A_WINDOW <= LANES and n_lat >= tq + 2 * LANES
    n_sub = _pick(n_lat // tq, (2, 1))
    return pl.pallas_call(
        functools.partial(_window_kernel, group=group, tq=tq, n_lat=n_lat, n_ctx=n_ctx, window=A_WINDOW),
        out_shape=jax.ShapeDtypeStruct(o_buf.shape, o_buf.dtype),
        grid=(A_KV_HEADS, n_lat // (n_sub * tq)),
        in_specs=[pl.BlockSpec(memory_space=pltpu.SMEM),
                  pl.BlockSpec((n_sub * tq, group * HEAD_DIM), lambda g, n: (n, g)),
                  pl.BlockSpec((rows, HEAD_DIM), lambda g, n: (0, A_K0 + g)),
                  pl.BlockSpec((rows, HEAD_DIM), lambda g, n: (0, A_V0 + g)),
                  _BUF_SPEC],
        out_specs=pl.BlockSpec((n_sub * tq, group * HEAD_DIM), lambda g, n: (n, MIX_A0 // group + g)),
        input_output_aliases={4: 0},
        compiler_params=_params("parallel", "arbitrary"),
        name="window_attention",
    )(sink, main, main, main, o_buf)


NBR_Q_ROWS = 8
NBR_KEY_ROWS = 16


def _nbr_key_start(rb, n_grid_rows):
    return jnp.clip(rb * NBR_Q_ROWS - NA_KH // 2, 0, n_grid_rows - NBR_KEY_ROWS)


def _nbr_kernel(q_ref, k_ref, v_ref, b_ref, buf_ref, o_ref, *, chain_rows, n_grid_rows, n_lat, n_ctx):
    del buf_ref
    k0 = pl.multiple_of(_nbr_key_start(pl.program_id(1), n_grid_rows) * GRID_W, (NA_KH // 2) * GRID_W)
    n_keys = NBR_KEY_ROWS * GRID_W
    kw, vw = k_ref[pl.ds(k0, n_keys), :], v_ref[pl.ds(k0, n_keys), :]
    kc, vc = k_ref[n_lat:n_lat + n_ctx, :], v_ref[n_lat:n_lat + n_ctx, :]
    rows_per_chain = chain_rows // GRID_W
    for sub in range(q_ref.shape[0] // chain_rows):
        rows = slice(sub * chain_rows, (sub + 1) * chain_rows)
        q = q_ref[rows, :]
        s_nb = [blk + jnp.concatenate([b_ref[sub * rows_per_chain + jj, kp] for jj in range(rows_per_chain)], axis=0)
                for kp, blk in enumerate(_lane_blocks(_dot_nt(q, kw)))]
        out = _softmax_pv([(s_nb, vw), (_lane_blocks(_dot_nt(q, kc)), vc)])
        o_ref[rows, :] = out.astype(o_ref.dtype)


def _nbr_bias_table(rel_bias):
    assert 2 * GRID_W == LANES
    n_heads = rel_bias.shape[0]
    hp = lax.Precision.HIGHEST
    off = jnp.array([0, NA_KH // 2, NA_KH], jnp.int32)[:, None, None]
    j = jnp.arange(NBR_Q_ROWS)[None, :, None]
    kr = jnp.arange(NBR_KEY_ROWS)[None, None, :]
    centred = j - NA_KH // 2
    rs = jnp.stack([jnp.maximum(centred[0], 0), centred[0], jnp.minimum(centred[0], 0)])
    key_row = kr - off
    row_ok = (key_row >= rs) & (key_row < rs + NA_KH)
    row_sel = jax.nn.one_hot(key_row - j + (NA_KH - 1), 2 * NA_KH - 1, dtype=F32) * row_ok[..., None]
    feat = jnp.einsum("vjka,hab->vhjkb", row_sel, rel_bias.astype(F32) * LOG2E, precision=hp)
    bad = jnp.broadcast_to((~row_ok).astype(F32)[:, None, :, :, None], feat.shape[:-1] + (1,))
    feat = jnp.concatenate([feat, bad], axis=-1)
    feat = feat.reshape(feat.shape[:3] + (NBR_KEY_ROWS // 2, 2 * feat.shape[-1]))
    feat = jnp.concatenate([feat, jnp.ones(feat.shape[:-1] + (1,), F32)], axis=-1)
    c = jnp.arange(GRID_W)[:, None]
    kc = jnp.arange(GRID_W)[None, :]
    cstart = jnp.clip(c - NA_KW // 2, 0, GRID_W - NA_KW)
    col_ok = (kc >= cstart) & (kc < cstart + NA_KW)
    col_sel = jax.nn.one_hot(kc - c + (NA_KW - 1), 2 * NA_KW - 1, dtype=F32)
    per_row = jnp.concatenate([col_sel.transpose(0, 2, 1), jnp.full((GRID_W, 1, GRID_W), NEG_INF, F32)], axis=1)
    sel = jnp.einsum("pq,cxk->cpxqk", jnp.eye(2, dtype=F32), per_row).reshape(GRID_W, 4 * NA_KW, LANES)
    col_bad = jnp.tile(jnp.where(col_ok, 0.0, NEG_INF).astype(F32), (1, 2))[:, None, :]
    sel = jnp.concatenate([sel, col_bad], axis=1)
    return jnp.einsum("vhjkx,cxl->vhjkcl", feat, sel, precision=hp)


def _nbr_attention(main, rel_bias, o_buf, n_lat, n_ctx):
    rows = main.shape[0]
    n_grid_rows = n_lat // GRID_W
    assert n_grid_rows % NBR_Q_ROWS == 0 and n_grid_rows >= NBR_KEY_ROWS
    assert NBR_KEY_ROWS >= NBR_Q_ROWS + NA_KH and NA_KH // 2 * 2 == NA_KH
    n_blocks = n_grid_rows // NBR_Q_ROWS
    tq = NBR_Q_ROWS * GRID_W

    def variant(rb):
        return (rb * NBR_Q_ROWS - _nbr_key_start(rb, n_grid_rows)) // (NA_KH // 2)

    return pl.pallas_call(
        functools.partial(_nbr_kernel, chain_rows=tq // 2, n_grid_rows=n_grid_rows, n_lat=n_lat, n_ctx=n_ctx),
        out_shape=jax.ShapeDtypeStruct(o_buf.shape, o_buf.dtype),
        grid=(D_HEADS, n_blocks),
        in_specs=[pl.BlockSpec((tq, HEAD_DIM), lambda h, rb: (rb, D_Q0 + h)),
                  pl.BlockSpec((rows, HEAD_DIM), lambda h, rb: (0, D_K0 + h)),
                  pl.BlockSpec((rows, HEAD_DIM), lambda h, rb: (0, D_V0 + h)),
                  pl.BlockSpec((None, None, NBR_Q_ROWS, NBR_KEY_ROWS // 2, GRID_W, LANES),
                               lambda h, rb: (variant(rb), h, 0, 0, 0, 0)),
                  _BUF_SPEC],
        out_specs=pl.BlockSpec((tq, HEAD_DIM), lambda h, rb: (rb, MIX_D0 + h)),
        input_output_aliases={4: 0},
        compiler_params=_params("parallel", "arbitrary"),
        name="nbr_attention",
    )(main, main, main, _nbr_bias_table(rel_bias), o_buf)


def _ctx_attn_kernel(sink_ref, q_ref, k_ref, v_ref, buf_ref, o_ref):
    del buf_ref
    sink = sink_ref[pl.program_id(0)]
    s = _dot_nt(q_ref[...], k_ref[...])
    m = jnp.maximum(jnp.max(s, axis=-1, keepdims=True), sink)
    p = jnp.exp2(s - m)
    denom = jnp.sum(p, axis=-1, keepdims=True) + jnp.exp2(sink - m)
    o_ref[...] = (_dot(p.astype(BF16), v_ref[...]) * (1.0 / denom)).astype(o_ref.dtype)


def _ctx_attention(q_arr, k_arr, v_arr, sink, o_buf, *, n_heads, group, dk, q_blk0, k_blk0, v_blk0, o_blk0, n_lat,
                   n_ctx):
    rb = n_lat // n_ctx
    return pl.pallas_call(
        _ctx_attn_kernel,
        out_shape=jax.ShapeDtypeStruct(o_buf.shape, o_buf.dtype),
        grid=(n_heads,),
        in_specs=[pl.BlockSpec(memory_space=pltpu.SMEM),
                  pl.BlockSpec((n_ctx, dk), lambda h: (rb, q_blk0 + h)),
                  pl.BlockSpec((n_ctx, dk), lambda h: (rb, k_blk0 + h // group)),
                  pl.BlockSpec((n_ctx, LANES), lambda h: (rb, v_blk0 + h // group)),
                  _BUF_SPEC],
        out_specs=pl.BlockSpec((n_ctx, LANES), lambda h: (rb, o_blk0 + h)),
        input_output_aliases={4: 0},
        compiler_params=_params("parallel"),
        name="ctx_attention",
    )(sink, q_arr, k_arr, v_arr, o_buf)


def _w_in_relayout_kernel(w_ref, main_ref, c_ref, *, o2, o3):
    main_ref[:, :o2] = w_ref[:, :o2].astype(BF16)
    main_ref[:, o2:] = w_ref[:, o3:].astype(BF16)
    c_ref[:, :o3 - o2] = w_ref[:, o2:o3].astype(BF16)
    c_ref[:, o3 - o2:] = jnp.zeros((c_ref.shape[0], c_ref.shape[1] - (o3 - o2)), BF16)


def _relayout_w_in(w, q_rank, kv_rank):
    n_layers, d, cols = w.shape
    a_cols = (A_HEADS + 2 * A_KV_HEADS) * HEAD_DIM
    b_cols = (B_HEADS + 2 * B_KV_HEADS) * HEAD_DIM
    c_cols = q_rank + kv_rank + C_ROPE_DIM
    o2, o3 = a_cols + b_cols, a_cols + b_cols + c_cols
    assert cols - c_cols == MAIN_COLS
    c_wide = q_rank + kv_rank + LANES
    tr = _pick(d, (256, 128))
    return pl.pallas_call(
        functools.partial(_w_in_relayout_kernel, o2=o2, o3=o3),
        out_shape=(jax.ShapeDtypeStruct((n_layers, d, MAIN_COLS), BF16),
                   jax.ShapeDtypeStruct((n_layers, d, c_wide), BF16)),
        grid=(n_layers, d // tr),
        in_specs=[pl.BlockSpec((None, tr, cols), lambda l, i: (l, i, 0))],
        out_specs=(pl.BlockSpec((None, tr, MAIN_COLS), lambda l, i: (l, i, 0)),
                   pl.BlockSpec((None, tr, c_wide), lambda l, i: (l, i, 0))),
        compiler_params=_params("parallel", "parallel"),
        name="w_in_relayout",
    )(w)


def _relayout_w_q_up(w):
    lead = w.shape[:-1]
    w = w.astype(BF16).reshape(lead + (C_HEADS, C_NOPE_DIM + C_ROPE_DIM))
    w = jnp.pad(w, ((0, 0),) * (len(lead) + 1) + ((0, C_QK_PAD - C_NOPE_DIM - C_ROPE_DIM),))
    return w.reshape(lead + (C_HEADS * C_QK_PAD,))


def _relayout_w_kv_up(w):
    lead = w.shape[:-1]
    w = w.astype(BF16).reshape(lead + (C_HEADS, C_NOPE_DIM + C_V_DIM))
    return jnp.concatenate([w[..., :C_NOPE_DIM].reshape(lead + (-1,)), w[..., C_NOPE_DIM:].reshape(lead + (-1,))],
                           axis=-1)


def kernel(x, c, ctx, c_ctx, w_mod_down, w_mod_up, norm_ffn1, ffn1_w_gu, ffn1_w_down, norm_mix, w_in,
           a_sink, b_q_norm, b_k_norm, c_q_norm, c_kv_norm, c_w_q_up, c_w_kv_up, d_rel_bias, w_out,
           norm_ffn2, ffn2_w_gu, ffn2_w_down, final_norm):
    bsz, n_lat, d = x.shape
    n_ctx = ctx.shape[1]
    n_layers = w_in.shape[0]
    q_rank, kv_rank = c_q_norm.shape[1], c_kv_norm.shape[1]
    assert bsz == 1 and n_lat % GRID_W == 0 and n_lat % n_ctx == 0 and n_ctx % LANES == 0
    rows = n_lat + n_ctx
    tr = n_ctx
    tm_all = (_pick(rows, (1408, 768, 512, 256, 128)), _pick(rows, (768, 512, 256, 128)))
    tm_lat = (_pick(n_lat, (1024, 512, 256, 128)),) * 2
    flash_tc = _pick(n_lat, (1024, 512, 256, 128))
    flash_tk = _pick(n_lat, (2048, 1024, 512, 256, 128))

    mod = _modulation(c, c_ctx, w_mod_down, w_mod_up)
    tables = _rope_tables(n_lat, n_ctx)
    no_sink = jnp.full((max(B_HEADS, C_HEADS, D_HEADS),), NEG_INF, F32)

    w_in_main, w_in_c = _relayout_w_in(w_in, q_rank, kv_rank)
    w_q_up_b, w_kv_up_b = _relayout_w_q_up(c_w_q_up), _relayout_w_kv_up(c_w_kv_up)
    w_dn1, w_dn2, w_out_b = ffn1_w_down.astype(BF16), ffn2_w_down.astype(BF16), w_out.astype(BF16)

    xs = jnp.concatenate([x[0], ctx[0]], axis=0)

    def ffn(xs, gain, w_gu, w_down, l, mod_l, k0, n_rows, tm):
        h = _norm_mod(xs, gain, mod_l, k0, k0 + 1, n_lat, tr)
        act = _gate_up(h, w_gu, l, n_rows, tm[0])
        return _residual_matmul(act, w_down, l, xs, mod_l, k0 + 2, 0.5, n_rows, n_lat, tm[1])

    for l in range(n_layers):
        need_ctx = l < n_layers - 1
        n_rows, tm = (rows, tm_all) if need_ctx else (n_lat, tm_lat)
        mod_l = mod[l]
        xs = ffn(xs, norm_ffn1[l], ffn1_w_gu, w_dn1, l, mod_l, 0, rows, tm_all)

        h = _norm_mod(xs, norm_mix[l], mod_l, 3, 4, n_lat, tr)
        main = _in_proj(h, w_in_main, l, tables, b_q_norm[l], b_k_norm[l], tm_all[0])
        qa, kva, kpe = _mla_down(h, w_in_c, l, tables, c_q_norm[l], c_kv_norm[l], tm_all[1])
        qc, kc, vc = _mla_expand(qa, kva, kpe, w_q_up_b, w_kv_up_b, l, tables[2], tables[3], tr)

        sink = a_sink[l] * LOG2E
        lat = dict(n_lat=n_lat, n_ctx=n_ctx)
        o = jnp.zeros((n_rows, MIX_COLS), BF16)
        o = _window_attention(main, sink, o, **lat)
        o = _flash(main, main, main, o, n_kv=B_KV_HEADS, group=B_HEADS // B_KV_HEADS, dk=HEAD_DIM,
                   q_blk0=B_Q0 // (B_HEADS // B_KV_HEADS), k_blk0=B_K0, v_blk0=B_V0,
                   o_blk0=MIX_B0 // (B_HEADS // B_KV_HEADS), tq=flash_tc, tc=flash_tc, tk=flash_tk, **lat)
        o = _flash(qc, kc, vc, o, n_kv=C_HEADS, group=1, dk=C_QK_PAD, q_blk0=0, k_blk0=0, v_blk0=0, o_blk0=MIX_C0,
                   tq=_pick(n_lat, (4 * flash_tc, 2 * flash_tc, flash_tc)), tc=flash_tc, tk=flash_tk, **lat)
        o = _nbr_attention(main, d_rel_bias[l], o, **lat)
        if need_ctx:
            o = _ctx_attention(main, main, main, sink, o, n_heads=A_HEADS, group=A_HEADS // A_KV_HEADS,
                               dk=HEAD_DIM, q_blk0=A_Q0, k_blk0=A_K0, v_blk0=A_V0, o_blk0=MIX_A0, **lat)
            o = _ctx_attention(main, main, main, no_sink, o, n_heads=B_HEADS, group=B_HEADS // B_KV_HEADS,
                               dk=HEAD_DIM, q_blk0=B_Q0, k_blk0=B_K0, v_blk0=B_V0, o_blk0=MIX_B0, **lat)
            o = _ctx_attention(qc, kc, vc, no_sink, o, n_heads=C_HEADS, group=1, dk=C_QK_PAD,
                               q_blk0=0, k_blk0=0, v_blk0=0, o_blk0=MIX_C0, **lat)
            o = _ctx_attention(main, main, main, no_sink, o, n_heads=D_HEADS, group=1, dk=HEAD_DIM,
                               q_blk0=D_Q0, k_blk0=D_K0, v_blk0=D_V0, o_blk0=MIX_D0, **lat)

        xs = _residual_matmul(o, w_out_b, l, xs, mod_l, 5, 1.0, n_rows, n_lat, tm[1])
        xs = ffn(xs, norm_ffn2[l], ffn2_w_gu, w_dn2, l, mod_l, 6, n_rows, tm)

    return _final_norm(xs[:n_lat], final_norm, tr)[None]
```

```python
import functools

import jax
import jax.numpy as jnp
from jax import lax
from jax.experimental import pallas as pl
from jax.experimental.pallas import tpu as pltpu

F32 = jnp.float32
BF16 = jnp.bfloat16

GRID_W = 64
HEAD_DIM = 128
ROPE_THETA = 10000.0
NORM_EPS = 1e-6
NEG_INF = -1e30
LOG2E = 1.4426950408889634
N_MOD = 9
A_HEADS, A_KV_HEADS, A_WINDOW = 8, 2, 128
B_HEADS, B_KV_HEADS = 8, 2
C_HEADS, C_NOPE_DIM, C_ROPE_DIM, C_V_DIM = 8, 128, 64, 128
D_HEADS, NA_KH, NA_KW = 8, 8, 16

LANES = 128
VMEM_LIMIT_BYTES = 56 * 1024 * 1024
C_QK_PAD = 2 * LANES

A_Q0, A_K0, A_V0 = 0, 8, 10
B_Q0, B_K0, B_V0 = 12, 20, 22
D_Q0, D_K0, D_V0 = 24, 32, 40
MAIN_COLS = 48 * LANES
MIX_A0, MIX_B0, MIX_C0, MIX_D0 = 0, 8, 16, 24
MIX_COLS = 32 * LANES
_BUF_SPEC = pl.BlockSpec(memory_space=pl.ANY)


def _params(*sem):
    return pltpu.CompilerParams(dimension_semantics=sem, vmem_limit_bytes=VMEM_LIMIT_BYTES)


def _pick(n, prefs):
    for p in prefs:
        if n % p == 0:
            return p
    raise ValueError(f"no tile in {prefs} divides {n}")


def _dot(a, b):
    return jnp.dot(a, b, preferred_element_type=F32)


def _dot_nt(a, b):
    return lax.dot_general(a, b, (((1,), (1,)), ((), ())), preferred_element_type=F32)


def _silu(x):
    return x / (1.0 + jnp.exp(-x))


def _rope(x, cos, sin_signed, half):
    n = x.shape[-1]
    lane = lax.broadcasted_iota(jnp.int32, x.shape, x.ndim - 1)
    first = jnp.bitwise_and(lane, 2 * half - 1) < half
    rot = jnp.where(first, pltpu.roll(x, n - half, x.ndim - 1), pltpu.roll(x, half, x.ndim - 1))
    return x * cos + rot * sin_signed


def _mod_down_kernel(s_ref, w_ref, o_ref, acc_ref):
    k = pl.program_id(1)

    @pl.when(k == 0)
    def _():
        acc_ref[...] = jnp.zeros_like(acc_ref)

    acc_ref[...] += _dot(_silu(s_ref[...]).astype(BF16), w_ref[...].astype(BF16))

    @pl.when(k == pl.num_programs(1) - 1)
    def _():
        o_ref[...] = acc_ref[...]


def _mod_up_kernel(t_ref, w_ref, o_ref):
    o_ref[...] = _dot(t_ref[...].astype(BF16), w_ref[...].astype(BF16))


def _modulation(c, c_ctx, w_down, w_up):
    n_layers, d, rank = w_down.shape
    n_out = w_up.shape[2]
    s = jnp.zeros((8, d), F32).at[0].set(c[0]).at[1].set(c_ctx)
    tk = _pick(d, (1024, 512, 256, 128))
    t = pl.pallas_call(
        _mod_down_kernel,
        out_shape=jax.ShapeDtypeStruct((n_layers, 8, rank), F32),
        grid=(n_layers, d // tk),
        in_specs=[pl.BlockSpec((8, tk), lambda l, k: (0, k)),
                  pl.BlockSpec((None, tk, rank), lambda l, k: (l, k, 0))],
        out_specs=pl.BlockSpec((None, 8, rank), lambda l, k: (l, 0, 0)),
        scratch_shapes=[pltpu.VMEM((8, rank), F32)],
        compiler_params=_params("parallel", "arbitrary"),
        name="mod_down",
    )(s, w_down)
    tn = _pick(n_out, (2048, 1024, 512, 256, 128))
    m = pl.pallas_call(
        _mod_up_kernel,
        out_shape=jax.ShapeDtypeStruct((n_layers, 8, n_out), F32),
        grid=(n_layers, n_out // tn),
        in_specs=[pl.BlockSpec((None, 8, rank), lambda l, j: (l, 0, 0)),
                  pl.BlockSpec((None, rank, tn), lambda l, j: (l, 0, j))],
        out_specs=pl.BlockSpec((None, 8, tn), lambda l, j: (l, 0, j)),
        compiler_params=_params("parallel", "parallel"),
        name="mod_up",
    )(t, w_up)
    return m[:, :2, :].reshape(n_layers, 2, N_MOD, 1, d)


def _norm_mod_kernel(x_ref, gain_ref, shift_ref, scale_ref, o_ref):
    x = x_ref[...]
    y = x * lax.rsqrt(jnp.mean(x * x, axis=-1, keepdims=True) + NORM_EPS) * gain_ref[...]
    o_ref[...] = (y * (1.0 + scale_ref[...]) + shift_ref[...]).astype(o_ref.dtype)


def _norm_mod(x, gain, mod_l, k_shift, k_scale, n_lat, tr):
    rows, d = x.shape
    n_lat_blocks = n_lat // tr

    def mod_spec(k):
        return pl.BlockSpec((None, None, 1, d),
                            lambda i: (jnp.where(i >= n_lat_blocks, 1, 0), k, 0, 0))

    return pl.pallas_call(
        _norm_mod_kernel,
        out_shape=jax.ShapeDtypeStruct((rows, d), BF16),
        grid=(rows // tr,),
        in_specs=[pl.BlockSpec((tr, d), lambda i: (i, 0)),
                  pl.BlockSpec((1, d), lambda i: (0, 0)),
                  mod_spec(k_shift), mod_spec(k_scale)],
        out_specs=pl.BlockSpec((tr, d), lambda i: (i, 0)),
        compiler_params=_params("parallel"),
        name="norm_mod",
    )(x, gain.reshape(1, d), mod_l, mod_l)


def _final_norm_kernel(x_ref, gain_ref, o_ref):
    x = x_ref[...]
    o_ref[...] = x * lax.rsqrt(jnp.mean(x * x, axis=-1, keepdims=True) + NORM_EPS) * gain_ref[...]


def _final_norm(x, gain, tr):
    rows, d = x.shape
    return pl.pallas_call(
        _final_norm_kernel,
        out_shape=jax.ShapeDtypeStruct((rows, d), F32),
        grid=(rows // tr,),
        in_specs=[pl.BlockSpec((tr, d), lambda i: (i, 0)),
                  pl.BlockSpec((1, d), lambda i: (0, 0))],
        out_specs=pl.BlockSpec((tr, d), lambda i: (i, 0)),
        compiler_params=_params("parallel"),
        name="final_norm",
    )(x, gain.reshape(1, d))


def _gate_up_kernel(h_ref, wg_ref, wu_ref, o_ref):
    h = h_ref[...]
    g = _dot(h, wg_ref[...].astype(BF16))
    u = _dot(h, wu_ref[...].astype(BF16))
    o_ref[...] = (_silu(g) * u).astype(o_ref.dtype)


def _gate_up(h, w_gu, layer, n_rows, tm):
    d = h.shape[1]
    f = w_gu.shape[2] // 2
    tn = _pick(f, (256, 128))
    nj = f // tn
    return pl.pallas_call(
        _gate_up_kernel,
        out_shape=jax.ShapeDtypeStruct((n_rows, f), BF16),
        grid=(n_rows // tm, nj),
        in_specs=[pl.BlockSpec((tm, d), lambda i, j: (i, 0)),
                  pl.BlockSpec((None, d, tn), lambda i, j: (layer, 0, j)),
                  pl.BlockSpec((None, d, tn), lambda i, j: (layer, 0, j + nj))],
        out_specs=pl.BlockSpec((tm, tn), lambda i, j: (i, j)),
        compiler_params=_params("parallel", "arbitrary"),
        name="gate_up",
    )(h, w_gu, w_gu)


def _residual_kernel(a_ref, w_ref, x_ref, g_ref, o_ref, *, coef, n_lat, tm):
    y = _dot(a_ref[...], w_ref[...])
    row = pl.program_id(0) * tm + lax.broadcasted_iota(jnp.int32, (tm, 1), 0)
    gate = jnp.where(row < n_lat, g_ref[0], g_ref[1])
    o_ref[...] = x_ref[...] + (coef * gate) * y


def _residual_matmul(a, w, layer, x, mod_l, k_gate, coef, n_rows, n_lat, tm):
    kdim = a.shape[1]
    d = w.shape[2]
    tn = _pick(d, (512, 256, 128))
    return pl.pallas_call(
        functools.partial(_residual_kernel, coef=coef, n_lat=n_lat, tm=tm),
        out_shape=jax.ShapeDtypeStruct((n_rows, d), F32),
        grid=(n_rows // tm, d // tn),
        in_specs=[pl.BlockSpec((tm, kdim), lambda i, j: (i, 0)),
                  pl.BlockSpec((None, kdim, tn), lambda i, j: (layer, 0, j)),
                  pl.BlockSpec((tm, tn), lambda i, j: (i, j)),
                  pl.BlockSpec((2, None, 1, tn), lambda i, j: (0, k_gate, 0, j))],
        out_specs=pl.BlockSpec((tm, tn), lambda i, j: (i, j)),
        compiler_params=_params("parallel", "arbitrary"),
        name="residual_matmul",
    )(a, w, x, mod_l)


IN_TILE_BLOCKS = 4
_MAIN_OPS = (("rope_scale",) * A_HEADS + ("rope",) * A_KV_HEADS + ("copy",) * A_KV_HEADS
             + ("norm_q_rope_scale",) * B_HEADS + ("norm_k_rope",) * B_KV_HEADS + ("copy",) * B_KV_HEADS
             + ("scale",) * D_HEADS + ("copy",) * (2 * D_HEADS))


def _head_norm(x, g):
    return x * lax.rsqrt(jnp.mean(x * x, axis=-1, keepdims=True) + NORM_EPS) * g


def _in_proj_kernel(h_ref, w_ref, cos_ref, sin_ref, gq_ref, gk_ref, o_ref, *, row_chunks):
    j = pl.program_id(1)
    scale = HEAD_DIM ** -0.5 * LOG2E
    chunk = h_ref.shape[0] // row_chunks

    def apply(op, x, rows):
        if op in ("norm_q_rope_scale", "norm_k_rope"):
            x = _head_norm(x, (gq_ref if op == "norm_q_rope_scale" else gk_ref)[...])
        if "rope" in op:
            x = _rope(x, cos_ref[rows, :], sin_ref[rows, :], 32)
        return x * scale if "scale" in op else x

    n_tiles = len(_MAIN_OPS) // IN_TILE_BLOCKS
    recipes = [_MAIN_OPS[t * IN_TILE_BLOCKS:(t + 1) * IN_TILE_BLOCKS] for t in range(n_tiles)]
    for recipe in sorted(set(recipes)):
        tiles = [t for t in range(n_tiles) if recipes[t] == recipe]
        hit = functools.reduce(jnp.logical_or, [j == t for t in tiles])

        @pl.when(hit)
        def _(recipe=recipe):
            for r in range(row_chunks):
                rows = slice(r * chunk, (r + 1) * chunk)
                y = _dot(h_ref[rows, :], w_ref[...])
                for b, op in enumerate(recipe):
                    cols = slice(b * LANES, (b + 1) * LANES)
                    o_ref[rows, cols] = apply(op, y[:, cols], rows).astype(o_ref.dtype)


def _in_proj(h, w_main, layer, tables, g_q, g_k, tm):
    rows, d = h.shape
    tn = IN_TILE_BLOCKS * LANES
    assert w_main.shape[2] == MAIN_COLS and MAIN_COLS % tn == 0
    row_chunks = 4 if tm % 64 == 0 else 1
    return pl.pallas_call(
        functools.partial(_in_proj_kernel, row_chunks=row_chunks),
        out_shape=jax.ShapeDtypeStruct((rows, MAIN_COLS), BF16),
        grid=(rows // tm, MAIN_COLS // tn),
        in_specs=[pl.BlockSpec((tm, d), lambda i, j: (i, 0)),
                  pl.BlockSpec((None, d, tn), lambda i, j: (layer, 0, j)),
                  pl.BlockSpec((tm, LANES), lambda i, j: (i, 0)),
                  pl.BlockSpec((tm, LANES), lambda i, j: (i, 0)),
                  pl.BlockSpec((1, LANES), lambda i, j: (0, 0)),
                  pl.BlockSpec((1, LANES), lambda i, j: (0, 0))],
        out_specs=pl.BlockSpec((tm, tn), lambda i, j: (i, j)),
        compiler_params=_params("parallel", "arbitrary"),
        name="in_proj",
    )(h, w_main, tables[0], tables[1], g_q.reshape(1, -1), g_k.reshape(1, -1))


def _mla_down_kernel(h_ref, w_ref, cosc_ref, sinc_ref, gq_ref, gkv_ref, qa_ref, kva_ref, kpe_ref):
    q_rank, kv_rank = qa_ref.shape[1], kva_ref.shape[1]
    y = _dot(h_ref[...], w_ref[...])
    qa_ref[...] = _head_norm(y[:, :q_rank], gq_ref[...]).astype(qa_ref.dtype)
    kva_ref[...] = _head_norm(y[:, q_rank:q_rank + kv_rank], gkv_ref[...]).astype(kva_ref.dtype)
    kpe = y[:, q_rank + kv_rank:q_rank + kv_rank + LANES]
    kpe_ref[...] = _rope(kpe, cosc_ref[...], sinc_ref[...], 16).astype(kpe_ref.dtype)


def _mla_down(h, w_c, layer, tables, g_cq, g_ckv, tm):
    rows, d = h.shape
    q_rank, kv_rank = g_cq.shape[0], g_ckv.shape[0]
    cols = w_c.shape[2]
    row_spec = lambda w: pl.BlockSpec((tm, w), lambda i: (i, 0))
    vec_spec = lambda w: pl.BlockSpec((1, w), lambda i: (0, 0))
    return pl.pallas_call(
        _mla_down_kernel,
        out_shape=(jax.ShapeDtypeStruct((rows, q_rank), BF16),
                   jax.ShapeDtypeStruct((rows, kv_rank), BF16),
                   jax.ShapeDtypeStruct((rows, LANES), BF16)),
        grid=(rows // tm,),
        in_specs=[row_spec(d), pl.BlockSpec((None, d, cols), lambda i: (layer, 0, 0)),
                  row_spec(LANES), row_spec(LANES), vec_spec(q_rank), vec_spec(kv_rank)],
        out_specs=(row_spec(q_rank), row_spec(kv_rank), row_spec(LANES)),
        compiler_params=_params("parallel"),
        name="mla_down",
    )(h, w_c, tables[2], tables[3], g_cq.reshape(1, -1), g_ckv.reshape(1, -1))


def _rope_tables(n_lat, n_ctx):
    t = jnp.arange(n_lat, dtype=jnp.int32)
    row, col = (t // GRID_W).astype(F32), (t % GRID_W).astype(F32)
    lane = jnp.arange(LANES)

    def table(dim, live):
        half = dim // 2
        pair = half // 2
        inv = ROPE_THETA ** (-jnp.arange(0, half, 2, dtype=F32) / half)
        inv_lane = inv[lane % pair]
        pos = jnp.where(((lane // half) % 2 == 0)[None, :], row[:, None], col[:, None])
        ang = pos * inv_lane[None, :]
        sign = jnp.where((lane % half) < pair, -1.0, 1.0).astype(F32)
        on = (lane < live)[None, :]
        cos = jnp.where(on, jnp.cos(ang), 1.0)
        sin = jnp.where(on, jnp.sin(ang) * sign[None, :], 0.0)
        pad = lambda a, v: jnp.concatenate([a, jnp.full((n_ctx, LANES), v, F32)], axis=0)
        return pad(cos, 1.0), pad(sin, 0.0)

    cos, sin = table(HEAD_DIM, LANES)
    cosc, sinc = table(C_ROPE_DIM, C_ROPE_DIM)
    return cos, sin, cosc, sinc


def _cq_up_kernel(a_ref, w_ref, cos_ref, sin_ref, o_ref, *, scale):
    y = _dot(a_ref[...], w_ref[...])
    cos, sin = cos_ref[...], sin_ref[...]
    for h in range(C_HEADS):
        c0 = h * C_QK_PAD
        o_ref[:, c0:c0 + LANES] = (y[:, c0:c0 + LANES] * scale).astype(o_ref.dtype)
        pe = _rope(y[:, c0 + LANES:c0 + 2 * LANES], cos, sin, 16)
        o_ref[:, c0 + LANES:c0 + 2 * LANES] = (pe * scale).astype(o_ref.dtype)


def _ckv_up_kernel(a_ref, w_ref, kpe_ref, k_ref, v_ref):
    y = _dot(a_ref[...], w_ref[...])
    kpe = kpe_ref[...]
    for h in range(C_HEADS):
        k_ref[:, h * C_QK_PAD:h * C_QK_PAD + LANES] = y[:, h * LANES:(h + 1) * LANES].astype(k_ref.dtype)
        k_ref[:, h * C_QK_PAD + LANES:(h + 1) * C_QK_PAD] = kpe
    v_ref[...] = y[:, C_HEADS * LANES:].astype(v_ref.dtype)


def _mla_expand(qa, kva, kpe, w_q, w_kv, layer, cosc, sinc, tm):
    rows, q_rank = qa.shape
    kv_rank = kva.shape[1]
    scale = (C_NOPE_DIM + C_ROPE_DIM) ** -0.5 * LOG2E
    row_spec = lambda w: pl.BlockSpec((tm, w), lambda i: (i, 0))
    layer_spec = lambda w: pl.BlockSpec((None,) + w.shape[1:], lambda i: (layer, 0, 0))
    qc = pl.pallas_call(
        functools.partial(_cq_up_kernel, scale=scale),
        out_shape=jax.ShapeDtypeStruct((rows, C_HEADS * C_QK_PAD), BF16),
        grid=(rows // tm,),
        in_specs=[row_spec(q_rank), layer_spec(w_q), row_spec(LANES), row_spec(LANES)],
        out_specs=row_spec(C_HEADS * C_QK_PAD),
        compiler_params=_params("parallel"),
        name="mla_q_up",
    )(qa, w_q, cosc, sinc)
    kc, vc = pl.pallas_call(
        _ckv_up_kernel,
        out_shape=(jax.ShapeDtypeStruct((rows, C_HEADS * C_QK_PAD), BF16),
                   jax.ShapeDtypeStruct((rows, C_HEADS * C_V_DIM), BF16)),
        grid=(rows // tm,),
        in_specs=[row_spec(kv_rank), layer_spec(w_kv), row_spec(LANES)],
        out_specs=(row_spec(C_HEADS * C_QK_PAD), row_spec(C_HEADS * C_V_DIM)),
        compiler_params=_params("parallel"),
        name="mla_kv_up",
    )(kva, w_kv, kpe)
    return qc, kc, vc


def _flash_kernel(q_ref, k_ref, v_ref, buf_ref, o_ref, m_sc, l_sc, acc_sc, *, group, dk, tc, tk, n_lat, n_ctx):
    del buf_ref
    tq = q_ref.shape[0]
    chains = [(g, r) for g in range(group) for r in range(tq // tc)]

    def chunk(k, v, first):
        n_blocks = k.shape[0] // LANES
        for ci, (g, r) in enumerate(chains):
            s = _dot_nt(q_ref[r * tc:(r + 1) * tc, g * dk:(g + 1) * dk], k)
            blocks = [s[:, b * LANES:(b + 1) * LANES] for b in range(n_blocks)]
            mx = blocks[0]
            for blk in blocks[1:]:
                mx = jnp.maximum(mx, blk)
            m_new = jnp.broadcast_to(jnp.max(mx, axis=-1, keepdims=True), (tc, LANES))
            if not first:
                m_prev = m_sc[ci]
                m_new = jnp.maximum(m_prev, m_new)
                alpha = jnp.exp2(m_prev - m_new)
            ps = [jnp.exp2(blk - m_new) for blk in blocks]
            l_new = ps[0]
            for p in ps[1:]:
                l_new = l_new + p
            pv = _dot(jnp.concatenate([p.astype(BF16) for p in ps], axis=1), v)
            if first:
                l_sc[ci] = l_new
                acc_sc[ci] = pv
            else:
                l_sc[ci] = alpha * l_sc[ci] + l_new
                acc_sc[ci] = alpha * acc_sc[ci] + pv
            m_sc[ci] = m_new

    chunk(k_ref[n_lat:n_lat + n_ctx, :], v_ref[n_lat:n_lat + n_ctx, :], True)

    def step(c, carry):
        start = pl.multiple_of(c * tk, tk)
        chunk(k_ref[pl.ds(start, tk), :], v_ref[pl.ds(start, tk), :], False)
        return carry

    lax.fori_loop(0, n_lat // tk, step, 0)
    for ci, (g, r) in enumerate(chains):
        inv = 1.0 / jnp.sum(l_sc[ci], axis=-1, keepdims=True)
        o_ref[r * tc:(r + 1) * tc, g * LANES:(g + 1) * LANES] = (acc_sc[ci] * inv).astype(o_ref.dtype)


def _flash(q_arr, k_arr, v_arr, o_buf, *, n_kv, group, dk, q_blk0, k_blk0, v_blk0, o_blk0, tq, tc, tk, n_lat,
           n_ctx):
    rows = k_arr.shape[0]
    n_chains = group * (tq // tc)
    stat = pltpu.VMEM((n_chains, tc, LANES), F32)
    return pl.pallas_call(
        functools.partial(_flash_kernel, group=group, dk=dk, tc=tc, tk=tk, n_lat=n_lat, n_ctx=n_ctx),
        out_shape=jax.ShapeDtypeStruct(o_buf.shape, o_buf.dtype),
        grid=(n_kv, n_lat // tq),
        in_specs=[pl.BlockSpec((tq, group * dk), lambda g, i: (i, q_blk0 + g)),
                  pl.BlockSpec((rows, dk), lambda g, i: (0, k_blk0 + g)),
                  pl.BlockSpec((rows, LANES), lambda g, i: (0, v_blk0 + g)),
                  _BUF_SPEC],
        out_specs=pl.BlockSpec((tq, group * LANES), lambda g, i: (i, o_blk0 + g)),
        scratch_shapes=[stat, stat, stat],
        input_output_aliases={3: 0},
        compiler_params=_params("parallel", "arbitrary"),
        name="flash",
    )(q_arr, k_arr, v_arr, o_buf)


def _lane_blocks(s):
    return [s[:, b * LANES:(b + 1) * LANES] for b in range(s.shape[1] // LANES)]


def _softmax_pv(segments, sink=None):
    blocks = [blk for blks, _ in segments for blk in blks]
    rows = blocks[0].shape[0]
    mx = blocks[0]
    for blk in blocks[1:]:
        mx = jnp.maximum(mx, blk)
    m = jnp.broadcast_to(jnp.max(mx, axis=-1, keepdims=True), (rows, LANES))
    if sink is not None:
        m = jnp.maximum(m, sink)
    out = l = None
    for blks, v in segments:
        ps = [jnp.exp2(blk - m) for blk in blks]
        for p in ps:
            l = p if l is None else l + p
        pv = _dot(jnp.concatenate([p.astype(BF16) for p in ps], axis=1), v)
        out = pv if out is None else out + pv
    if sink is not None:
        lane = lax.broadcasted_iota(jnp.int32, (rows, LANES), 1)
        l = l + jnp.where(lane == 0, jnp.exp2(sink - m), 0.0)
    return out * (1.0 / jnp.sum(l, axis=-1, keepdims=True))


def _window_kernel(sink_ref, q_ref, k_ref, v_ref, buf_ref, o_ref, *, group, tq, n_lat, n_ctx, window):
    del buf_ref
    g, n = pl.program_id(0), pl.program_id(1)
    span = tq + 2 * LANES
    kc, vc = k_ref[n_lat:n_lat + n_ctx, :], v_ref[n_lat:n_lat + n_ctx, :]
    row = jnp.bitwise_and(lax.broadcasted_iota(jnp.int32, (group * tq, LANES), 0), tq - 1)
    lane_minus_row = lax.broadcasted_iota(jnp.int32, (group * tq, LANES), 1) - row
    sink = jnp.concatenate([jnp.full((tq, LANES), sink_ref[g * group + i], F32) for i in range(group)], axis=0)
    n_sub = q_ref.shape[0] // tq
    for sub in range(n_sub):
        rows = slice(sub * tq, (sub + 1) * tq)
        q0 = (n * n_sub + sub) * tq
        start = pl.multiple_of(jnp.clip(q0 - LANES, 0, n_lat - span), LANES)
        kw, vw = k_ref[pl.ds(start, span), :], v_ref[pl.ds(start, span), :]
        q = jnp.concatenate([q_ref[rows, i * HEAD_DIM:(i + 1) * HEAD_DIM] for i in range(group)], axis=0)
        s_w = [jnp.where(jnp.abs(lane_minus_row + (start - q0 + b * LANES)) <= window, blk, NEG_INF)
               for b, blk in enumerate(_lane_blocks(_dot_nt(q, kw)))]
        out = _softmax_pv([(s_w, vw), (_lane_blocks(_dot_nt(q, kc)), vc)], sink=sink)
        for i in range(group):
            o_ref[rows, i * HEAD_DIM:(i + 1) * HEAD_DIM] = out[i * tq:(i + 1) * tq].astype(o_ref.dtype)


def _window_attention(main, sink, o_buf, n_lat, n_ctx):
    rows = main.shape[0]
    group = A_HEADS // A_KV_HEADS
    tq = _pick(n_lat, (256, 128))
    assert A_WINDOW <= LANES and n_lat >= tq + 2 * LANES
    n_sub = _pick(n_lat // tq, (2, 1))
    return pl.pallas_call(
        functools.partial(_window_kernel, group=group, tq=tq, n_lat=n_lat, n_ctx=n_ctx, window=A_WINDOW),
        out_shape=jax.ShapeDtypeStruct(o_buf.shape, o_buf.dtype),
        grid=(A_KV_HEADS, n_lat // (n_sub * tq)),
        in_specs=[pl.BlockSpec(memory_space=pltpu.SMEM),
                  pl.BlockSpec((n_sub * tq, group * HEAD_DIM), lambda g, n: (n, g)),
                  pl.BlockSpec((rows, HEAD_DIM), lambda g, n: (0, A_K0 + g)),
                  pl.BlockSpec((rows, HEAD_DIM), lambda g, n: (0, A_V0 + g)),
                  _BUF_SPEC],
        out_specs=pl.BlockSpec((n_sub * tq, group * HEAD_DIM), lambda g, n: (n, MIX_A0 // group + g)),
        input_output_aliases={4: 0},
        compiler_params=_params("parallel", "arbitrary"),
        name="window_attention",
    )(sink, main, main, main, o_buf)


NBR_Q_ROWS = 8
NBR_KEY_ROWS = 16


def _nbr_key_start(rb, n_grid_rows):
    return jnp.clip(rb * NBR_Q_ROWS - NA_KH // 2, 0, n_grid_rows - NBR_KEY_ROWS)


def _nbr_kernel(q_ref, k_ref, v_ref, b_ref, buf_ref, o_ref, *, chain_rows, n_grid_rows, n_lat, n_ctx):
    del buf_ref
    k0 = pl.multiple_of(_nbr_key_start(pl.program_id(1), n_grid_rows) * GRID_W, (NA_KH // 2) * GRID_W)
    n_keys = NBR_KEY_ROWS * GRID_W
    kw, vw = k_ref[pl.ds(k0, n_keys), :], v_ref[pl.ds(k0, n_keys), :]
    kc, vc = k_ref[n_lat:n_lat + n_ctx, :], v_ref[n_lat:n_lat + n_ctx, :]
    rows_per_chain = chain_rows // GRID_W
    for sub in range(q_ref.shape[0] // chain_rows):
        rows = slice(sub * chain_rows, (sub + 1) * chain_rows)
        q = q_ref[rows, :]
        s_nb = [blk + jnp.concatenate([b_ref[sub * rows_per_chain + jj, kp] for jj in range(rows_per_chain)], axis=0)
                for kp, blk in enumerate(_lane_blocks(_dot_nt(q, kw)))]
        out = _softmax_pv([(s_nb, vw), (_lane_blocks(_dot_nt(q, kc)), vc)])
        o_ref[rows, :] = out.astype(o_ref.dtype)


def _nbr_bias_table(rel_bias):
    assert 2 * GRID_W == LANES
    n_heads = rel_bias.shape[0]
    hp = lax.Precision.HIGHEST
    off = jnp.array([0, NA_KH // 2, NA_KH], jnp.int32)[:, None, None]
    j = jnp.arange(NBR_Q_ROWS)[None, :, None]
    kr = jnp.arange(NBR_KEY_ROWS)[None, None, :]
    centred = j - NA_KH // 2
    rs = jnp.stack([jnp.maximum(centred[0], 0), centred[0], jnp.minimum(centred[0], 0)])
    key_row = kr - off
    row_ok = (key_row >= rs) & (key_row < rs + NA_KH)
    row_sel = jax.nn.one_hot(key_row - j + (NA_KH - 1), 2 * NA_KH - 1, dtype=F32) * row_ok[..., None]
    feat = jnp.einsum("vjka,hab->vhjkb", row_sel, rel_bias.astype(F32) * LOG2E, precision=hp)
    bad = jnp.broadcast_to((~row_ok).astype(F32)[:, None, :, :, None], feat.shape[:-1] + (1,))
    feat = jnp.concatenate([feat, bad], axis=-1)
    feat = feat.reshape(feat.shape[:3] + (NBR_KEY_ROWS // 2, 2 * feat.shape[-1]))
    feat = jnp.concatenate([feat, jnp.ones(feat.shape[:-1] + (1,), F32)], axis=-1)
    c = jnp.arange(GRID_W)[:, None]
    kc = jnp.arange(GRID_W)[None, :]
    cstart = jnp.clip(c - NA_KW // 2, 0, GRID_W - NA_KW)
    col_ok = (kc >= cstart) & (kc < cstart + NA_KW)
    col_sel = jax.nn.one_hot(kc - c + (NA_KW - 1), 2 * NA_KW - 1, dtype=F32)
    per_row = jnp.concatenate([col_sel.transpose(0, 2, 1), jnp.full((GRID_W, 1, GRID_W), NEG_INF, F32)], axis=1)
    sel = jnp.einsum("pq,cxk->cpxqk", jnp.eye(2, dtype=F32), per_row).reshape(GRID_W, 4 * NA_KW, LANES)
    col_bad = jnp.tile(jnp.where(col_ok, 0.0, NEG_INF).astype(F32), (1, 2))[:, None, :]
    sel = jnp.concatenate([sel, col_bad], axis=1)
    return jnp.einsum("vhjkx,cxl->vhjkcl", feat, sel, precision=hp)


def _nbr_attention(main, rel_bias, o_buf, n_lat, n_ctx):
    rows = main.shape[0]
    n_grid_rows = n_lat // GRID_W
    assert n_grid_rows % NBR_Q_ROWS == 0 and n_grid_rows >= NBR_KEY_ROWS
    assert NBR_KEY_ROWS >= NBR_Q_ROWS + NA_KH and NA_KH // 2 * 2 == NA_KH
    n_blocks = n_grid_rows // NBR_Q_ROWS
    tq = NBR_Q_ROWS * GRID_W

    def variant(rb):
        return (rb * NBR_Q_ROWS - _nbr_key_start(rb, n_grid_rows)) // (NA_KH // 2)

    return pl.pallas_call(
        functools.partial(_nbr_kernel, chain_rows=tq // 2, n_grid_rows=n_grid_rows, n_lat=n_lat, n_ctx=n_ctx),
        out_shape=jax.ShapeDtypeStruct(o_buf.shape, o_buf.dtype),
        grid=(D_HEADS, n_blocks),
        in_specs=[pl.BlockSpec((tq, HEAD_DIM), lambda h, rb: (rb, D_Q0 + h)),
                  pl.BlockSpec((rows, HEAD_DIM), lambda h, rb: (0, D_K0 + h)),
                  pl.BlockSpec((rows, HEAD_DIM), lambda h, rb: (0, D_V0 + h)),
                  pl.BlockSpec((None, None, NBR_Q_ROWS, NBR_KEY_ROWS // 2, GRID_W, LANES),
                               lambda h, rb: (variant(rb), h, 0, 0, 0, 0)),
                  _BUF_SPEC],
        out_specs=pl.BlockSpec((tq, HEAD_DIM), lambda h, rb: (rb, MIX_D0 + h)),
        input_output_aliases={4: 0},
        compiler_params=_params("parallel", "arbitrary"),
        name="nbr_attention",
    )(main, main, main, _nbr_bias_table(rel_bias), o_buf)


def _ctx_attn_kernel(sink_ref, q_ref, k_ref, v_ref, buf_ref, o_ref):
    del buf_ref
    sink = sink_ref[pl.program_id(0)]
    s = _dot_nt(q_ref[...], k_ref[...])
    m = jnp.maximum(jnp.max(s, axis=-1, keepdims=True), sink)
    p = jnp.exp2(s - m)
    denom = jnp.sum(p, axis=-1, keepdims=True) + jnp.exp2(sink - m)
    o_ref[...] = (_dot(p.astype(BF16), v_ref[...]) * (1.0 / denom)).astype(o_ref.dtype)


def _ctx_attention(q_arr, k_arr, v_arr, sink, o_buf, *, n_heads, group, dk, q_blk0, k_blk0, v_blk0, o_blk0, n_lat,
                   n_ctx):
    rb = n_lat // n_ctx
    return pl.pallas_call(
        _ctx_attn_kernel,
        out_shape=jax.ShapeDtypeStruct(o_buf.shape, o_buf.dtype),
        grid=(n_heads,),
        in_specs=[pl.BlockSpec(memory_space=pltpu.SMEM),
                  pl.BlockSpec((n_ctx, dk), lambda h: (rb, q_blk0 + h)),
                  pl.BlockSpec((n_ctx, dk), lambda h: (rb, k_blk0 + h // group)),
                  pl.BlockSpec((n_ctx, LANES), lambda h: (rb, v_blk0 + h // group)),
                  _BUF_SPEC],
        out_specs=pl.BlockSpec((n_ctx, LANES), lambda h: (rb, o_blk0 + h)),
        input_output_aliases={4: 0},
        compiler_params=_params("parallel"),
        name="ctx_attention",
    )(sink, q_arr, k_arr, v_arr, o_buf)


def _w_in_relayout_kernel(w_ref, main_ref, c_ref, *, o2, o3):
    main_ref[:, :o2] = w_ref[:, :o2]
    main_ref[:, o2:] = w_ref[:, o3:]
    c_ref[:, :o3 - o2] = w_ref[:, o2:o3]
    c_ref[:, o3 - o2:] = jnp.zeros((c_ref.shape[0], c_ref.shape[1] - (o3 - o2)), c_ref.dtype)


def _relayout_w_in(w, q_rank, kv_rank):
    n_layers, d, cols = w.shape
    a_cols = (A_HEADS + 2 * A_KV_HEADS) * HEAD_DIM
    b_cols = (B_HEADS + 2 * B_KV_HEADS) * HEAD_DIM
    c_cols = q_rank + kv_rank + C_ROPE_DIM
    o2, o3 = a_cols + b_cols, a_cols + b_cols + c_cols
    assert cols - c_cols == MAIN_COLS
    c_wide = q_rank + kv_rank + LANES
    tr = _pick(d, (256, 128))
    return pl.pallas_call(
        functools.partial(_w_in_relayout_kernel, o2=o2, o3=o3),
        out_shape=(jax.ShapeDtypeStruct((n_layers, d, MAIN_COLS), BF16),
                   jax.ShapeDtypeStruct((n_layers, d, c_wide), BF16)),
        grid=(n_layers, d // tr),
        in_specs=[pl.BlockSpec((None, tr, cols), lambda l, i: (l, i, 0))],
        out_specs=(pl.BlockSpec((None, tr, MAIN_COLS), lambda l, i: (l, i, 0)),
                   pl.BlockSpec((None, tr, c_wide), lambda l, i: (l, i, 0))),
        compiler_params=_params("parallel", "parallel"),
        name="w_in_relayout",
    )(w)


def _relayout_w_q_up(w):
    lead = w.shape[:-1]
    w = w.astype(BF16).reshape(lead + (C_HEADS, C_NOPE_DIM + C_ROPE_DIM))
    w = jnp.pad(w, ((0, 0),) * (len(lead) + 1) + ((0, C_QK_PAD - C_NOPE_DIM - C_ROPE_DIM),))
    return w.reshape(lead + (C_HEADS * C_QK_PAD,))


def _relayout_w_kv_up(w):
    lead = w.shape[:-1]
    w = w.astype(BF16).reshape(lead + (C_HEADS, C_NOPE_DIM + C_V_DIM))
    return jnp.concatenate([w[..., :C_NOPE_DIM].reshape(lead + (-1,)), w[..., C_NOPE_DIM:].reshape(lead + (-1,))],
                           axis=-1)


def kernel(x, c, ctx, c_ctx, w_mod_down, w_mod_up, norm_ffn1, ffn1_w_gu, ffn1_w_down, norm_mix, w_in,
           a_sink, b_q_norm, b_k_norm, c_q_norm, c_kv_norm, c_w_q_up, c_w_kv_up, d_rel_bias, w_out,
           norm_ffn2, ffn2_w_gu, ffn2_w_down, final_norm):
    bsz, n_lat, d = x.shape
    n_ctx = ctx.shape[1]
    n_layers = w_in.shape[0]
    q_rank, kv_rank = c_q_norm.shape[1], c_kv_norm.shape[1]
    assert bsz == 1 and n_lat % GRID_W == 0 and n_lat % n_ctx == 0 and n_ctx % LANES == 0
    rows = n_lat + n_ctx
    tr = n_ctx
    tm_all = (_pick(rows, (1408, 768, 512, 256, 128)), _pick(rows, (768, 512, 256, 128)))
    tm_lat = (_pick(n_lat, (1024, 512, 256, 128)),) * 2
    flash_tc = _pick(n_lat, (1024, 512, 256, 128))
    flash_tk = _pick(n_lat, (2048, 1024, 512, 256, 128))

    mod = _modulation(c, c_ctx, w_mod_down, w_mod_up)
    tables = _rope_tables(n_lat, n_ctx)
    no_sink = jnp.full((max(B_HEADS, C_HEADS, D_HEADS),), NEG_INF, F32)

    w_in_main, w_in_c = _relayout_w_in(w_in.astype(BF16), q_rank, kv_rank)
    w_q_up_b, w_kv_up_b = _relayout_w_q_up(c_w_q_up), _relayout_w_kv_up(c_w_kv_up)
    w_dn1, w_dn2, w_out_b = ffn1_w_down.astype(BF16), ffn2_w_down.astype(BF16), w_out.astype(BF16)

    xs = jnp.concatenate([x[0], ctx[0]], axis=0)

    def ffn(xs, gain, w_gu, w_down, l, mod_l, k0, n_rows, tm):
        h = _norm_mod(xs, gain, mod_l, k0, k0 + 1, n_lat, tr)
        act = _gate_up(h, w_gu, l, n_rows, tm[0])
        return _residual_matmul(act, w_down, l, xs, mod_l, k0 + 2, 0.5, n_rows, n_lat, tm[1])

    for l in range(n_layers):
        need_ctx = l < n_layers - 1
        n_rows, tm = (rows, tm_all) if need_ctx else (n_lat, tm_lat)
        mod_l = mod[l]
        xs = ffn(xs, norm_ffn1[l], ffn1_w_gu, w_dn1, l, mod_l, 0, rows, tm_all)

        h = _norm_mod(xs, norm_mix[l], mod_l, 3, 4, n_lat, tr)
        main = _in_proj(h, w_in_main, l, tables, b_q_norm[l], b_k_norm[l], tm_all[0])
        qa, kva, kpe = _mla_down(h, w_in_c, l, tables, c_q_norm[l], c_kv_norm[l], tm_all[1])
        qc, kc, vc = _mla_expand(qa, kva, kpe, w_q_up_b, w_kv_up_b, l, tables[2], tables[3], tr)

        sink = a_sink[l] * LOG2E
        lat = dict(n_lat=n_lat, n_ctx=n_ctx)
        o = jnp.zeros((n_rows, MIX_COLS), BF16)
        o = _window_attention(main, sink, o, **lat)
        o = _flash(main, main, main, o, n_kv=B_KV_HEADS, group=B_HEADS // B_KV_HEADS, dk=HEAD_DIM,
                   q_blk0=B_Q0 // (B_HEADS // B_KV_HEADS), k_blk0=B_K0, v_blk0=B_V0,
                   o_blk0=MIX_B0 // (B_HEADS // B_KV_HEADS), tq=flash_tc, tc=flash_tc, tk=flash_tk, **lat)
        o = _flash(qc, kc, vc, o, n_kv=C_HEADS, group=1, dk=C_QK_PAD, q_blk0=0, k_blk0=0, v_blk0=0, o_blk0=MIX_C0,
                   tq=_pick(n_lat, (4 * flash_tc, 2 * flash_tc, flash_tc)), tc=flash_tc, tk=flash_tk, **lat)
        o = _nbr_attention(main, d_rel_bias[l], o, **lat)
        if need_ctx:
            o = _ctx_attention(main, main, main, sink, o, n_heads=A_HEADS, group=A_HEADS // A_KV_HEADS,
                               dk=HEAD_DIM, q_blk0=A_Q0, k_blk0=A_K0, v_blk0=A_V0, o_blk0=MIX_A0, **lat)
            o = _ctx_attention(main, main, main, no_sink, o, n_heads=B_HEADS, group=B_HEADS // B_KV_HEADS,
                               dk=HEAD_DIM, q_blk0=B_Q0, k_blk0=B_K0, v_blk0=B_V0, o_blk0=MIX_B0, **lat)
            o = _ctx_attention(qc, kc, vc, no_sink, o, n_heads=C_HEADS, group=1, dk=C_QK_PAD,
                               q_blk0=0, k_blk0=0, v_blk0=0, o_blk0=MIX_C0, **lat)
            o = _ctx_attention(main, main, main, no_sink, o, n_heads=D_HEADS, group=1, dk=HEAD_DIM,
                               q_blk0=D_Q0, k_blk0=D_K0, v_blk0=D_V0, o_blk0=MIX_D0, **lat)

        xs = _residual_matmul(o, w_out_b, l, xs, mod_l, 5, 1.0, n_rows, n_lat, tm[1])
        xs = ffn(xs, norm_ffn2[l], ffn2_w_gu, w_dn2, l, mod_l, 6, n_rows, tm)

    return _final_norm(xs[:n_lat], final_norm, tr)[None]
```

```python
import functools

import jax
import jax.numpy as jnp
from jax import lax
from jax.experimental import pallas as pl
from jax.experimental.pallas import tpu as pltpu

F32 = jnp.float32
BF16 = jnp.bfloat16

GRID_W = 64
HEAD_DIM = 128
ROPE_THETA = 10000.0
NORM_EPS = 1e-6
NEG_INF = -1e30
LOG2E = 1.4426950408889634
N_MOD = 9
A_HEADS, A_KV_HEADS, A_WINDOW = 8, 2, 128
B_HEADS, B_KV_HEADS = 8, 2
C_HEADS, C_NOPE_DIM, C_ROPE_DIM, C_V_DIM = 8, 128, 64, 128
D_HEADS, NA_KH, NA_KW = 8, 8, 16

LANES = 128
VMEM_LIMIT_BYTES = 56 * 1024 * 1024
C_QK_PAD = 2 * LANES

A_Q0, A_K0, A_V0 = 0, 8, 10
B_Q0, B_K0, B_V0 = 12, 20, 22
D_Q0, D_K0, D_V0 = 24, 32, 40
MAIN_COLS = 48 * LANES
MIX_A0, MIX_B0, MIX_C0, MIX_D0 = 0, 8, 16, 24
MIX_COLS = 32 * LANES
_BUF_SPEC = pl.BlockSpec(memory_space=pl.ANY)


def _params(*sem):
    return pltpu.CompilerParams(dimension_semantics=sem, vmem_limit_bytes=VMEM_LIMIT_BYTES)


def _pick(n, prefs):
    for p in prefs:
        if n % p == 0:
            return p
    raise ValueError(f"no tile in {prefs} divides {n}")


def _dot(a, b):
    return jnp.dot(a, b, preferred_element_type=F32)


def _dot_nt(a, b):
    return lax.dot_general(a, b, (((1,), (1,)), ((), ())), preferred_element_type=F32)


def _silu(x):
    return x / (1.0 + jnp.exp(-x))


def _rope(x, cos, sin_signed, half):
    n = x.shape[-1]
    lane = lax.broadcasted_iota(jnp.int32, x.shape, x.ndim - 1)
    first = jnp.bitwise_and(lane, 2 * half - 1) < half
    rot = jnp.where(first, pltpu.roll(x, n - half, x.ndim - 1), pltpu.roll(x, half, x.ndim - 1))
    return x * cos + rot * sin_signed


def _mod_down_kernel(s_ref, w_ref, o_ref, acc_ref):
    k = pl.program_id(1)

    @pl.when(k == 0)
    def _():
        acc_ref[...] = jnp.zeros_like(acc_ref)

    acc_ref[...] += _dot(_silu(s_ref[...]).astype(BF16), w_ref[...].astype(BF16))

    @pl.when(k == pl.num_programs(1) - 1)
    def _():
        o_ref[...] = acc_ref[...]


def _mod_up_kernel(t_ref, w_ref, o_ref):
    o_ref[...] = _dot(t_ref[...].astype(BF16), w_ref[...].astype(BF16))


def _modulation(c, c_ctx, w_down, w_up):
    n_layers, d, rank = w_down.shape
    n_out = w_up.shape[2]
    s = jnp.zeros((8, d), F32).at[0].set(c[0]).at[1].set(c_ctx)
    tk = _pick(d, (1024, 512, 256, 128))
    t = pl.pallas_call(
        _mod_down_kernel,
        out_shape=jax.ShapeDtypeStruct((n_layers, 8, rank), F32),
        grid=(n_layers, d // tk),
        in_specs=[pl.BlockSpec((8, tk), lambda l, k: (0, k)),
                  pl.BlockSpec((None, tk, rank), lambda l, k: (l, k, 0))],
        out_specs=pl.BlockSpec((None, 8, rank), lambda l, k: (l, 0, 0)),
        scratch_shapes=[pltpu.VMEM((8, rank), F32)],
        compiler_params=_params("parallel", "arbitrary"),
        name="mod_down",
    )(s, w_down)
    tn = _pick(n_out, (2048, 1024, 512, 256, 128))
    m = pl.pallas_call(
        _mod_up_kernel,
        out_shape=jax.ShapeDtypeStruct((n_layers, 8, n_out), F32),
        grid=(n_layers, n_out // tn),
        in_specs=[pl.BlockSpec((None, 8, rank), lambda l, j: (l, 0, 0)),
                  pl.BlockSpec((None, rank, tn), lambda l, j: (l, 0, j))],
        out_specs=pl.BlockSpec((None, 8, tn), lambda l, j: (l, 0, j)),
        compiler_params=_params("parallel", "parallel"),
        name="mod_up",
    )(t, w_up)
    return m[:, :2, :].reshape(n_layers, 2, N_MOD, 1, d)


def _norm_mod_kernel(x_ref, gain_ref, shift_ref, scale_ref, o_ref):
    x = x_ref[...]
    y = x * lax.rsqrt(jnp.mean(x * x, axis=-1, keepdims=True) + NORM_EPS) * gain_ref[...]
    o_ref[...] = (y * (1.0 + scale_ref[...]) + shift_ref[...]).astype(o_ref.dtype)


def _norm_mod(x, gain, mod_l, k_shift, k_scale, n_lat, tr):
    rows, d = x.shape
    n_lat_blocks = n_lat // tr

    def mod_spec(k):
        return pl.BlockSpec((None, None, 1, d),
                            lambda i: (jnp.where(i >= n_lat_blocks, 1, 0), k, 0, 0))

    return pl.pallas_call(
        _norm_mod_kernel,
        out_shape=jax.ShapeDtypeStruct((rows, d), BF16),
        grid=(rows // tr,),
        in_specs=[pl.BlockSpec((tr, d), lambda i: (i, 0)),
                  pl.BlockSpec((1, d), lambda i: (0, 0)),
                  mod_spec(k_shift), mod_spec(k_scale)],
        out_specs=pl.BlockSpec((tr, d), lambda i: (i, 0)),
        compiler_params=_params("parallel"),
        name="norm_mod",
    )(x, gain.reshape(1, d), mod_l, mod_l)


def _final_norm_kernel(x_ref, gain_ref, o_ref):
    x = x_ref[...]
    o_ref[...] = x * lax.rsqrt(jnp.mean(x * x, axis=-1, keepdims=True) + NORM_EPS) * gain_ref[...]


def _final_norm(x, gain, tr):
    rows, d = x.shape
    return pl.pallas_call(
        _final_norm_kernel,
        out_shape=jax.ShapeDtypeStruct((rows, d), F32),
        grid=(rows // tr,),
        in_specs=[pl.BlockSpec((tr, d), lambda i: (i, 0)),
                  pl.BlockSpec((1, d), lambda i: (0, 0))],
        out_specs=pl.BlockSpec((tr, d), lambda i: (i, 0)),
        compiler_params=_params("parallel"),
        name="final_norm",
    )(x, gain.reshape(1, d))


def _gate_up_kernel(h_ref, wg_ref, wu_ref, o_ref):
    h = h_ref[...]
    g = _dot(h, wg_ref[...].astype(BF16))
    u = _dot(h, wu_ref[...].astype(BF16))
    o_ref[...] = (_silu(g) * u).astype(o_ref.dtype)


def _gate_up(h, w_gu, layer, n_rows, tm):
    d = h.shape[1]
    f = w_gu.shape[2] // 2
    tn = _pick(f, (256, 128))
    nj = f // tn
    return pl.pallas_call(
        _gate_up_kernel,
        out_shape=jax.ShapeDtypeStruct((n_rows, f), BF16),
        grid=(n_rows // tm, nj),
        in_specs=[pl.BlockSpec((tm, d), lambda i, j: (i, 0)),
                  pl.BlockSpec((None, d, tn), lambda i, j: (layer, 0, j)),
                  pl.BlockSpec((None, d, tn), lambda i, j: (layer, 0, j + nj))],
        out_specs=pl.BlockSpec((tm, tn), lambda i, j: (i, j)),
        compiler_params=_params("parallel", "arbitrary"),
        name="gate_up",
    )(h, w_gu, w_gu)


def _residual_kernel(a_ref, w_ref, x_ref, g_ref, o_ref, *, coef, n_lat, tm):
    y = _dot(a_ref[...], w_ref[...])
    row = pl.program_id(0) * tm + lax.broadcasted_iota(jnp.int32, (tm, 1), 0)
    gate = jnp.where(row < n_lat, g_ref[0], g_ref[1])
    o_ref[...] = x_ref[...] + (coef * gate) * y


def _residual_matmul(a, w, layer, x, mod_l, k_gate, coef, n_rows, n_lat, tm):
    kdim = a.shape[1]
    d = w.shape[2]
    tn = _pick(d, (512, 256, 128))
    return pl.pallas_call(
        functools.partial(_residual_kernel, coef=coef, n_lat=n_lat, tm=tm),
        out_shape=jax.ShapeDtypeStruct((n_rows, d), F32),
        grid=(n_rows // tm, d // tn),
        in_specs=[pl.BlockSpec((tm, kdim), lambda i, j: (i, 0)),
                  pl.BlockSpec((None, kdim, tn), lambda i, j: (layer, 0, j)),
                  pl.BlockSpec((tm, tn), lambda i, j: (i, j)),
                  pl.BlockSpec((2, None, 1, tn), lambda i, j: (0, k_gate, 0, j))],
        out_specs=pl.BlockSpec((tm, tn), lambda i, j: (i, j)),
        compiler_params=_params("parallel", "arbitrary"),
        name="residual_matmul",
    )(a, w, x, mod_l)


IN_TILE_BLOCKS = 4
_MAIN_OPS = (("rope_scale",) * A_HEADS + ("rope",) * A_KV_HEADS + ("copy",) * A_KV_HEADS
             + ("norm_q_rope_scale",) * B_HEADS + ("norm_k_rope",) * B_KV_HEADS + ("copy",) * B_KV_HEADS
             + ("scale",) * D_HEADS + ("copy",) * (2 * D_HEADS))


def _head_norm(x, g):
    return x * lax.rsqrt(jnp.mean(x * x, axis=-1, keepdims=True) + NORM_EPS) * g


def _in_proj_kernel(h_ref, w_ref, cos_ref, sin_ref, gq_ref, gk_ref, o_ref, *, row_chunks):
    j = pl.program_id(1)
    scale = HEAD_DIM ** -0.5 * LOG2E
    chunk = h_ref.shape[0] // row_chunks

    def apply(op, x, rows):
        if op in ("norm_q_rope_scale", "norm_k_rope"):
            x = _head_norm(x, (gq_ref if op == "norm_q_rope_scale" else gk_ref)[...])
        if "rope" in op:
            x = _rope(x, cos_ref[rows, :], sin_ref[rows, :], 32)
        return x * scale if "scale" in op else x

    n_tiles = len(_MAIN_OPS) // IN_TILE_BLOCKS
    recipes = [_MAIN_OPS[t * IN_TILE_BLOCKS:(t + 1) * IN_TILE_BLOCKS] for t in range(n_tiles)]
    for recipe in sorted(set(recipes)):
        tiles = [t for t in range(n_tiles) if recipes[t] == recipe]
        hit = functools.reduce(jnp.logical_or, [j == t for t in tiles])

        @pl.when(hit)
        def _(recipe=recipe):
            for r in range(row_chunks):
                rows = slice(r * chunk, (r + 1) * chunk)
                y = _dot_nt(h_ref[rows, :], w_ref[...])
                for b, op in enumerate(recipe):
                    cols = slice(b * LANES, (b + 1) * LANES)
                    o_ref[rows, cols] = apply(op, y[:, cols], rows).astype(o_ref.dtype)


def _in_proj(h, w_main_t, layer, tables, g_q, g_k, tm):
    rows, d = h.shape
    tn = IN_TILE_BLOCKS * LANES
    assert w_main_t.shape[1] == MAIN_COLS and MAIN_COLS % tn == 0
    row_chunks = 4 if tm % 64 == 0 else 1
    return pl.pallas_call(
        functools.partial(_in_proj_kernel, row_chunks=row_chunks),
        out_shape=jax.ShapeDtypeStruct((rows, MAIN_COLS), BF16),
        grid=(rows // tm, MAIN_COLS // tn),
        in_specs=[pl.BlockSpec((tm, d), lambda i, j: (i, 0)),
                  pl.BlockSpec((None, tn, d), lambda i, j: (layer, j, 0)),
                  pl.BlockSpec((tm, LANES), lambda i, j: (i, 0)),
                  pl.BlockSpec((tm, LANES), lambda i, j: (i, 0)),
                  pl.BlockSpec((1, LANES), lambda i, j: (0, 0)),
                  pl.BlockSpec((1, LANES), lambda i, j: (0, 0))],
        out_specs=pl.BlockSpec((tm, tn), lambda i, j: (i, j)),
        compiler_params=_params("parallel", "arbitrary"),
        name="in_proj",
    )(h, w_main_t, tables[0], tables[1], g_q.reshape(1, -1), g_k.reshape(1, -1))


def _mla_down_kernel(h_ref, w_ref, cosc_ref, sinc_ref, gq_ref, gkv_ref, qa_ref, kva_ref, kpe_ref):
    q_rank, kv_rank = qa_ref.shape[1], kva_ref.shape[1]
    y = _dot_nt(h_ref[...], w_ref[...])
    qa_ref[...] = _head_norm(y[:, :q_rank], gq_ref[...]).astype(qa_ref.dtype)
    kva_ref[...] = _head_norm(y[:, q_rank:q_rank + kv_rank], gkv_ref[...]).astype(kva_ref.dtype)
    kpe = y[:, q_rank + kv_rank:q_rank + kv_rank + LANES]
    kpe_ref[...] = _rope(kpe, cosc_ref[...], sinc_ref[...], 16).astype(kpe_ref.dtype)


def _mla_down(h, w_c_t, layer, tables, g_cq, g_ckv, tm):
    rows, d = h.shape
    q_rank, kv_rank = g_cq.shape[0], g_ckv.shape[0]
    cols = w_c_t.shape[1]
    row_spec = lambda w: pl.BlockSpec((tm, w), lambda i: (i, 0))
    vec_spec = lambda w: pl.BlockSpec((1, w), lambda i: (0, 0))
    return pl.pallas_call(
        _mla_down_kernel,
        out_shape=(jax.ShapeDtypeStruct((rows, q_rank), BF16),
                   jax.ShapeDtypeStruct((rows, kv_rank), BF16),
                   jax.ShapeDtypeStruct((rows, LANES), BF16)),
        grid=(rows // tm,),
        in_specs=[row_spec(d), pl.BlockSpec((None, cols, d), lambda i: (layer, 0, 0)),
                  row_spec(LANES), row_spec(LANES), vec_spec(q_rank), vec_spec(kv_rank)],
        out_specs=(row_spec(q_rank), row_spec(kv_rank), row_spec(LANES)),
        compiler_params=_params("parallel"),
        name="mla_down",
    )(h, w_c_t, tables[2], tables[3], g_cq.reshape(1, -1), g_ckv.reshape(1, -1))


def _rope_tables(n_lat, n_ctx):
    t = jnp.arange(n_lat, dtype=jnp.int32)
    row, col = (t // GRID_W).astype(F32), (t % GRID_W).astype(F32)
    lane = jnp.arange(LANES)

    def table(dim, live):
        half = dim // 2
        pair = half // 2
        inv = ROPE_THETA ** (-jnp.arange(0, half, 2, dtype=F32) / half)
        inv_lane = inv[lane % pair]
        pos = jnp.where(((lane // half) % 2 == 0)[None, :], row[:, None], col[:, None])
        ang = pos * inv_lane[None, :]
        sign = jnp.where((lane % half) < pair, -1.0, 1.0).astype(F32)
        on = (lane < live)[None, :]
        cos = jnp.where(on, jnp.cos(ang), 1.0)
        sin = jnp.where(on, jnp.sin(ang) * sign[None, :], 0.0)
        pad = lambda a, v: jnp.concatenate([a, jnp.full((n_ctx, LANES), v, F32)], axis=0)
        return pad(cos, 1.0), pad(sin, 0.0)

    cos, sin = table(HEAD_DIM, LANES)
    cosc, sinc = table(C_ROPE_DIM, C_ROPE_DIM)
    return cos, sin, cosc, sinc


def _cq_up_kernel(a_ref, w_ref, cos_ref, sin_ref, o_ref, *, scale):
    y = _dot(a_ref[...], w_ref[...])
    cos, sin = cos_ref[...], sin_ref[...]
    for h in range(C_HEADS):
        c0 = h * C_QK_PAD
        o_ref[:, c0:c0 + LANES] = (y[:, c0:c0 + LANES] * scale).astype(o_ref.dtype)
        pe = _rope(y[:, c0 + LANES:c0 + 2 * LANES], cos, sin, 16)
        o_ref[:, c0 + LANES:c0 + 2 * LANES] = (pe * scale).astype(o_ref.dtype)


def _ckv_up_kernel(a_ref, w_ref, kpe_ref, k_ref, v_ref):
    y = _dot(a_ref[...], w_ref[...])
    kpe = kpe_ref[...]
    for h in range(C_HEADS):
        k_ref[:, h * C_QK_PAD:h * C_QK_PAD + LANES] = y[:, h * LANES:(h + 1) * LANES].astype(k_ref.dtype)
        k_ref[:, h * C_QK_PAD + LANES:(h + 1) * C_QK_PAD] = kpe
    v_ref[...] = y[:, C_HEADS * LANES:].astype(v_ref.dtype)


def _mla_expand(qa, kva, kpe, w_q, w_kv, layer, cosc, sinc, tm):
    rows, q_rank = qa.shape
    kv_rank = kva.shape[1]
    scale = (C_NOPE_DIM + C_ROPE_DIM) ** -0.5 * LOG2E
    row_spec = lambda w: pl.BlockSpec((tm, w), lambda i: (i, 0))
    layer_spec = lambda w: pl.BlockSpec((None,) + w.shape[1:], lambda i: (layer, 0, 0))
    qc = pl.pallas_call(
        functools.partial(_cq_up_kernel, scale=scale),
        out_shape=jax.ShapeDtypeStruct((rows, C_HEADS * C_QK_PAD), BF16),
        grid=(rows // tm,),
        in_specs=[row_spec(q_rank), layer_spec(w_q), row_spec(LANES), row_spec(LANES)],
        out_specs=row_spec(C_HEADS * C_QK_PAD),
        compiler_params=_params("parallel"),
        name="mla_q_up",
    )(qa, w_q, cosc, sinc)
    kc, vc = pl.pallas_call(
        _ckv_up_kernel,
        out_shape=(jax.ShapeDtypeStruct((rows, C_HEADS * C_QK_PAD), BF16),
                   jax.ShapeDtypeStruct((rows, C_HEADS * C_V_DIM), BF16)),
        grid=(rows // tm,),
        in_specs=[row_spec(kv_rank), layer_spec(w_kv), row_spec(LANES)],
        out_specs=(row_spec(C_HEADS * C_QK_PAD), row_spec(C_HEADS * C_V_DIM)),
        compiler_params=_params("parallel"),
        name="mla_kv_up",
    )(kva, w_kv, kpe)
    return qc, kc, vc


def _flash_kernel(q_ref, k_ref, v_ref, buf_ref, o_ref, m_sc, l_sc, acc_sc, *, group, dk, tc, tk, n_lat, n_ctx):
    del buf_ref
    tq = q_ref.shape[0]
    chains = [(g, r) for g in range(group) for r in range(tq // tc)]

    def chunk(k, v, first):
        n_blocks = k.shape[0] // LANES
        for ci, (g, r) in enumerate(chains):
            s = _dot_nt(q_ref[r * tc:(r + 1) * tc, g * dk:(g + 1) * dk], k)
            blocks = [s[:, b * LANES:(b + 1) * LANES] for b in range(n_blocks)]
            mx = blocks[0]
            for blk in blocks[1:]:
                mx = jnp.maximum(mx, blk)
            m_new = jnp.broadcast_to(jnp.max(mx, axis=-1, keepdims=True), (tc, LANES))
            if not first:
                m_prev = m_sc[ci]
                m_new = jnp.maximum(m_prev, m_new)
                alpha = jnp.exp2(m_prev - m_new)
            ps = [jnp.exp2(blk - m_new) for blk in blocks]
            l_new = ps[0]
            for p in ps[1:]:
                l_new = l_new + p
            pv = _dot(jnp.concatenate([p.astype(BF16) for p in ps], axis=1), v)
            if first:
                l_sc[ci] = l_new
                acc_sc[ci] = pv
            else:
                l_sc[ci] = alpha * l_sc[ci] + l_new
                acc_sc[ci] = alpha * acc_sc[ci] + pv
            m_sc[ci] = m_new

    chunk(k_ref[n_lat:n_lat + n_ctx, :], v_ref[n_lat:n_lat + n_ctx, :], True)

    def step(c, carry):
        start = pl.multiple_of(c * tk, tk)
        chunk(k_ref[pl.ds(start, tk), :], v_ref[pl.ds(start, tk), :], False)
        return carry

    lax.fori_loop(0, n_lat // tk, step, 0)
    for ci, (g, r) in enumerate(chains):
        inv = 1.0 / jnp.sum(l_sc[ci], axis=-1, keepdims=True)
        o_ref[r * tc:(r + 1) * tc, g * LANES:(g + 1) * LANES] = (acc_sc[ci] * inv).astype(o_ref.dtype)


def _flash(q_arr, k_arr, v_arr, o_buf, *, n_kv, group, dk, q_blk0, k_blk0, v_blk0, o_blk0, tq, tc, tk, n_lat,
           n_ctx):
    rows = k_arr.shape[0]
    n_chains = group * (tq // tc)
    stat = pltpu.VMEM((n_chains, tc, LANES), F32)
    return pl.pallas_call(
        functools.partial(_flash_kernel, group=group, dk=dk, tc=tc, tk=tk, n_lat=n_lat, n_ctx=n_ctx),
        out_shape=jax.ShapeDtypeStruct(o_buf.shape, o_buf.dtype),
        grid=(n_kv, n_lat // tq),
        in_specs=[pl.BlockSpec((tq, group * dk), lambda g, i: (i, q_blk0 + g)),
                  pl.BlockSpec((rows, dk), lambda g, i: (0, k_blk0 + g)),
                  pl.BlockSpec((rows, LANES), lambda g, i: (0, v_blk0 + g)),
                  _BUF_SPEC],
        out_specs=pl.BlockSpec((tq, group * LANES), lambda g, i: (i, o_blk0 + g)),
        scratch_shapes=[stat, stat, stat],
        input_output_aliases={3: 0},
        compiler_params=_params("parallel", "arbitrary"),
        name="flash",
    )(q_arr, k_arr, v_arr, o_buf)


def _lane_blocks(s):
    return [s[:, b * LANES:(b + 1) * LANES] for b in range(s.shape[1] // LANES)]


def _softmax_pv(segments, sink=None):
    blocks = [blk for blks, _ in segments for blk in blks]
    rows = blocks[0].shape[0]
    mx = blocks[0]
    for blk in blocks[1:]:
        mx = jnp.maximum(mx, blk)
    m = jnp.broadcast_to(jnp.max(mx, axis=-1, keepdims=True), (rows, LANES))
    if sink is not None:
        m = jnp.maximum(m, sink)
    out = l = None
    for blks, v in segments:
        ps = [jnp.exp2(blk - m) for blk in blks]
        for p in ps:
            l = p if l is None else l + p
        pv = _dot(jnp.concatenate([p.astype(BF16) for p in ps], axis=1), v)
        out = pv if out is None else out + pv
    if sink is not None:
        lane = lax.broadcasted_iota(jnp.int32, (rows, LANES), 1)
        l = l + jnp.where(lane == 0, jnp.exp2(sink - m), 0.0)
    return out * (1.0 / jnp.sum(l, axis=-1, keepdims=True))


def _window_kernel(sink_ref, q_ref, k_ref, v_ref, buf_ref, o_ref, *, group, tq, n_lat, n_ctx, window):
    del buf_ref
    g, n = pl.program_id(0), pl.program_id(1)
    span = tq + 2 * LANES
    kc, vc = k_ref[n_lat:n_lat + n_ctx, :], v_ref[n_lat:n_lat + n_ctx, :]
    row = jnp.bitwise_and(lax.broadcasted_iota(jnp.int32, (group * tq, LANES), 0), tq - 1)
    lane_minus_row = lax.broadcasted_iota(jnp.int32, (group * tq, LANES), 1) - row
    sink = jnp.concatenate([jnp.full((tq, LANES), sink_ref[g * group + i], F32) for i in range(group)], axis=0)
    n_sub = q_ref.shape[0] // tq
    for sub in range(n_sub):
        rows = slice(sub * tq, (sub + 1) * tq)
        q0 = (n * n_sub + sub) * tq
        start = pl.multiple_of(jnp.clip(q0 - LANES, 0, n_lat - span), LANES)
        kw, vw = k_ref[pl.ds(start, span), :], v_ref[pl.ds(start, span), :]
        q = jnp.concatenate([q_ref[rows, i * HEAD_DIM:(i + 1) * HEAD_DIM] for i in range(group)], axis=0)
        s_w = [jnp.where(jnp.abs(lane_minus_row + (start - q0 + b * LANES)) <= window, blk, NEG_INF)
               for b, blk in enumerate(_lane_blocks(_dot_nt(q, kw)))]
        out = _softmax_pv([(s_w, vw), (_lane_blocks(_dot_nt(q, kc)), vc)], sink=sink)
        for i in range(group):
            o_ref[rows, i * HEAD_DIM:(i + 1) * HEAD_DIM] = out[i * tq:(i + 1) * tq].astype(o_ref.dtype)


def _window_attention(main, sink, o_buf, n_lat, n_ctx):
    rows = main.shape[0]
    group = A_HEADS // A_KV_HEADS
    tq = _pick(n_lat, (256, 128))
    assert A_WINDOW <= LANES and n_lat >= tq + 2 * LANES
    n_sub = _pick(n_lat // tq, (2, 1))
    return pl.pallas_call(
        functools.partial(_window_kernel, group=group, tq=tq, n_lat=n_lat, n_ctx=n_ctx, window=A_WINDOW),
        out_shape=jax.ShapeDtypeStruct(o_buf.shape, o_buf.dtype),
        grid=(A_KV_HEADS, n_lat // (n_sub * tq)),
        in_specs=[pl.BlockSpec(memory_space=pltpu.SMEM),
                  pl.BlockSpec((n_sub * tq, group * HEAD_DIM), lambda g, n: (n, g)),
                  pl.BlockSpec((rows, HEAD_DIM), lambda g, n: (0, A_K0 + g)),
                  pl.BlockSpec((rows, HEAD_DIM), lambda g, n: (0, A_V0 + g)),
                  _BUF_SPEC],
        out_specs=pl.BlockSpec((n_sub * tq, group * HEAD_DIM), lambda g, n: (n, MIX_A0 // group + g)),
        input_output_aliases={4: 0},
        compiler_params=_params("parallel", "arbitrary"),
        name="window_attention",
    )(sink, main, main, main, o_buf)


NBR_Q_ROWS = 8
NBR_KEY_ROWS = 16


def _nbr_key_start(rb, n_grid_rows):
    return jnp.clip(rb * NBR_Q_ROWS - NA_KH // 2, 0, n_grid_rows - NBR_KEY_ROWS)


def _nbr_kernel(q_ref, k_ref, v_ref, b_ref, buf_ref, o_ref, *, chain_rows, n_grid_rows, n_lat, n_ctx):
    del buf_ref
    k0 = pl.multiple_of(_nbr_key_start(pl.program_id(1), n_grid_rows) * GRID_W, (NA_KH // 2) * GRID_W)
    n_keys = NBR_KEY_ROWS * GRID_W
    kw, vw = k_ref[pl.ds(k0, n_keys), :], v_ref[pl.ds(k0, n_keys), :]
    kc, vc = k_ref[n_lat:n_lat + n_ctx, :], v_ref[n_lat:n_lat + n_ctx, :]
    rows_per_chain = chain_rows // GRID_W
    for sub in range(q_ref.shape[0] // chain_rows):
        rows = slice(sub * chain_rows, (sub + 1) * chain_rows)
        q = q_ref[rows, :]
        s_nb = [blk + jnp.concatenate([b_ref[sub * rows_per_chain + jj, kp] for jj in range(rows_per_chain)], axis=0)
                for kp, blk in enumerate(_lane_blocks(_dot_nt(q, kw)))]
        out = _softmax_pv([(s_nb, vw), (_lane_blocks(_dot_nt(q, kc)), vc)])
        o_ref[rows, :] = out.astype(o_ref.dtype)


def _nbr_bias_table(rel_bias):
    assert 2 * GRID_W == LANES
    n_heads = rel_bias.shape[0]
    hp = lax.Precision.HIGHEST
    off = jnp.array([0, NA_KH // 2, NA_KH], jnp.int32)[:, None, None]
    j = jnp.arange(NBR_Q_ROWS)[None, :, None]
    kr = jnp.arange(NBR_KEY_ROWS)[None, None, :]
    centred = j - NA_KH // 2
    rs = jnp.stack([jnp.maximum(centred[0], 0), centred[0], jnp.minimum(centred[0], 0)])
    key_row = kr - off
    row_ok = (key_row >= rs) & (key_row < rs + NA_KH)
    row_sel = jax.nn.one_hot(key_row - j + (NA_KH - 1), 2 * NA_KH - 1, dtype=F32) * row_ok[..., None]
    feat = jnp.einsum("vjka,hab->vhjkb", row_sel, rel_bias.astype(F32) * LOG2E, precision=hp)
    bad = jnp.broadcast_to((~row_ok).astype(F32)[:, None, :, :, None], feat.shape[:-1] + (1,))
    feat = jnp.concatenate([feat, bad], axis=-1)
    feat = feat.reshape(feat.shape[:3] + (NBR_KEY_ROWS // 2, 2 * feat.shape[-1]))
    feat = jnp.concatenate([feat, jnp.ones(feat.shape[:-1] + (1,), F32)], axis=-1)
    c = jnp.arange(GRID_W)[:, None]
    kc = jnp.arange(GRID_W)[None, :]
    cstart = jnp.clip(c - NA_KW // 2, 0, GRID_W - NA_KW)
    col_ok = (kc >= cstart) & (kc < cstart + NA_KW)
    col_sel = jax.nn.one_hot(kc - c + (NA_KW - 1), 2 * NA_KW - 1, dtype=F32)
    per_row = jnp.concatenate([col_sel.transpose(0, 2, 1), jnp.full((GRID_W, 1, GRID_W), NEG_INF, F32)], axis=1)
    sel = jnp.einsum("pq,cxk->cpxqk", jnp.eye(2, dtype=F32), per_row).reshape(GRID_W, 4 * NA_KW, LANES)
    col_bad = jnp.tile(jnp.where(col_ok, 0.0, NEG_INF).astype(F32), (1, 2))[:, None, :]
    sel = jnp.concatenate([sel, col_bad], axis=1)
    return jnp.einsum("vhjkx,cxl->vhjkcl", feat, sel, precision=hp)


def _nbr_attention(main, rel_bias, o_buf, n_lat, n_ctx):
    rows = main.shape[0]
    n_grid_rows = n_lat // GRID_W
    assert n_grid_rows % NBR_Q_ROWS == 0 and n_grid_rows >= NBR_KEY_ROWS
    assert NBR_KEY_ROWS >= NBR_Q_ROWS + NA_KH and NA_KH // 2 * 2 == NA_KH
    n_blocks = n_grid_rows // NBR_Q_ROWS
    tq = NBR_Q_ROWS * GRID_W

    def variant(rb):
        return (rb * NBR_Q_ROWS - _nbr_key_start(rb, n_grid_rows)) // (NA_KH // 2)

    return pl.pallas_call(
        functools.partial(_nbr_kernel, chain_rows=tq // 2, n_grid_rows=n_grid_rows, n_lat=n_lat, n_ctx=n_ctx),
        out_shape=jax.ShapeDtypeStruct(o_buf.shape, o_buf.dtype),
        grid=(D_HEADS, n_blocks),
        in_specs=[pl.BlockSpec((tq, HEAD_DIM), lambda h, rb: (rb, D_Q0 + h)),
                  pl.BlockSpec((rows, HEAD_DIM), lambda h, rb: (0, D_K0 + h)),
                  pl.BlockSpec((rows, HEAD_DIM), lambda h, rb: (0, D_V0 + h)),
                  pl.BlockSpec((None, None, NBR_Q_ROWS, NBR_KEY_ROWS // 2, GRID_W, LANES),
                               lambda h, rb: (variant(rb), h, 0, 0, 0, 0)),
                  _BUF_SPEC],
        out_specs=pl.BlockSpec((tq, HEAD_DIM), lambda h, rb: (rb, MIX_D0 + h)),
        input_output_aliases={4: 0},
        compiler_params=_params("parallel", "arbitrary"),
        name="nbr_attention",
    )(main, main, main, _nbr_bias_table(rel_bias), o_buf)


def _ctx_attn_kernel(sink_ref, q_ref, k_ref, v_ref, buf_ref, o_ref):
    del buf_ref
    sink = sink_ref[pl.program_id(0)]
    s = _dot_nt(q_ref[...], k_ref[...])
    m = jnp.maximum(jnp.max(s, axis=-1, keepdims=True), sink)
    p = jnp.exp2(s - m)
    denom = jnp.sum(p, axis=-1, keepdims=True) + jnp.exp2(sink - m)
    o_ref[...] = (_dot(p.astype(BF16), v_ref[...]) * (1.0 / denom)).astype(o_ref.dtype)


def _ctx_attention(q_arr, k_arr, v_arr, sink, o_buf, *, n_heads, group, dk, q_blk0, k_blk0, v_blk0, o_blk0, n_lat,
                   n_ctx):
    rb = n_lat // n_ctx
    return pl.pallas_call(
        _ctx_attn_kernel,
        out_shape=jax.ShapeDtypeStruct(o_buf.shape, o_buf.dtype),
        grid=(n_heads,),
        in_specs=[pl.BlockSpec(memory_space=pltpu.SMEM),
                  pl.BlockSpec((n_ctx, dk), lambda h: (rb, q_blk0 + h)),
                  pl.BlockSpec((n_ctx, dk), lambda h: (rb, k_blk0 + h // group)),
                  pl.BlockSpec((n_ctx, LANES), lambda h: (rb, v_blk0 + h // group)),
                  _BUF_SPEC],
        out_specs=pl.BlockSpec((n_ctx, LANES), lambda h: (rb, o_blk0 + h)),
        input_output_aliases={4: 0},
        compiler_params=_params("parallel"),
        name="ctx_attention",
    )(sink, q_arr, k_arr, v_arr, o_buf)


def _relayout_w_in(w, q_rank, kv_rank):
    cols = w.shape[-1]
    a_cols = (A_HEADS + 2 * A_KV_HEADS) * HEAD_DIM
    b_cols = (B_HEADS + 2 * B_KV_HEADS) * HEAD_DIM
    c_cols = q_rank + kv_rank + C_ROPE_DIM
    o2, o3 = a_cols + b_cols, a_cols + b_cols + c_cols
    assert cols - c_cols == MAIN_COLS
    wt = jnp.swapaxes(w, -1, -2)
    w_main_t = jnp.concatenate([wt[..., :o2, :], wt[..., o3:, :]], axis=-2).astype(BF16)
    c_pad = q_rank + kv_rank + LANES - c_cols
    w_c_t = jnp.pad(wt[..., o2:o3, :], ((0, 0),) * (w.ndim - 2) + ((0, c_pad), (0, 0))).astype(BF16)
    return w_main_t, w_c_t


def _relayout_w_q_up(w):
    lead = w.shape[:-1]
    w = w.astype(BF16).reshape(lead + (C_HEADS, C_NOPE_DIM + C_ROPE_DIM))
    w = jnp.pad(w, ((0, 0),) * (len(lead) + 1) + ((0, C_QK_PAD - C_NOPE_DIM - C_ROPE_DIM),))
    return w.reshape(lead + (C_HEADS * C_QK_PAD,))


def _relayout_w_kv_up(w):
    lead = w.shape[:-1]
    w = w.astype(BF16).reshape(lead + (C_HEADS, C_NOPE_DIM + C_V_DIM))
    return jnp.concatenate([w[..., :C_NOPE_DIM].reshape(lead + (-1,)), w[..., C_NOPE_DIM:].reshape(lead + (-1,))],
                           axis=-1)


def kernel(x, c, ctx, c_ctx, w_mod_down, w_mod_up, norm_ffn1, ffn1_w_gu, ffn1_w_down, norm_mix, w_in,
           a_sink, b_q_norm, b_k_norm, c_q_norm, c_kv_norm, c_w_q_up, c_w_kv_up, d_rel_bias, w_out,
           norm_ffn2, ffn2_w_gu, ffn2_w_down, final_norm):
    bsz, n_lat, d = x.shape
    n_ctx = ctx.shape[1]
    n_layers = w_in.shape[0]
    q_rank, kv_rank = c_q_norm.shape[1], c_kv_norm.shape[1]
    assert bsz == 1 and n_lat % GRID_W == 0 and n_lat % n_ctx == 0 and n_ctx % LANES == 0
    rows = n_lat + n_ctx
    tr = n_ctx
    tm_all = (_pick(rows, (1408, 768, 512, 256, 128)), _pick(rows, (768, 512, 256, 128)))
    tm_lat = (_pick(n_lat, (1024, 512, 256, 128)),) * 2
    flash_tc = _pick(n_lat, (1024, 512, 256, 128))
    flash_tk = _pick(n_lat, (2048, 1024, 512, 256, 128))

    mod = _modulation(c, c_ctx, w_mod_down, w_mod_up)
    tables = _rope_tables(n_lat, n_ctx)
    no_sink = jnp.full((max(B_HEADS, C_HEADS, D_HEADS),), NEG_INF, F32)

    w_in_main, w_in_c = _relayout_w_in(w_in, q_rank, kv_rank)
    w_q_up_b, w_kv_up_b = _relayout_w_q_up(c_w_q_up), _relayout_w_kv_up(c_w_kv_up)
    w_dn1, w_dn2, w_out_b = ffn1_w_down.astype(BF16), ffn2_w_down.astype(BF16), w_out.astype(BF16)

    xs = jnp.concatenate([x[0], ctx[0]], axis=0)

    def ffn(xs, gain, w_gu, w_down, l, mod_l, k0, n_rows, tm):
        h = _norm_mod(xs, gain, mod_l, k0, k0 + 1, n_lat, tr)
        act = _gate_up(h, w_gu, l, n_rows, tm[0])
        return _residual_matmul(act, w_down, l, xs, mod_l, k0 + 2, 0.5, n_rows, n_lat, tm[1])

    for l in range(n_layers):
        need_ctx = l < n_layers - 1
        n_rows, tm = (rows, tm_all) if need_ctx else (n_lat, tm_lat)
        mod_l = mod[l]
        xs = ffn(xs, norm_ffn1[l], ffn1_w_gu, w_dn1, l, mod_l, 0, rows, tm_all)

        h = _norm_mod(xs, norm_mix[l], mod_l, 3, 4, n_lat, tr)
        main = _in_proj(h, w_in_main, l, tables, b_q_norm[l], b_k_norm[l], tm_all[0])
        qa, kva, kpe = _mla_down(h, w_in_c, l, tables, c_q_norm[l], c_kv_norm[l], tm_all[1])
        qc, kc, vc = _mla_expand(qa, kva, kpe, w_q_up_b, w_kv_up_b, l, tables[2], tables[3], tr)

        sink = a_sink[l] * LOG2E
        lat = dict(n_lat=n_lat, n_ctx=n_ctx)
        o = jnp.zeros((n_rows, MIX_COLS), BF16)
        o = _window_attention(main, sink, o, **lat)
        o = _flash(main, main, main, o, n_kv=B_KV_HEADS, group=B_HEADS // B_KV_HEADS, dk=HEAD_DIM,
                   q_blk0=B_Q0 // (B_HEADS // B_KV_HEADS), k_blk0=B_K0, v_blk0=B_V0,
                   o_blk0=MIX_B0 // (B_HEADS // B_KV_HEADS), tq=flash_tc, tc=flash_tc, tk=flash_tk, **lat)
        o = _flash(qc, kc, vc, o, n_kv=C_HEADS, group=1, dk=C_QK_PAD, q_blk0=0, k_blk0=0, v_blk0=0, o_blk0=MIX_C0,
                   tq=_pick(n_lat, (4 * flash_tc, 2 * flash_tc, flash_tc)), tc=flash_tc, tk=flash_tk, **lat)
        o = _nbr_attention(main, d_rel_bias[l], o, **lat)
        if need_ctx:
            o = _ctx_attention(main, main, main, sink, o, n_heads=A_HEADS, group=A_HEADS // A_KV_HEADS,
                               dk=HEAD_DIM, q_blk0=A_Q0, k_blk0=A_K0, v_blk0=A_V0, o_blk0=MIX_A0, **lat)
            o = _ctx_attention(main, main, main, no_sink, o, n_heads=B_HEADS, group=B_HEADS // B_KV_HEADS,
                               dk=HEAD_DIM, q_blk0=B_Q0, k_blk0=B_K0, v_blk0=B_V0, o_blk0=MIX_B0, **lat)
            o = _ctx_attention(qc, kc, vc, no_sink, o, n_heads=C_HEADS, group=1, dk=C_QK_PAD,
                               q_blk0=0, k_blk0=0, v_blk0=0, o_blk0=MIX_C0, **lat)
            o = _ctx_attention(main, main, main, no_sink, o, n_heads=D_HEADS, group=1, dk=HEAD_DIM,
                               q_blk0=D_Q0, k_blk0=D_K0, v_blk0=D_V0, o_blk0=MIX_D0, **lat)

        xs = _residual_matmul(o, w_out_b, l, xs, mod_l, 5, 1.0, n_rows, n_lat, tm[1])
        xs = ffn(xs, norm_ffn2[l], ffn2_w_gu, w_dn2, l, mod_l, 6, n_rows, tm)

    return _final_norm(xs[:n_lat], final_norm, tr)[None]
```

```python
import functools

import jax
import jax.numpy as jnp
from jax import lax
from jax.experimental import pallas as pl
from jax.experimental.pallas import tpu as pltpu

F32 = jnp.float32
BF16 = jnp.bfloat16

GRID_W = 64
HEAD_DIM = 128
ROPE_THETA = 10000.0
NORM_EPS = 1e-6
NEG_INF = -1e30
LOG2E = 1.4426950408889634
N_MOD = 9
A_HEADS, A_KV_HEADS, A_WINDOW = 8, 2, 128
B_HEADS, B_KV_HEADS = 8, 2
C_HEADS, C_NOPE_DIM, C_ROPE_DIM, C_V_DIM = 8, 128, 64, 128
D_HEADS, NA_KH, NA_KW = 8, 8, 16

LANES = 128
VMEM_LIMIT_BYTES = 56 * 1024 * 1024
C_QK_PAD = 2 * LANES

A_Q0, A_K0, A_V0 = 0, 8, 10
B_Q0, B_K0, B_V0 = 12, 20, 22
D_Q0, D_K0, D_V0 = 24, 32, 40
MAIN_COLS = 48 * LANES
MIX_A0, MIX_B0, MIX_C0, MIX_D0 = 0, 8, 16, 24
MIX_COLS = 32 * LANES
_BUF_SPEC = pl.BlockSpec(memory_space=pl.ANY)


def _params(*sem):
    return pltpu.CompilerParams(dimension_semantics=sem, vmem_limit_bytes=VMEM_LIMIT_BYTES)


def _pick(n, prefs):
    for p in prefs:
        if n % p == 0:
            return p
    raise ValueError(f"no tile in {prefs} divides {n}")


def _dot(a, b):
    return jnp.dot(a, b, preferred_element_type=F32)


def _dot_nt(a, b):
    return lax.dot_general(a, b, (((1,), (1,)), ((), ())), preferred_element_type=F32)


def _silu(x):
    return x / (1.0 + jnp.exp(-x))


def _rope(x, cos, sin_signed, half):
    n = x.shape[-1]
    lane = lax.broadcasted_iota(jnp.int32, x.shape, x.ndim - 1)
    first = jnp.bitwise_and(lane, 2 * half - 1) < half
    rot = jnp.where(first, pltpu.roll(x, n - half, x.ndim - 1), pltpu.roll(x, half, x.ndim - 1))
    return x * cos + rot * sin_signed


def _mod_down_kernel(s_ref, w_ref, o_ref, acc_ref):
    k = pl.program_id(1)

    @pl.when(k == 0)
    def _():
        acc_ref[...] = jnp.zeros_like(acc_ref)

    acc_ref[...] += _dot(_silu(s_ref[...]).astype(BF16), w_ref[...].astype(BF16))

    @pl.when(k == pl.num_programs(1) - 1)
    def _():
        o_ref[...] = acc_ref[...]


def _mod_up_kernel(t_ref, w_ref, o_ref):
    o_ref[...] = _dot(t_ref[...].astype(BF16), w_ref[...].astype(BF16))


def _modulation(c, c_ctx, w_down, w_up):
    n_layers, d, rank = w_down.shape
    n_out = w_up.shape[2]
    s = jnp.zeros((8, d), F32).at[0].set(c[0]).at[1].set(c_ctx)
    tk = _pick(d, (1024, 512, 256, 128))
    t = pl.pallas_call(
        _mod_down_kernel,
        out_shape=jax.ShapeDtypeStruct((n_layers, 8, rank), F32),
        grid=(n_layers, d // tk),
        in_specs=[pl.BlockSpec((8, tk), lambda l, k: (0, k)),
                  pl.BlockSpec((None, tk, rank), lambda l, k: (l, k, 0))],
        out_specs=pl.BlockSpec((None, 8, rank), lambda l, k: (l, 0, 0)),
        scratch_shapes=[pltpu.VMEM((8, rank), F32)],
        compiler_params=_params("parallel", "arbitrary"),
        name="mod_down",
    )(s, w_down)
    tn = _pick(n_out, (2048, 1024, 512, 256, 128))
    m = pl.pallas_call(
        _mod_up_kernel,
        out_shape=jax.ShapeDtypeStruct((n_layers, 8, n_out), F32),
        grid=(n_layers, n_out // tn),
        in_specs=[pl.BlockSpec((None, 8, rank), lambda l, j: (l, 0, 0)),
                  pl.BlockSpec((None, rank, tn), lambda l, j: (l, 0, j))],
        out_specs=pl.BlockSpec((None, 8, tn), lambda l, j: (l, 0, j)),
        compiler_params=_params("parallel", "parallel"),
        name="mod_up",
    )(t, w_up)
    return m[:, :2, :].reshape(n_layers, 2, N_MOD, 1, d)


def _norm_mod_kernel(x_ref, gain_ref, shift_ref, scale_ref, o_ref):
    x = x_ref[...]
    y = x * lax.rsqrt(jnp.mean(x * x, axis=-1, keepdims=True) + NORM_EPS) * gain_ref[...]
    o_ref[...] = (y * (1.0 + scale_ref[...]) + shift_ref[...]).astype(o_ref.dtype)


def _norm_mod(x, gain, mod_l, k_shift, k_scale, n_lat, tr):
    rows, d = x.shape
    n_lat_blocks = n_lat // tr

    def mod_spec(k):
        return pl.BlockSpec((None, None, 1, d),
                            lambda i: (jnp.where(i >= n_lat_blocks, 1, 0), k, 0, 0))

    return pl.pallas_call(
        _norm_mod_kernel,
        out_shape=jax.ShapeDtypeStruct((rows, d), BF16),
        grid=(rows // tr,),
        in_specs=[pl.BlockSpec((tr, d), lambda i: (i, 0)),
                  pl.BlockSpec((1, d), lambda i: (0, 0)),
                  mod_spec(k_shift), mod_spec(k_scale)],
        out_specs=pl.BlockSpec((tr, d), lambda i: (i, 0)),
        compiler_params=_params("parallel"),
        name="norm_mod",
    )(x, gain.reshape(1, d), mod_l, mod_l)


def _final_norm_kernel(x_ref, gain_ref, o_ref):
    x = x_ref[...]
    o_ref[...] = x * lax.rsqrt(jnp.mean(x * x, axis=-1, keepdims=True) + NORM_EPS) * gain_ref[...]


def _final_norm(x, gain, tr):
    rows, d = x.shape
    return pl.pallas_call(
        _final_norm_kernel,
        out_shape=jax.ShapeDtypeStruct((rows, d), F32),
        grid=(rows // tr,),
        in_specs=[pl.BlockSpec((tr, d), lambda i: (i, 0)),
                  pl.BlockSpec((1, d), lambda i: (0, 0))],
        out_specs=pl.BlockSpec((tr, d), lambda i: (i, 0)),
        compiler_params=_params("parallel"),
        name="final_norm",
    )(x, gain.reshape(1, d))


def _gate_up_kernel(h_ref, wg_ref, wu_ref, o_ref):
    h = h_ref[...]
    g = _dot(h, wg_ref[...].astype(BF16))
    u = _dot(h, wu_ref[...].astype(BF16))
    o_ref[...] = (_silu(g) * u).astype(o_ref.dtype)


def _gate_up(h, w_gu, layer, n_rows, tm):
    d = h.shape[1]
    f = w_gu.shape[2] // 2
    tn = _pick(f, (256, 128))
    nj = f // tn
    return pl.pallas_call(
        _gate_up_kernel,
        out_shape=jax.ShapeDtypeStruct((n_rows, f), BF16),
        grid=(n_rows // tm, nj),
        in_specs=[pl.BlockSpec((tm, d), lambda i, j: (i, 0)),
                  pl.BlockSpec((None, d, tn), lambda i, j: (layer, 0, j)),
                  pl.BlockSpec((None, d, tn), lambda i, j: (layer, 0, j + nj))],
        out_specs=pl.BlockSpec((tm, tn), lambda i, j: (i, j)),
        compiler_params=_params("parallel", "arbitrary"),
        name="gate_up",
    )(h, w_gu, w_gu)


def _residual_kernel(a_ref, w_ref, x_ref, g_ref, o_ref, *, coef, n_lat, tm):
    y = _dot(a_ref[...], w_ref[...])
    row = pl.program_id(0) * tm + lax.broadcasted_iota(jnp.int32, (tm, 1), 0)
    gate = jnp.where(row < n_lat, g_ref[0], g_ref[1])
    o_ref[...] = x_ref[...] + (coef * gate) * y


def _residual_matmul(a, w, layer, x, mod_l, k_gate, coef, n_rows, n_lat, tm):
    kdim = a.shape[1]
    d = w.shape[2]
    tn = _pick(d, (512, 256, 128))
    return pl.pallas_call(
        functools.partial(_residual_kernel, coef=coef, n_lat=n_lat, tm=tm),
        out_shape=jax.ShapeDtypeStruct((n_rows, d), F32),
        grid=(n_rows // tm, d // tn),
        in_specs=[pl.BlockSpec((tm, kdim), lambda i, j: (i, 0)),
                  pl.BlockSpec((None, kdim, tn), lambda i, j: (layer, 0, j)),
                  pl.BlockSpec((tm, tn), lambda i, j: (i, j)),
                  pl.BlockSpec((2, None, 1, tn), lambda i, j: (0, k_gate, 0, j))],
        out_specs=pl.BlockSpec((tm, tn), lambda i, j: (i, j)),
        compiler_params=_params("parallel", "arbitrary"),
        name="residual_matmul",
    )(a, w, x, mod_l)


IN_TILE_BLOCKS = 4
_MAIN_OPS = (("rope_scale",) * A_HEADS + ("rope",) * A_KV_HEADS + ("copy",) * A_KV_HEADS
             + ("norm_q_rope_scale",) * B_HEADS + ("norm_k_rope",) * B_KV_HEADS + ("copy",) * B_KV_HEADS
             + ("scale",) * D_HEADS + ("copy",) * (2 * D_HEADS))


def _head_norm(x, g):
    return x * lax.rsqrt(jnp.mean(x * x, axis=-1, keepdims=True) + NORM_EPS) * g


def _in_proj_kernel(h_ref, w_ref, cos_ref, sin_ref, gq_ref, gk_ref, o_ref, *, row_chunks):
    j = pl.program_id(1)
    scale = HEAD_DIM ** -0.5 * LOG2E
    chunk = h_ref.shape[0] // row_chunks

    def apply(op, x, rows):
        if op in ("norm_q_rope_scale", "norm_k_rope"):
            x = _head_norm(x, (gq_ref if op == "norm_q_rope_scale" else gk_ref)[...])
        if "rope" in op:
            x = _rope(x, cos_ref[rows, :], sin_ref[rows, :], 32)
        return x * scale if "scale" in op else x

    n_tiles = len(_MAIN_OPS) // IN_TILE_BLOCKS
    recipes = [_MAIN_OPS[t * IN_TILE_BLOCKS:(t + 1) * IN_TILE_BLOCKS] for t in range(n_tiles)]
    for recipe in sorted(set(recipes)):
        tiles = [t for t in range(n_tiles) if recipes[t] == recipe]
        hit = functools.reduce(jnp.logical_or, [j == t for t in tiles])

        @pl.when(hit)
        def _(recipe=recipe):
            for r in range(row_chunks):
                rows = slice(r * chunk, (r + 1) * chunk)
                y = _dot_nt(h_ref[rows, :], w_ref[0])
                for b, op in enumerate(recipe):
                    cols = slice(b * LANES, (b + 1) * LANES)
                    o_ref[rows, cols] = apply(op, y[:, cols], rows).astype(o_ref.dtype)


def _in_proj(h, w_t, layer, tables, g_q, g_k, tm, mla_rows):
    rows, d = h.shape
    tn = IN_TILE_BLOCKS * LANES
    o2, o3 = mla_rows
    assert w_t.shape[1] - (o3 - o2) == MAIN_COLS and MAIN_COLS % tn == 0 and o2 % tn == 0
    row_chunks = 4 if tm % 64 == 0 else 1

    assert (o3 - o2) % C_ROPE_DIM == 0

    def w_row(j):
        return pl.multiple_of(jnp.where(j < o2 // tn, j * tn, o3 - o2 + j * tn), C_ROPE_DIM)

    return pl.pallas_call(
        functools.partial(_in_proj_kernel, row_chunks=row_chunks),
        out_shape=jax.ShapeDtypeStruct((rows, MAIN_COLS), BF16),
        grid=(rows // tm, MAIN_COLS // tn),
        in_specs=[pl.BlockSpec((tm, d), lambda i, j: (i, 0)),
                  pl.BlockSpec((pl.Element(1), pl.Element(tn), pl.Element(d)),
                               lambda i, j: (layer, w_row(j), 0)),
                  pl.BlockSpec((tm, LANES), lambda i, j: (i, 0)),
                  pl.BlockSpec((tm, LANES), lambda i, j: (i, 0)),
                  pl.BlockSpec((1, LANES), lambda i, j: (0, 0)),
                  pl.BlockSpec((1, LANES), lambda i, j: (0, 0))],
        out_specs=pl.BlockSpec((tm, tn), lambda i, j: (i, j)),
        compiler_params=_params("parallel", "arbitrary"),
        name="in_proj",
    )(h, w_t, tables[0], tables[1], g_q.reshape(1, -1), g_k.reshape(1, -1))


def _mla_down_kernel(h_ref, w_ref, cosc_ref, sinc_ref, gq_ref, gkv_ref, qa_ref, kva_ref, kpe_ref):
    q_rank, kv_rank = qa_ref.shape[1], kva_ref.shape[1]
    y = _dot_nt(h_ref[...], w_ref[0])
    qa_ref[...] = _head_norm(y[:, :q_rank], gq_ref[...]).astype(qa_ref.dtype)
    kva_ref[...] = _head_norm(y[:, q_rank:q_rank + kv_rank], gkv_ref[...]).astype(kva_ref.dtype)
    kpe = y[:, q_rank + kv_rank:q_rank + kv_rank + LANES]
    lane = lax.broadcasted_iota(jnp.int32, kpe.shape, 1)
    kpe = jnp.where(lane < C_ROPE_DIM, _rope(kpe, cosc_ref[...], sinc_ref[...], 16), 0.0)
    kpe_ref[...] = kpe.astype(kpe_ref.dtype)


def _mla_down(h, w_t, layer, tables, g_cq, g_ckv, tm, mla_rows):
    rows, d = h.shape
    q_rank, kv_rank = g_cq.shape[0], g_ckv.shape[0]
    cols = q_rank + kv_rank + LANES
    assert mla_rows[1] - mla_rows[0] == q_rank + kv_rank + C_ROPE_DIM and mla_rows[0] + cols <= w_t.shape[1]
    row_spec = lambda w: pl.BlockSpec((tm, w), lambda i: (i, 0))
    vec_spec = lambda w: pl.BlockSpec((1, w), lambda i: (0, 0))
    return pl.pallas_call(
        _mla_down_kernel,
        out_shape=(jax.ShapeDtypeStruct((rows, q_rank), BF16),
                   jax.ShapeDtypeStruct((rows, kv_rank), BF16),
                   jax.ShapeDtypeStruct((rows, LANES), BF16)),
        grid=(rows // tm,),
        in_specs=[row_spec(d),
                  pl.BlockSpec((pl.Element(1), pl.Element(cols), pl.Element(d)), lambda i: (layer, mla_rows[0], 0)),
                  row_spec(LANES), row_spec(LANES), vec_spec(q_rank), vec_spec(kv_rank)],
        out_specs=(row_spec(q_rank), row_spec(kv_rank), row_spec(LANES)),
        compiler_params=_params("parallel"),
        name="mla_down",
    )(h, w_t, tables[2], tables[3], g_cq.reshape(1, -1), g_ckv.reshape(1, -1))


def _rope_tables(n_lat, n_ctx):
    t = jnp.arange(n_lat, dtype=jnp.int32)
    row, col = (t // GRID_W).astype(F32), (t % GRID_W).astype(F32)
    lane = jnp.arange(LANES)

    def table(dim, live):
        half = dim // 2
        pair = half // 2
        inv = ROPE_THETA ** (-jnp.arange(0, half, 2, dtype=F32) / half)
        inv_lane = inv[lane % pair]
        pos = jnp.where(((lane // half) % 2 == 0)[None, :], row[:, None], col[:, None])
        ang = pos * inv_lane[None, :]
        sign = jnp.where((lane % half) < pair, -1.0, 1.0).astype(F32)
        on = (lane < live)[None, :]
        cos = jnp.where(on, jnp.cos(ang), 1.0)
        sin = jnp.where(on, jnp.sin(ang) * sign[None, :], 0.0)
        pad = lambda a, v: jnp.concatenate([a, jnp.full((n_ctx, LANES), v, F32)], axis=0)
        return pad(cos, 1.0), pad(sin, 0.0)

    cos, sin = table(HEAD_DIM, LANES)
    cosc, sinc = table(C_ROPE_DIM, C_ROPE_DIM)
    return cos, sin, cosc, sinc


def _cq_up_kernel(a_ref, w_ref, cos_ref, sin_ref, o_ref, *, scale):
    y = _dot(a_ref[...], w_ref[...])
    cos, sin = cos_ref[...], sin_ref[...]
    for h in range(C_HEADS):
        c0 = h * C_QK_PAD
        o_ref[:, c0:c0 + LANES] = (y[:, c0:c0 + LANES] * scale).astype(o_ref.dtype)
        pe = _rope(y[:, c0 + LANES:c0 + 2 * LANES], cos, sin, 16)
        o_ref[:, c0 + LANES:c0 + 2 * LANES] = (pe * scale).astype(o_ref.dtype)


def _ckv_up_kernel(a_ref, w_ref, kpe_ref, k_ref, v_ref):
    y = _dot(a_ref[...], w_ref[...])
    kpe = kpe_ref[...]
    for h in range(C_HEADS):
        k_ref[:, h * C_QK_PAD:h * C_QK_PAD + LANES] = y[:, h * LANES:(h + 1) * LANES].astype(k_ref.dtype)
        k_ref[:, h * C_QK_PAD + LANES:(h + 1) * C_QK_PAD] = kpe
    v_ref[...] = y[:, C_HEADS * LANES:].astype(v_ref.dtype)


def _mla_expand(qa, kva, kpe, w_q, w_kv, layer, cosc, sinc, tm):
    rows, q_rank = qa.shape
    kv_rank = kva.shape[1]
    scale = (C_NOPE_DIM + C_ROPE_DIM) ** -0.5 * LOG2E
    row_spec = lambda w: pl.BlockSpec((tm, w), lambda i: (i, 0))
    layer_spec = lambda w: pl.BlockSpec((None,) + w.shape[1:], lambda i: (layer, 0, 0))
    qc = pl.pallas_call(
        functools.partial(_cq_up_kernel, scale=scale),
        out_shape=jax.ShapeDtypeStruct((rows, C_HEADS * C_QK_PAD), BF16),
        grid=(rows // tm,),
        in_specs=[row_spec(q_rank), layer_spec(w_q), row_spec(LANES), row_spec(LANES)],
        out_specs=row_spec(C_HEADS * C_QK_PAD),
        compiler_params=_params("parallel"),
        name="mla_q_up",
    )(qa, w_q, cosc, sinc)
    kc, vc = pl.pallas_call(
        _ckv_up_kernel,
        out_shape=(jax.ShapeDtypeStruct((rows, C_HEADS * C_QK_PAD), BF16),
                   jax.ShapeDtypeStruct((rows, C_HEADS * C_V_DIM), BF16)),
        grid=(rows // tm,),
        in_specs=[row_spec(kv_rank), layer_spec(w_kv), row_spec(LANES)],
        out_specs=(row_spec(C_HEADS * C_QK_PAD), row_spec(C_HEADS * C_V_DIM)),
        compiler_params=_params("parallel"),
        name="mla_kv_up",
    )(kva, w_kv, kpe)
    return qc, kc, vc


def _flash_kernel(q_ref, k_ref, v_ref, buf_ref, o_ref, m_sc, l_sc, acc_sc, *, group, dk, tc, tk, n_lat, n_ctx):
    del buf_ref
    tq = q_ref.shape[0]
    chains = [(g, r) for g in range(group) for r in range(tq // tc)]

    def chunk(k, v, first):
        n_blocks = k.shape[0] // LANES
        for ci, (g, r) in enumerate(chains):
            s = _dot_nt(q_ref[r * tc:(r + 1) * tc, g * dk:(g + 1) * dk], k)
            blocks = [s[:, b * LANES:(b + 1) * LANES] for b in range(n_blocks)]
            mx = blocks[0]
            for blk in blocks[1:]:
                mx = jnp.maximum(mx, blk)
            m_new = jnp.broadcast_to(jnp.max(mx, axis=-1, keepdims=True), (tc, LANES))
            if not first:
                m_prev = m_sc[ci]
                m_new = jnp.maximum(m_prev, m_new)
                alpha = jnp.exp2(m_prev - m_new)
            ps = [jnp.exp2(blk - m_new) for blk in blocks]
            l_new = ps[0]
            for p in ps[1:]:
                l_new = l_new + p
            pv = _dot(jnp.concatenate([p.astype(BF16) for p in ps], axis=1), v)
            if first:
                l_sc[ci] = l_new
                acc_sc[ci] = pv
            else:
                l_sc[ci] = alpha * l_sc[ci] + l_new
                acc_sc[ci] = alpha * acc_sc[ci] + pv
            m_sc[ci] = m_new

    chunk(k_ref[n_lat:n_lat + n_ctx, :], v_ref[n_lat:n_lat + n_ctx, :], True)

    def step(c, carry):
        start = pl.multiple_of(c * tk, tk)
        chunk(k_ref[pl.ds(start, tk), :], v_ref[pl.ds(start, tk), :], False)
        return carry

    lax.fori_loop(0, n_lat // tk, step, 0)
    for ci, (g, r) in enumerate(chains):
        inv = 1.0 / jnp.sum(l_sc[ci], axis=-1, keepdims=True)
        o_ref[r * tc:(r + 1) * tc, g * LANES:(g + 1) * LANES] = (acc_sc[ci] * inv).astype(o_ref.dtype)


def _flash(q_arr, k_arr, v_arr, o_buf, *, n_kv, group, dk, q_blk0, k_blk0, v_blk0, o_blk0, tq, tc, tk, n_lat,
           n_ctx):
    rows = k_arr.shape[0]
    n_chains = group * (tq // tc)
    stat = pltpu.VMEM((n_chains, tc, LANES), F32)
    return pl.pallas_call(
        functools.partial(_flash_kernel, group=group, dk=dk, tc=tc, tk=tk, n_lat=n_lat, n_ctx=n_ctx),
        out_shape=jax.ShapeDtypeStruct(o_buf.shape, o_buf.dtype),
        grid=(n_kv, n_lat // tq),
        in_specs=[pl.BlockSpec((tq, group * dk), lambda g, i: (i, q_blk0 + g)),
                  pl.BlockSpec((rows, dk), lambda g, i: (0, k_blk0 + g)),
                  pl.BlockSpec((rows, LANES), lambda g, i: (0, v_blk0 + g)),
                  _BUF_SPEC],
        out_specs=pl.BlockSpec((tq, group * LANES), lambda g, i: (i, o_blk0 + g)),
        scratch_shapes=[stat, stat, stat],
        input_output_aliases={3: 0},
        compiler_params=_params("parallel", "arbitrary"),
        name="flash",
    )(q_arr, k_arr, v_arr, o_buf)


def _lane_blocks(s):
    return [s[:, b * LANES:(b + 1) * LANES] for b in range(s.shape[1] // LANES)]


def _softmax_pv(segments, sink=None):
    blocks = [blk for blks, _ in segments for blk in blks]
    rows = blocks[0].shape[0]
    mx = blocks[0]
    for blk in blocks[1:]:
        mx = jnp.maximum(mx, blk)
    m = jnp.broadcast_to(jnp.max(mx, axis=-1, keepdims=True), (rows, LANES))
    if sink is not None:
        m = jnp.maximum(m, sink)
    out = l = None
    for blks, v in segments:
        ps = [jnp.exp2(blk - m) for blk in blks]
        for p in ps:
            l = p if l is None else l + p
        pv = _dot(jnp.concatenate([p.astype(BF16) for p in ps], axis=1), v)
        out = pv if out is None else out + pv
    if sink is not None:
        lane = lax.broadcasted_iota(jnp.int32, (rows, LANES), 1)
        l = l + jnp.where(lane == 0, jnp.exp2(sink - m), 0.0)
    return out * (1.0 / jnp.sum(l, axis=-1, keepdims=True))


def _window_kernel(sink_ref, q_ref, k_ref, v_ref, buf_ref, o_ref, *, group, tq, n_lat, n_ctx, window):
    del buf_ref
    g, n = pl.program_id(0), pl.program_id(1)
    span = tq + 2 * LANES
    kc, vc = k_ref[n_lat:n_lat + n_ctx, :], v_ref[n_lat:n_lat + n_ctx, :]
    row = jnp.bitwise_and(lax.broadcasted_iota(jnp.int32, (group * tq, LANES), 0), tq - 1)
    lane_minus_row = lax.broadcasted_iota(jnp.int32, (group * tq, LANES), 1) - row
    sink = jnp.concatenate([jnp.full((tq, LANES), sink_ref[g * group + i], F32) for i in range(group)], axis=0)
    n_sub = q_ref.shape[0] // tq
    for sub in range(n_sub):
        rows = slice(sub * tq, (sub + 1) * tq)
        q0 = (n * n_sub + sub) * tq
        start = pl.multiple_of(jnp.clip(q0 - LANES, 0, n_lat - span), LANES)
        kw, vw = k_ref[pl.ds(start, span), :], v_ref[pl.ds(start, span), :]
        q = jnp.concatenate([q_ref[rows, i * HEAD_DIM:(i + 1) * HEAD_DIM] for i in range(group)], axis=0)
        s_w = [jnp.where(jnp.abs(lane_minus_row + (start - q0 + b * LANES)) <= window, blk, NEG_INF)
               for b, blk in enumerate(_lane_blocks(_dot_nt(q, kw)))]
        out = _softmax_pv([(s_w, vw), (_lane_blocks(_dot_nt(q, kc)), vc)], sink=sink)
        for i in range(group):
            o_ref[rows, i * HEAD_DIM:(i + 1) * HEAD_DIM] = out[i * tq:(i + 1) * tq].astype(o_ref.dtype)


def _window_attention(main, sink, o_buf, n_lat, n_ctx):
    rows = main.shape[0]
    group = A_HEADS // A_KV_HEADS
    tq = _pick(n_lat, (256, 128))
    assert A_WINDOW <= LANES and n_lat >= tq + 2 * LANES
    n_sub = _pick(n_lat // tq, (2, 1))
    return pl.pallas_call(
        functools.partial(_window_kernel, group=group, tq=tq, n_lat=n_lat, n_ctx=n_ctx, window=A_WINDOW),
        out_shape=jax.ShapeDtypeStruct(o_buf.shape, o_buf.dtype),
        grid=(A_KV_HEADS, n_lat // (n_sub * tq)),
        in_specs=[pl.BlockSpec(memory_space=pltpu.SMEM),
                  pl.BlockSpec((n_sub * tq, group * HEAD_DIM), lambda g, n: (n, g)),
                  pl.BlockSpec((rows, HEAD_DIM), lambda g, n: (0, A_K0 + g)),
                  pl.BlockSpec((rows, HEAD_DIM), lambda g, n: (0, A_V0 + g)),
                  _BUF_SPEC],
        out_specs=pl.BlockSpec((n_sub * tq, group * HEAD_DIM), lambda g, n: (n, MIX_A0 // group + g)),
        input_output_aliases={4: 0},
        compiler_params=_params("parallel", "arbitrary"),
        name="window_attention",
    )(sink, main, main, main, o_buf)


NBR_Q_ROWS = 8
NBR_KEY_ROWS = 16


def _nbr_key_start(rb, n_grid_rows):
    return jnp.clip(rb * NBR_Q_ROWS - NA_KH // 2, 0, n_grid_rows - NBR_KEY_ROWS)


def _nbr_kernel(q_ref, k_ref, v_ref, b_ref, buf_ref, o_ref, *, chain_rows, n_grid_rows, n_lat, n_ctx):
    del buf_ref
    k0 = pl.multiple_of(_nbr_key_start(pl.program_id(1), n_grid_rows) * GRID_W, (NA_KH // 2) * GRID_W)
    n_keys = NBR_KEY_ROWS * GRID_W
    kw, vw = k_ref[pl.ds(k0, n_keys), :], v_ref[pl.ds(k0, n_keys), :]
    kc, vc = k_ref[n_lat:n_lat + n_ctx, :], v_ref[n_lat:n_lat + n_ctx, :]
    rows_per_chain = chain_rows // GRID_W
    for sub in range(q_ref.shape[0] // chain_rows):
        rows = slice(sub * chain_rows, (sub + 1) * chain_rows)
        q = q_ref[rows, :]
        s_nb = [blk + jnp.concatenate([b_ref[sub * rows_per_chain + jj, kp] for jj in range(rows_per_chain)], axis=0)
                for kp, blk in enumerate(_lane_blocks(_dot_nt(q, kw)))]
        out = _softmax_pv([(s_nb, vw), (_lane_blocks(_dot_nt(q, kc)), vc)])
        o_ref[rows, :] = out.astype(o_ref.dtype)


def _nbr_bias_table(rel_bias):
    assert 2 * GRID_W == LANES
    n_heads = rel_bias.shape[0]
    hp = lax.Precision.HIGHEST
    off = jnp.array([0, NA_KH // 2, NA_KH], jnp.int32)[:, None, None]
    j = jnp.arange(NBR_Q_ROWS)[None, :, None]
    kr = jnp.arange(NBR_KEY_ROWS)[None, None, :]
    centred = j - NA_KH // 2
    rs = jnp.stack([jnp.maximum(centred[0], 0), centred[0], jnp.minimum(centred[0], 0)])
    key_row = kr - off
    row_ok = (key_row >= rs) & (key_row < rs + NA_KH)
    row_sel = jax.nn.one_hot(key_row - j + (NA_KH - 1), 2 * NA_KH - 1, dtype=F32) * row_ok[..., None]
    feat = jnp.einsum("vjka,hab->vhjkb", row_sel, rel_bias.astype(F32) * LOG2E, precision=hp)
    bad = jnp.broadcast_to((~row_ok).astype(F32)[:, None, :, :, None], feat.shape[:-1] + (1,))
    feat = jnp.concatenate([feat, bad], axis=-1)
    feat = feat.reshape(feat.shape[:3] + (NBR_KEY_ROWS // 2, 2 * feat.shape[-1]))
    feat = jnp.concatenate([feat, jnp.ones(feat.shape[:-1] + (1,), F32)], axis=-1)
    c = jnp.arange(GRID_W)[:, None]
    kc = jnp.arange(GRID_W)[None, :]
    cstart = jnp.clip(c - NA_KW // 2, 0, GRID_W - NA_KW)
    col_ok = (kc >= cstart) & (kc < cstart + NA_KW)
    col_sel = jax.nn.one_hot(kc - c + (NA_KW - 1), 2 * NA_KW - 1, dtype=F32)
    per_row = jnp.concatenate([col_sel.transpose(0, 2, 1), jnp.full((GRID_W, 1, GRID_W), NEG_INF, F32)], axis=1)
    sel = jnp.einsum("pq,cxk->cpxqk", jnp.eye(2, dtype=F32), per_row).reshape(GRID_W, 4 * NA_KW, LANES)
    col_bad = jnp.tile(jnp.where(col_ok, 0.0, NEG_INF).astype(F32), (1, 2))[:, None, :]
    sel = jnp.concatenate([sel, col_bad], axis=1)
    return jnp.einsum("vhjkx,cxl->vhjkcl", feat, sel, precision=hp)


def _nbr_attention(main, rel_bias, o_buf, n_lat, n_ctx):
    rows = main.shape[0]
    n_grid_rows = n_lat // GRID_W
    assert n_grid_rows % NBR_Q_ROWS == 0 and n_grid_rows >= NBR_KEY_ROWS
    assert NBR_KEY_ROWS >= NBR_Q_ROWS + NA_KH and NA_KH // 2 * 2 == NA_KH
    n_blocks = n_grid_rows // NBR_Q_ROWS
    tq = NBR_Q_ROWS * GRID_W

    def variant(rb):
        return (rb * NBR_Q_ROWS - _nbr_key_start(rb, n_grid_rows)) // (NA_KH // 2)

    return pl.pallas_call(
        functools.partial(_nbr_kernel, chain_rows=tq // 2, n_grid_rows=n_grid_rows, n_lat=n_lat, n_ctx=n_ctx),
        out_shape=jax.ShapeDtypeStruct(o_buf.shape, o_buf.dtype),
        grid=(D_HEADS, n_blocks),
        in_specs=[pl.BlockSpec((tq, HEAD_DIM), lambda h, rb: (rb, D_Q0 + h)),
                  pl.BlockSpec((rows, HEAD_DIM), lambda h, rb: (0, D_K0 + h)),
                  pl.BlockSpec((rows, HEAD_DIM), lambda h, rb: (0, D_V0 + h)),
                  pl.BlockSpec((None, None, NBR_Q_ROWS, NBR_KEY_ROWS // 2, GRID_W, LANES),
                               lambda h, rb: (variant(rb), h, 0, 0, 0, 0)),
                  _BUF_SPEC],
        out_specs=pl.BlockSpec((tq, HEAD_DIM), lambda h, rb: (rb, MIX_D0 + h)),
        input_output_aliases={4: 0},
        compiler_params=_params("parallel", "arbitrary"),
        name="nbr_attention",
    )(main, main, main, _nbr_bias_table(rel_bias), o_buf)


def _ctx_attn_kernel(sink_ref, q_ref, k_ref, v_ref, buf_ref, o_ref):
    del buf_ref
    sink = sink_ref[pl.program_id(0)]
    s = _dot_nt(q_ref[...], k_ref[...])
    m = jnp.maximum(jnp.max(s, axis=-1, keepdims=True), sink)
    p = jnp.exp2(s - m)
    denom = jnp.sum(p, axis=-1, keepdims=True) + jnp.exp2(sink - m)
    o_ref[...] = (_dot(p.astype(BF16), v_ref[...]) * (1.0 / denom)).astype(o_ref.dtype)


def _ctx_attention(q_arr, k_arr, v_arr, sink, o_buf, *, n_heads, group, dk, q_blk0, k_blk0, v_blk0, o_blk0, n_lat,
                   n_ctx):
    rb = n_lat // n_ctx
    return pl.pallas_call(
        _ctx_attn_kernel,
        out_shape=jax.ShapeDtypeStruct(o_buf.shape, o_buf.dtype),
        grid=(n_heads,),
        in_specs=[pl.BlockSpec(memory_space=pltpu.SMEM),
                  pl.BlockSpec((n_ctx, dk), lambda h: (rb, q_blk0 + h)),
                  pl.BlockSpec((n_ctx, dk), lambda h: (rb, k_blk0 + h // group)),
                  pl.BlockSpec((n_ctx, LANES), lambda h: (rb, v_blk0 + h // group)),
                  _BUF_SPEC],
        out_specs=pl.BlockSpec((n_ctx, LANES), lambda h: (rb, o_blk0 + h)),
        input_output_aliases={4: 0},
        compiler_params=_params("parallel"),
        name="ctx_attention",
    )(sink, q_arr, k_arr, v_arr, o_buf)


def _transpose_w_in(w, q_rank, kv_rank):
    a_cols = (A_HEADS + 2 * A_KV_HEADS) * HEAD_DIM
    b_cols = (B_HEADS + 2 * B_KV_HEADS) * HEAD_DIM
    c_cols = q_rank + kv_rank + C_ROPE_DIM
    return jnp.swapaxes(w, -1, -2).astype(BF16), (a_cols + b_cols, a_cols + b_cols + c_cols)


def _relayout_w_q_up(w):
    lead = w.shape[:-1]
    w = w.astype(BF16).reshape(lead + (C_HEADS, C_NOPE_DIM + C_ROPE_DIM))
    w = jnp.pad(w, ((0, 0),) * (len(lead) + 1) + ((0, C_QK_PAD - C_NOPE_DIM - C_ROPE_DIM),))
    return w.reshape(lead + (C_HEADS * C_QK_PAD,))


def _relayout_w_kv_up(w):
    lead = w.shape[:-1]
    w = w.astype(BF16).reshape(lead + (C_HEADS, C_NOPE_DIM + C_V_DIM))
    return jnp.concatenate([w[..., :C_NOPE_DIM].reshape(lead + (-1,)), w[..., C_NOPE_DIM:].reshape(lead + (-1,))],
                           axis=-1)


def kernel(x, c, ctx, c_ctx, w_mod_down, w_mod_up, norm_ffn1, ffn1_w_gu, ffn1_w_down, norm_mix, w_in,
           a_sink, b_q_norm, b_k_norm, c_q_norm, c_kv_norm, c_w_q_up, c_w_kv_up, d_rel_bias, w_out,
           norm_ffn2, ffn2_w_gu, ffn2_w_down, final_norm):
    bsz, n_lat, d = x.shape
    n_ctx = ctx.shape[1]
    n_layers = w_in.shape[0]
    q_rank, kv_rank = c_q_norm.shape[1], c_kv_norm.shape[1]
    assert bsz == 1 and n_lat % GRID_W == 0 and n_lat % n_ctx == 0 and n_ctx % LANES == 0
    rows = n_lat + n_ctx
    tr = n_ctx
    tm_all = (_pick(rows, (1408, 768, 512, 256, 128)), _pick(rows, (768, 512, 256, 128)))
    tm_lat = (_pick(n_lat, (1024, 512, 256, 128)),) * 2
    flash_tc = _pick(n_lat, (1024, 512, 256, 128))
    flash_tk = _pick(n_lat, (2048, 1024, 512, 256, 128))

    mod = _modulation(c, c_ctx, w_mod_down, w_mod_up)
    tables = _rope_tables(n_lat, n_ctx)
    no_sink = jnp.full((max(B_HEADS, C_HEADS, D_HEADS),), NEG_INF, F32)

    w_in_t, mla_rows = _transpose_w_in(w_in, q_rank, kv_rank)
    w_q_up_b, w_kv_up_b = _relayout_w_q_up(c_w_q_up), _relayout_w_kv_up(c_w_kv_up)
    w_dn1, w_dn2, w_out_b = ffn1_w_down.astype(BF16), ffn2_w_down.astype(BF16), w_out.astype(BF16)

    xs = jnp.concatenate([x[0], ctx[0]], axis=0)

    def ffn(xs, gain, w_gu, w_down, l, mod_l, k0, n_rows, tm):
        h = _norm_mod(xs, gain, mod_l, k0, k0 + 1, n_lat, tr)
        act = _gate_up(h, w_gu, l, n_rows, tm[0])
        return _residual_matmul(act, w_down, l, xs, mod_l, k0 + 2, 0.5, n_rows, n_lat, tm[1])

    for l in range(n_layers):
        need_ctx = l < n_layers - 1
        n_rows, tm = (rows, tm_all) if need_ctx else (n_lat, tm_lat)
        mod_l = mod[l]
        xs = ffn(xs, norm_ffn1[l], ffn1_w_gu, w_dn1, l, mod_l, 0, rows, tm_all)

        h = _norm_mod(xs, norm_mix[l], mod_l, 3, 4, n_lat, tr)
        main = _in_proj(h, w_in_t, l, tables, b_q_norm[l], b_k_norm[l], tm_all[0], mla_rows)
        qa, kva, kpe = _mla_down(h, w_in_t, l, tables, c_q_norm[l], c_kv_norm[l], tm_all[1], mla_rows)
        qc, kc, vc = _mla_expand(qa, kva, kpe, w_q_up_b, w_kv_up_b, l, tables[2], tables[3], tr)

        sink = a_sink[l] * LOG2E
        lat = dict(n_lat=n_lat, n_ctx=n_ctx)
        o = jnp.zeros((n_rows, MIX_COLS), BF16)
        o = _window_attention(main, sink, o, **lat)
        o = _flash(main, main, main, o, n_kv=B_KV_HEADS, group=B_HEADS // B_KV_HEADS, dk=HEAD_DIM,
                   q_blk0=B_Q0 // (B_HEADS // B_KV_HEADS), k_blk0=B_K0, v_blk0=B_V0,
                   o_blk0=MIX_B0 // (B_HEADS // B_KV_HEADS), tq=flash_tc, tc=flash_tc, tk=flash_tk, **lat)
        o = _flash(qc, kc, vc, o, n_kv=C_HEADS, group=1, dk=C_QK_PAD, q_blk0=0, k_blk0=0, v_blk0=0, o_blk0=MIX_C0,
                   tq=_pick(n_lat, (4 * flash_tc, 2 * flash_tc, flash_tc)), tc=flash_tc, tk=flash_tk, **lat)
        o = _nbr_attention(main, d_rel_bias[l], o, **lat)
        if need_ctx:
            o = _ctx_attention(main, main, main, sink, o, n_heads=A_HEADS, group=A_HEADS // A_KV_HEADS,
                               dk=HEAD_DIM, q_blk0=A_Q0, k_blk0=A_K0, v_blk0=A_V0, o_blk0=MIX_A0, **lat)
            o = _ctx_attention(main, main, main, no_sink, o, n_heads=B_HEADS, group=B_HEADS // B_KV_HEADS,
                               dk=HEAD_DIM, q_blk0=B_Q0, k_blk0=B_K0, v_blk0=B_V0, o_blk0=MIX_B0, **lat)
            o = _ctx_attention(qc, kc, vc, no_sink, o, n_heads=C_HEADS, group=1, dk=C_QK_PAD,
                               q_blk0=0, k_blk0=0, v_blk0=0, o_blk0=MIX_C0, **lat)
            o = _ctx_attention(main, main, main, no_sink, o, n_heads=D_HEADS, group=1, dk=HEAD_DIM,
                               q_blk0=D_Q0, k_blk0=D_K0, v_blk0=D_V0, o_blk0=MIX_D0, **lat)

        xs = _residual_matmul(o, w_out_b, l, xs, mod_l, 5, 1.0, n_rows, n_lat, tm[1])
        xs = ffn(xs, norm_ffn2[l], ffn2_w_gu, w_dn2, l, mod_l, 6, n_rows, tm)

    return _final_norm(xs[:n_lat], final_norm, tr)[None]
```

```python
import functools

import jax
import jax.numpy as jnp
from jax import lax
from jax.experimental import pallas as pl
from jax.experimental.pallas import tpu as pltpu

F32 = jnp.float32
BF16 = jnp.bfloat16

GRID_W = 64
HEAD_DIM = 128
ROPE_THETA = 10000.0
NORM_EPS = 1e-6
NEG_INF = -1e30
LOG2E = 1.4426950408889634
N_MOD = 9
A_HEADS, A_KV_HEADS, A_WINDOW = 8, 2, 128
B_HEADS, B_KV_HEADS = 8, 2
C_HEADS, C_NOPE_DIM, C_ROPE_DIM, C_V_DIM = 8, 128, 64, 128
D_HEADS, NA_KH, NA_KW = 8, 8, 16

LANES = 128
VMEM_LIMIT_BYTES = 56 * 1024 * 1024
C_QK_PAD = 2 * LANES

A_Q0, A_K0, A_V0 = 0, 8, 10
B_Q0, B_K0, B_V0 = 12, 20, 22
D_Q0, D_K0, D_V0 = 24, 32, 40
MAIN_COLS = 48 * LANES
MIX_A0, MIX_B0, MIX_C0, MIX_D0 = 0, 8, 16, 24
MIX_COLS = 32 * LANES
_BUF_SPEC = pl.BlockSpec(memory_space=pl.ANY)


def _params(*sem):
    return pltpu.CompilerParams(dimension_semantics=sem, vmem_limit_bytes=VMEM_LIMIT_BYTES)


def _pick(n, prefs):
    for p in prefs:
        if n % p == 0:
            return p
    raise ValueError(f"no tile in {prefs} divides {n}")


def _dot(a, b):
    return jnp.dot(a, b, preferred_element_type=F32)


def _dot_nt(a, b):
    return lax.dot_general(a, b, (((1,), (1,)), ((), ())), preferred_element_type=F32)


def _silu(x):
    return x / (1.0 + jnp.exp(-x))


def _rope(x, cos, sin_signed, half):
    n = x.shape[-1]
    lane = lax.broadcasted_iota(jnp.int32, x.shape, x.ndim - 1)
    first = jnp.bitwise_and(lane, 2 * half - 1) < half
    rot = jnp.where(first, pltpu.roll(x, n - half, x.ndim - 1), pltpu.roll(x, half, x.ndim - 1))
    return x * cos + rot * sin_signed


def _mod_down_kernel(s_ref, w_ref, o_ref, acc_ref):
    k = pl.program_id(1)

    @pl.when(k == 0)
    def _():
        acc_ref[...] = jnp.zeros_like(acc_ref)

    acc_ref[...] += _dot(_silu(s_ref[...]).astype(BF16), w_ref[...].astype(BF16))

    @pl.when(k == pl.num_programs(1) - 1)
    def _():
        o_ref[...] = acc_ref[...]


def _mod_up_kernel(t_ref, w_ref, o_ref):
    o_ref[...] = _dot(t_ref[...].astype(BF16), w_ref[...].astype(BF16))


def _modulation(c, c_ctx, w_down, w_up):
    n_layers, d, rank = w_down.shape
    n_out = w_up.shape[2]
    s = jnp.zeros((8, d), F32).at[0].set(c[0]).at[1].set(c_ctx)
    tk = _pick(d, (1024, 512, 256, 128))
    t = pl.pallas_call(
        _mod_down_kernel,
        out_shape=jax.ShapeDtypeStruct((n_layers, 8, rank), F32),
        grid=(n_layers, d // tk),
        in_specs=[pl.BlockSpec((8, tk), lambda l, k: (0, k)),
                  pl.BlockSpec((None, tk, rank), lambda l, k: (l, k, 0))],
        out_specs=pl.BlockSpec((None, 8, rank), lambda l, k: (l, 0, 0)),
        scratch_shapes=[pltpu.VMEM((8, rank), F32)],
        compiler_params=_params("parallel", "arbitrary"),
        name="mod_down",
    )(s, w_down)
    tn = _pick(n_out, (2048, 1024, 512, 256, 128))
    m = pl.pallas_call(
        _mod_up_kernel,
        out_shape=jax.ShapeDtypeStruct((n_layers, 8, n_out), F32),
        grid=(n_layers, n_out // tn),
        in_specs=[pl.BlockSpec((None, 8, rank), lambda l, j: (l, 0, 0)),
                  pl.BlockSpec((None, rank, tn), lambda l, j: (l, 0, j))],
        out_specs=pl.BlockSpec((None, 8, tn), lambda l, j: (l, 0, j)),
        compiler_params=_params("parallel", "parallel"),
        name="mod_up",
    )(t, w_up)
    return m[:, :2, :].reshape(n_layers, 2, N_MOD, 1, d)


def _norm_mod_kernel(x_ref, gain_ref, shift_ref, scale_ref, o_ref):
    x = x_ref[...]
    y = x * lax.rsqrt(jnp.mean(x * x, axis=-1, keepdims=True) + NORM_EPS) * gain_ref[...]
    o_ref[...] = (y * (1.0 + scale_ref[...]) + shift_ref[...]).astype(o_ref.dtype)


def _norm_mod(x, gain, mod_l, k_shift, k_scale, n_lat, tr):
    rows, d = x.shape
    n_lat_blocks = n_lat // tr

    def mod_spec(k):
        return pl.BlockSpec((None, None, 1, d),
                            lambda i: (jnp.where(i >= n_lat_blocks, 1, 0), k, 0, 0))

    return pl.pallas_call(
        _norm_mod_kernel,
        out_shape=jax.ShapeDtypeStruct((rows, d), BF16),
        grid=(rows // tr,),
        in_specs=[pl.BlockSpec((tr, d), lambda i: (i, 0)),
                  pl.BlockSpec((1, d), lambda i: (0, 0)),
                  mod_spec(k_shift), mod_spec(k_scale)],
        out_specs=pl.BlockSpec((tr, d), lambda i: (i, 0)),
        compiler_params=_params("parallel"),
        name="norm_mod",
    )(x, gain.reshape(1, d), mod_l, mod_l)


def _final_norm_kernel(x_ref, gain_ref, o_ref):
    x = x_ref[...]
    o_ref[...] = x * lax.rsqrt(jnp.mean(x * x, axis=-1, keepdims=True) + NORM_EPS) * gain_ref[...]


def _final_norm(x, gain, tr):
    rows, d = x.shape
    return pl.pallas_call(
        _final_norm_kernel,
        out_shape=jax.ShapeDtypeStruct((rows, d), F32),
        grid=(rows // tr,),
        in_specs=[pl.BlockSpec((tr, d), lambda i: (i, 0)),
                  pl.BlockSpec((1, d), lambda i: (0, 0))],
        out_specs=pl.BlockSpec((tr, d), lambda i: (i, 0)),
        compiler_params=_params("parallel"),
        name="final_norm",
    )(x, gain.reshape(1, d))


def _gate_up_kernel(h_ref, wg_ref, wu_ref, o_ref):
    h = h_ref[...]
    g = _dot(h, wg_ref[...].astype(BF16))
    u = _dot(h, wu_ref[...].astype(BF16))
    o_ref[...] = (_silu(g) * u).astype(o_ref.dtype)


def _gate_up(h, w_gu, layer, n_rows, tm):
    d = h.shape[1]
    f = w_gu.shape[2] // 2
    tn = _pick(f, (256, 128))
    nj = f // tn
    return pl.pallas_call(
        _gate_up_kernel,
        out_shape=jax.ShapeDtypeStruct((n_rows, f), BF16),
        grid=(n_rows // tm, nj),
        in_specs=[pl.BlockSpec((tm, d), lambda i, j: (i, 0)),
                  pl.BlockSpec((None, d, tn), lambda i, j: (layer, 0, j)),
                  pl.BlockSpec((None, d, tn), lambda i, j: (layer, 0, j + nj))],
        out_specs=pl.BlockSpec((tm, tn), lambda i, j: (i, j)),
        compiler_params=_params("parallel", "arbitrary"),
        name="gate_up",
    )(h, w_gu, w_gu)


def _residual_kernel(a_ref, w_ref, x_ref, g_ref, o_ref, *, coef, n_lat, tm):
    y = _dot(a_ref[...], w_ref[...])
    row = pl.program_id(0) * tm + lax.broadcasted_iota(jnp.int32, (tm, 1), 0)
    gate = jnp.where(row < n_lat, g_ref[0], g_ref[1])
    o_ref[...] = x_ref[...] + (coef * gate) * y


def _residual_matmul(a, w, layer, x, mod_l, k_gate, coef, n_rows, n_lat, tm):
    kdim = a.shape[1]
    d = w.shape[2]
    tn = _pick(d, (512, 256, 128))
    return pl.pallas_call(
        functools.partial(_residual_kernel, coef=coef, n_lat=n_lat, tm=tm),
        out_shape=jax.ShapeDtypeStruct((n_rows, d), F32),
        grid=(n_rows // tm, d // tn),
        in_specs=[pl.BlockSpec((tm, kdim), lambda i, j: (i, 0)),
                  pl.BlockSpec((None, kdim, tn), lambda i, j: (layer, 0, j)),
                  pl.BlockSpec((tm, tn), lambda i, j: (i, j)),
                  pl.BlockSpec((2, None, 1, tn), lambda i, j: (0, k_gate, 0, j))],
        out_specs=pl.BlockSpec((tm, tn), lambda i, j: (i, j)),
        compiler_params=_params("parallel", "arbitrary"),
        name="residual_matmul",
    )(a, w, x, mod_l)


IN_TILE_BLOCKS = 4
_MAIN_OPS = (("rope_scale",) * A_HEADS + ("rope",) * A_KV_HEADS + ("copy",) * A_KV_HEADS
             + ("norm_q_rope_scale",) * B_HEADS + ("norm_k_rope",) * B_KV_HEADS + ("copy",) * B_KV_HEADS
             + ("scale",) * D_HEADS + ("copy",) * (2 * D_HEADS))


def _head_norm(x, g):
    return x * lax.rsqrt(jnp.mean(x * x, axis=-1, keepdims=True) + NORM_EPS) * g


def _in_proj_kernel(h_ref, w_ref, cos_ref, sin_ref, gq_ref, gk_ref, o_ref, *, row_chunks):
    j = pl.program_id(1)
    scale = HEAD_DIM ** -0.5 * LOG2E
    chunk = h_ref.shape[0] // row_chunks

    def apply(op, x, rows):
        if op in ("norm_q_rope_scale", "norm_k_rope"):
            x = _head_norm(x, (gq_ref if op == "norm_q_rope_scale" else gk_ref)[...])
        if "rope" in op:
            x = _rope(x, cos_ref[rows, :], sin_ref[rows, :], 32)
        return x * scale if "scale" in op else x

    n_tiles = len(_MAIN_OPS) // IN_TILE_BLOCKS
    recipes = [_MAIN_OPS[t * IN_TILE_BLOCKS:(t + 1) * IN_TILE_BLOCKS] for t in range(n_tiles)]
    for recipe in sorted(set(recipes)):
        tiles = [t for t in range(n_tiles) if recipes[t] == recipe]
        hit = functools.reduce(jnp.logical_or, [j == t for t in tiles])

        @pl.when(hit)
        def _(recipe=recipe):
            for r in range(row_chunks):
                rows = slice(r * chunk, (r + 1) * chunk)
                y = _dot_nt(h_ref[rows, :], w_ref[0])
                for b, op in enumerate(recipe):
                    cols = slice(b * LANES, (b + 1) * LANES)
                    o_ref[rows, cols] = apply(op, y[:, cols], rows).astype(o_ref.dtype)


def _in_proj(h, w_t, layer, tables, g_q, g_k, tm, mla_rows):
    rows, d = h.shape
    tn = IN_TILE_BLOCKS * LANES
    o2, o3 = mla_rows
    assert w_t.shape[1] - (o3 - o2) == MAIN_COLS and MAIN_COLS % tn == 0 and o2 % tn == 0
    row_chunks = 4 if tm % 64 == 0 else 1

    assert (o3 - o2) % C_ROPE_DIM == 0

    def w_row(j):
        return pl.multiple_of(jnp.where(j < o2 // tn, j * tn, o3 - o2 + j * tn), C_ROPE_DIM)

    return pl.pallas_call(
        functools.partial(_in_proj_kernel, row_chunks=row_chunks),
        out_shape=jax.ShapeDtypeStruct((rows, MAIN_COLS), BF16),
        grid=(rows // tm, MAIN_COLS // tn),
        in_specs=[pl.BlockSpec((tm, d), lambda i, j: (i, 0)),
                  pl.BlockSpec((pl.Element(1), pl.Element(tn), pl.Element(d)),
                               lambda i, j: (layer, w_row(j), 0)),
                  pl.BlockSpec((tm, LANES), lambda i, j: (i, 0)),
                  pl.BlockSpec((tm, LANES), lambda i, j: (i, 0)),
                  pl.BlockSpec((1, LANES), lambda i, j: (0, 0)),
                  pl.BlockSpec((1, LANES), lambda i, j: (0, 0))],
        out_specs=pl.BlockSpec((tm, tn), lambda i, j: (i, j)),
        compiler_params=_params("parallel", "arbitrary"),
        name="in_proj",
    )(h, w_t, tables[0], tables[1], g_q.reshape(1, -1), g_k.reshape(1, -1))


def _mla_down_kernel(h_ref, w_ref, cosc_ref, sinc_ref, gq_ref, gkv_ref, qa_ref, kva_ref, kpe_ref):
    q_rank, kv_rank = qa_ref.shape[1], kva_ref.shape[1]
    y = _dot_nt(h_ref[...], w_ref[0])
    qa_ref[...] = _head_norm(y[:, :q_rank], gq_ref[...]).astype(qa_ref.dtype)
    kva_ref[...] = _head_norm(y[:, q_rank:q_rank + kv_rank], gkv_ref[...]).astype(kva_ref.dtype)
    kpe = y[:, q_rank + kv_rank:q_rank + kv_rank + LANES]
    lane = lax.broadcasted_iota(jnp.int32, kpe.shape, 1)
    kpe = jnp.where(lane < C_ROPE_DIM, _rope(kpe, cosc_ref[...], sinc_ref[...], 16), 0.0)
    kpe_ref[...] = kpe.astype(kpe_ref.dtype)


def _mla_down(h, w_t, layer, tables, g_cq, g_ckv, tm, mla_rows):
    rows, d = h.shape
    q_rank, kv_rank = g_cq.shape[0], g_ckv.shape[0]
    cols = q_rank + kv_rank + LANES
    assert mla_rows[1] - mla_rows[0] == q_rank + kv_rank + C_ROPE_DIM and mla_rows[0] + cols <= w_t.shape[1]
    row_spec = lambda w: pl.BlockSpec((tm, w), lambda i: (i, 0))
    vec_spec = lambda w: pl.BlockSpec((1, w), lambda i: (0, 0))
    return pl.pallas_call(
        _mla_down_kernel,
        out_shape=(jax.ShapeDtypeStruct((rows, q_rank), BF16),
                   jax.ShapeDtypeStruct((rows, kv_rank), BF16),
                   jax.ShapeDtypeStruct((rows, LANES), BF16)),
        grid=(rows // tm,),
        in_specs=[row_spec(d),
                  pl.BlockSpec((pl.Element(1), pl.Element(cols), pl.Element(d)), lambda i: (layer, mla_rows[0], 0)),
                  row_spec(LANES), row_spec(LANES), vec_spec(q_rank), vec_spec(kv_rank)],
        out_specs=(row_spec(q_rank), row_spec(kv_rank), row_spec(LANES)),
        compiler_params=_params("parallel"),
        name="mla_down",
    )(h, w_t, tables[2], tables[3], g_cq.reshape(1, -1), g_ckv.reshape(1, -1))


def _rope_tables(n_lat, n_ctx):
    t = jnp.arange(n_lat, dtype=jnp.int32)
    row, col = (t // GRID_W).astype(F32), (t % GRID_W).astype(F32)
    lane = jnp.arange(LANES)

    def table(dim, live):
        half = dim // 2
        pair = half // 2
        inv = ROPE_THETA ** (-jnp.arange(0, half, 2, dtype=F32) / half)
        inv_lane = inv[lane % pair]
        pos = jnp.where(((lane // half) % 2 == 0)[None, :], row[:, None], col[:, None])
        ang = pos * inv_lane[None, :]
        sign = jnp.where((lane % half) < pair, -1.0, 1.0).astype(F32)
        on = (lane < live)[None, :]
        cos = jnp.where(on, jnp.cos(ang), 1.0)
        sin = jnp.where(on, jnp.sin(ang) * sign[None, :], 0.0)
        pad = lambda a, v: jnp.concatenate([a, jnp.full((n_ctx, LANES), v, F32)], axis=0)
        return pad(cos, 1.0), pad(sin, 0.0)

    cos, sin = table(HEAD_DIM, LANES)
    cosc, sinc = table(C_ROPE_DIM, C_ROPE_DIM)
    return cos, sin, cosc, sinc


def _cq_up_kernel(a_ref, w_ref, cos_ref, sin_ref, o_ref, *, scale):
    y = _dot(a_ref[...], w_ref[...])
    cos, sin = cos_ref[...], sin_ref[...]
    for h in range(C_HEADS):
        c0 = h * C_QK_PAD
        o_ref[:, c0:c0 + LANES] = (y[:, c0:c0 + LANES] * scale).astype(o_ref.dtype)
        pe = _rope(y[:, c0 + LANES:c0 + 2 * LANES], cos, sin, 16)
        o_ref[:, c0 + LANES:c0 + 2 * LANES] = (pe * scale).astype(o_ref.dtype)


def _ckv_up_kernel(a_ref, w_ref, kpe_ref, k_ref, v_ref):
    y = _dot(a_ref[...], w_ref[...])
    kpe = kpe_ref[...]
    for h in range(C_HEADS):
        k_ref[:, h * C_QK_PAD:h * C_QK_PAD + LANES] = y[:, h * LANES:(h + 1) * LANES].astype(k_ref.dtype)
        k_ref[:, h * C_QK_PAD + LANES:(h + 1) * C_QK_PAD] = kpe
    v_ref[...] = y[:, C_HEADS * LANES:].astype(v_ref.dtype)


def _mla_expand(qa, kva, kpe, w_q, w_kv, layer, cosc, sinc, tm):
    rows, q_rank = qa.shape
    kv_rank = kva.shape[1]
    scale = (C_NOPE_DIM + C_ROPE_DIM) ** -0.5 * LOG2E
    row_spec = lambda w: pl.BlockSpec((tm, w), lambda i: (i, 0))
    layer_spec = lambda w: pl.BlockSpec((None,) + w.shape[1:], lambda i: (layer, 0, 0))
    qc = pl.pallas_call(
        functools.partial(_cq_up_kernel, scale=scale),
        out_shape=jax.ShapeDtypeStruct((rows, C_HEADS * C_QK_PAD), BF16),
        grid=(rows // tm,),
        in_specs=[row_spec(q_rank), layer_spec(w_q), row_spec(LANES), row_spec(LANES)],
        out_specs=row_spec(C_HEADS * C_QK_PAD),
        compiler_params=_params("parallel"),
        name="mla_q_up",
    )(qa, w_q, cosc, sinc)
    kc, vc = pl.pallas_call(
        _ckv_up_kernel,
        out_shape=(jax.ShapeDtypeStruct((rows, C_HEADS * C_QK_PAD), BF16),
                   jax.ShapeDtypeStruct((rows, C_HEADS * C_V_DIM), BF16)),
        grid=(rows // tm,),
        in_specs=[row_spec(kv_rank), layer_spec(w_kv), row_spec(LANES)],
        out_specs=(row_spec(C_HEADS * C_QK_PAD), row_spec(C_HEADS * C_V_DIM)),
        compiler_params=_params("parallel"),
        name="mla_kv_up",
    )(kva, w_kv, kpe)
    return qc, kc, vc


def _flash_kernel(q_ref, k_ref, v_ref, buf_ref, o_ref, m_sc, l_sc, acc_sc, *, group, dk, tc, tk, n_lat, n_ctx):
    del buf_ref
    tq = q_ref.shape[0]
    chains = [(g, r) for g in range(group) for r in range(tq // tc)]

    def chunk(k, v, first):
        n_blocks = k.shape[0] // LANES
        for ci, (g, r) in enumerate(chains):
            s = _dot_nt(q_ref[r * tc:(r + 1) * tc, g * dk:(g + 1) * dk], k)
            blocks = [s[:, b * LANES:(b + 1) * LANES] for b in range(n_blocks)]
            mx = blocks[0]
            for blk in blocks[1:]:
                mx = jnp.maximum(mx, blk)
            m_new = jnp.broadcast_to(jnp.max(mx, axis=-1, keepdims=True), (tc, LANES))
            if not first:
                m_prev = m_sc[ci]
                m_new = jnp.maximum(m_prev, m_new)
                alpha = jnp.exp2(m_prev - m_new)
            ps = [jnp.exp2(blk - m_new) for blk in blocks]
            l_new = ps[0]
            for p in ps[1:]:
                l_new = l_new + p
            pv = _dot(jnp.concatenate([p.astype(BF16) for p in ps], axis=1), v)
            if first:
                l_sc[ci] = l_new
                acc_sc[ci] = pv
            else:
                l_sc[ci] = alpha * l_sc[ci] + l_new
                acc_sc[ci] = alpha * acc_sc[ci] + pv
            m_sc[ci] = m_new

    chunk(k_ref[n_lat:n_lat + n_ctx, :], v_ref[n_lat:n_lat + n_ctx, :], True)

    def step(c, carry):
        start = pl.multiple_of(c * tk, tk)
        chunk(k_ref[pl.ds(start, tk), :], v_ref[pl.ds(start, tk), :], False)
        return carry

    lax.fori_loop(0, n_lat // tk, step, 0)
    for ci, (g, r) in enumerate(chains):
        inv = 1.0 / jnp.sum(l_sc[ci], axis=-1, keepdims=True)
        o_ref[r * tc:(r + 1) * tc, g * LANES:(g + 1) * LANES] = (acc_sc[ci] * inv).astype(o_ref.dtype)


def _flash(q_arr, k_arr, v_arr, o_buf, *, n_kv, group, dk, q_blk0, k_blk0, v_blk0, o_blk0, tq, tc, tk, n_lat,
           n_ctx):
    rows = k_arr.shape[0]
    n_chains = group * (tq // tc)
    stat = pltpu.VMEM((n_chains, tc, LANES), F32)
    return pl.pallas_call(
        functools.partial(_flash_kernel, group=group, dk=dk, tc=tc, tk=tk, n_lat=n_lat, n_ctx=n_ctx),
        out_shape=jax.ShapeDtypeStruct(o_buf.shape, o_buf.dtype),
        grid=(n_kv, n_lat // tq),
        in_specs=[pl.BlockSpec((tq, group * dk), lambda g, i: (i, q_blk0 + g)),
                  pl.BlockSpec((rows, dk), lambda g, i: (0, k_blk0 + g)),
                  pl.BlockSpec((rows, LANES), lambda g, i: (0, v_blk0 + g)),
                  _BUF_SPEC],
        out_specs=pl.BlockSpec((tq, group * LANES), lambda g, i: (i, o_blk0 + g)),
        scratch_shapes=[stat, stat, stat],
        input_output_aliases={3: 0},
        compiler_params=_params("parallel", "arbitrary"),
        name="flash",
    )(q_arr, k_arr, v_arr, o_buf)


def _lane_blocks(s):
    return [s[:, b * LANES:(b + 1) * LANES] for b in range(s.shape[1] // LANES)]


def _softmax_pv(segments, sink=None):
    blocks = [blk for blks, _ in segments for blk in blks]
    rows = blocks[0].shape[0]
    mx = blocks[0]
    for blk in blocks[1:]:
        mx = jnp.maximum(mx, blk)
    m = jnp.broadcast_to(jnp.max(mx, axis=-1, keepdims=True), (rows, LANES))
    if sink is not None:
        m = jnp.maximum(m, sink)
    out = l = None
    for blks, v in segments:
        ps = [jnp.exp2(blk - m) for blk in blks]
        for p in ps:
            l = p if l is None else l + p
        pv = _dot(jnp.concatenate([p.astype(BF16) for p in ps], axis=1), v)
        out = pv if out is None else out + pv
    if sink is not None:
        lane = lax.broadcasted_iota(jnp.int32, (rows, LANES), 1)
        l = l + jnp.where(lane == 0, jnp.exp2(sink - m), 0.0)
    return out * (1.0 / jnp.sum(l, axis=-1, keepdims=True))


def _window_kernel(sink_ref, q_ref, k_ref, v_ref, buf_ref, o_ref, *, group, tq, n_lat, n_ctx, window):
    del buf_ref
    g, n = pl.program_id(0), pl.program_id(1)
    span = tq + 2 * LANES
    kc, vc = k_ref[n_lat:n_lat + n_ctx, :], v_ref[n_lat:n_lat + n_ctx, :]
    row = jnp.bitwise_and(lax.broadcasted_iota(jnp.int32, (group * tq, LANES), 0), tq - 1)
    lane_minus_row = lax.broadcasted_iota(jnp.int32, (group * tq, LANES), 1) - row
    sink = jnp.concatenate([jnp.full((tq, LANES), sink_ref[g * group + i], F32) for i in range(group)], axis=0)
    n_sub = q_ref.shape[0] // tq
    for sub in range(n_sub):
        rows = slice(sub * tq, (sub + 1) * tq)
        q0 = (n * n_sub + sub) * tq
        start = pl.multiple_of(jnp.clip(q0 - LANES, 0, n_lat - span), LANES)
        kw, vw = k_ref[pl.ds(start, span), :], v_ref[pl.ds(start, span), :]
        q = jnp.concatenate([q_ref[rows, i * HEAD_DIM:(i + 1) * HEAD_DIM] for i in range(group)], axis=0)
        s_w = [jnp.where(jnp.abs(lane_minus_row + (start - q0 + b * LANES)) <= window, blk, NEG_INF)
               for b, blk in enumerate(_lane_blocks(_dot_nt(q, kw)))]
        out = _softmax_pv([(s_w, vw), (_lane_blocks(_dot_nt(q, kc)), vc)], sink=sink)
        for i in range(group):
            o_ref[rows, i * HEAD_DIM:(i + 1) * HEAD_DIM] = out[i * tq:(i + 1) * tq].astype(o_ref.dtype)


def _window_attention(main, sink, o_buf, n_lat, n_ctx):
    rows = main.shape[0]
    group = A_HEADS // A_KV_HEADS
    tq = _pick(n_lat, (256, 128))
    assert A_WINDOW <= LANES and n_lat >= tq + 2 * LANES
    n_sub = _pick(n_lat // tq, (2, 1))
    return pl.pallas_call(
        functools.partial(_window_kernel, group=group, tq=tq, n_lat=n_lat, n_ctx=n_ctx, window=A_WINDOW),
        out_shape=jax.ShapeDtypeStruct(o_buf.shape, o_buf.dtype),
        grid=(A_KV_HEADS, n_lat // (n_sub * tq)),
        in_specs=[pl.BlockSpec(memory_space=pltpu.SMEM),
                  pl.BlockSpec((n_sub * tq, group * HEAD_DIM), lambda g, n: (n, g)),
                  pl.BlockSpec((rows, HEAD_DIM), lambda g, n: (0, A_K0 + g)),
                  pl.BlockSpec((rows, HEAD_DIM), lambda g, n: (0, A_V0 + g)),
                  _BUF_SPEC],
        out_specs=pl.BlockSpec((n_sub * tq, group * HEAD_DIM), lambda g, n: (n, MIX_A0 // group + g)),
        input_output_aliases={4: 0},
        compiler_params=_params("parallel", "arbitrary"),
        name="window_attention",
    )(sink, main, main, main, o_buf)


NBR_Q_ROWS = NA_KH // 2
NBR_KEY_ROWS = 12
NBR_CHAINS = 2


def _nbr_key_start(r0, n_grid_rows):
    return jnp.clip(r0 - NA_KH // 2, 0, n_grid_rows - NBR_KEY_ROWS)


def _nbr_kernel(q_ref, k_ref, v_ref, *rest, n_grid_rows, n_lat, n_ctx):
    b_refs, o_ref = rest[:NBR_CHAINS], rest[NBR_CHAINS + 1]
    chain_rows = NBR_Q_ROWS * GRID_W
    n_keys = NBR_KEY_ROWS * GRID_W
    kc, vc = k_ref[n_lat:n_lat + n_ctx, :], v_ref[n_lat:n_lat + n_ctx, :]
    for c in range(NBR_CHAINS):
        r0 = (pl.program_id(1) * NBR_CHAINS + c) * NBR_Q_ROWS
        k0 = pl.multiple_of(_nbr_key_start(r0, n_grid_rows) * GRID_W, chain_rows)
        kw, vw = k_ref[pl.ds(k0, n_keys), :], v_ref[pl.ds(k0, n_keys), :]
        rows = slice(c * chain_rows, (c + 1) * chain_rows)
        q = q_ref[rows, :]
        s_nb = [blk + jnp.concatenate([b_refs[c][jj, kp] for jj in range(NBR_Q_ROWS)], axis=0)
                for kp, blk in enumerate(_lane_blocks(_dot_nt(q, kw)))]
        out = _softmax_pv([(s_nb, vw), (_lane_blocks(_dot_nt(q, kc)), vc)])
        o_ref[rows, :] = out.astype(o_ref.dtype)


def _nbr_bias_table(rel_bias):
    assert 2 * GRID_W == LANES
    n_heads = rel_bias.shape[0]
    hp = lax.Precision.HIGHEST
    off = jnp.array([0, NA_KH // 2, NA_KH], jnp.int32)[:, None, None]
    j = jnp.arange(NBR_Q_ROWS)[None, :, None]
    kr = jnp.arange(NBR_KEY_ROWS)[None, None, :]
    centred = j - NA_KH // 2
    rs = jnp.stack([jnp.maximum(centred[0], 0), centred[0], jnp.minimum(centred[0], -(NA_KH // 2))])
    key_row = kr - off
    row_ok = (key_row >= rs) & (key_row < rs + NA_KH)
    row_sel = jax.nn.one_hot(key_row - j + (NA_KH - 1), 2 * NA_KH - 1, dtype=F32) * row_ok[..., None]
    feat = jnp.einsum("vjka,hab->vhjkb", row_sel, rel_bias.astype(F32) * LOG2E, precision=hp)
    bad = jnp.broadcast_to((~row_ok).astype(F32)[:, None, :, :, None], feat.shape[:-1] + (1,))
    feat = jnp.concatenate([feat, bad], axis=-1)
    feat = feat.reshape(feat.shape[:3] + (NBR_KEY_ROWS // 2, 2 * feat.shape[-1]))
    feat = jnp.concatenate([feat, jnp.ones(feat.shape[:-1] + (1,), F32)], axis=-1)
    c = jnp.arange(GRID_W)[:, None]
    kc = jnp.arange(GRID_W)[None, :]
    cstart = jnp.clip(c - NA_KW // 2, 0, GRID_W - NA_KW)
    col_ok = (kc >= cstart) & (kc < cstart + NA_KW)
    col_sel = jax.nn.one_hot(kc - c + (NA_KW - 1), 2 * NA_KW - 1, dtype=F32)
    per_row = jnp.concatenate([col_sel.transpose(0, 2, 1), jnp.full((GRID_W, 1, GRID_W), NEG_INF, F32)], axis=1)
    sel = jnp.einsum("pq,cxk->cpxqk", jnp.eye(2, dtype=F32), per_row).reshape(GRID_W, 4 * NA_KW, LANES)
    col_bad = jnp.tile(jnp.where(col_ok, 0.0, NEG_INF).astype(F32), (1, 2))[:, None, :]
    sel = jnp.concatenate([sel, col_bad], axis=1)
    return jnp.einsum("vhjkx,cxl->vhjkcl", feat, sel, precision=hp)


def _nbr_attention(main, rel_bias, o_buf, n_lat, n_ctx):
    rows = main.shape[0]
    n_grid_rows = n_lat // GRID_W
    step_rows = NBR_CHAINS * NBR_Q_ROWS
    assert n_grid_rows % step_rows == 0 and n_grid_rows >= NBR_KEY_ROWS
    assert NBR_KEY_ROWS >= NBR_Q_ROWS + NA_KH - 1 and NBR_KEY_ROWS % 2 == 0 and NA_KH % 2 == 0
    tq = step_rows * GRID_W
    table = _nbr_bias_table(rel_bias)

    def table_spec(c):
        def variant(rb):
            r0 = (rb * NBR_CHAINS + c) * NBR_Q_ROWS
            return (r0 - _nbr_key_start(r0, n_grid_rows)) // (NA_KH // 2)
        return pl.BlockSpec((None, None, NBR_Q_ROWS, NBR_KEY_ROWS // 2, GRID_W, LANES),
                            lambda h, rb: (variant(rb), h, 0, 0, 0, 0))

    return pl.pallas_call(
        functools.partial(_nbr_kernel, n_grid_rows=n_grid_rows, n_lat=n_lat, n_ctx=n_ctx),
        out_shape=jax.ShapeDtypeStruct(o_buf.shape, o_buf.dtype),
        grid=(D_HEADS, n_grid_rows // step_rows),
        in_specs=[pl.BlockSpec((tq, HEAD_DIM), lambda h, rb: (rb, D_Q0 + h)),
                  pl.BlockSpec((rows, HEAD_DIM), lambda h, rb: (0, D_K0 + h)),
                  pl.BlockSpec((rows, HEAD_DIM), lambda h, rb: (0, D_V0 + h)),
                  *[table_spec(c) for c in range(NBR_CHAINS)],
                  _BUF_SPEC],
        out_specs=pl.BlockSpec((tq, HEAD_DIM), lambda h, rb: (rb, MIX_D0 + h)),
        input_output_aliases={3 + NBR_CHAINS: 0},
        compiler_params=_params("parallel", "arbitrary"),
        name="nbr_attention",
    )(main, main, main, *([table] * NBR_CHAINS), o_buf)


def _ctx_attn_kernel(sink_ref, q_ref, k_ref, v_ref, buf_ref, o_ref):
    del buf_ref
    sink = sink_ref[pl.program_id(0)]
    s = _dot_nt(q_ref[...], k_ref[...])
    m = jnp.maximum(jnp.max(s, axis=-1, keepdims=True), sink)
    p = jnp.exp2(s - m)
    denom = jnp.sum(p, axis=-1, keepdims=True) + jnp.exp2(sink - m)
    o_ref[...] = (_dot(p.astype(BF16), v_ref[...]) * (1.0 / denom)).astype(o_ref.dtype)


def _ctx_attention(q_arr, k_arr, v_arr, sink, o_buf, *, n_heads, group, dk, q_blk0, k_blk0, v_blk0, o_blk0, n_lat,
                   n_ctx):
    rb = n_lat // n_ctx
    return pl.pallas_call(
        _ctx_attn_kernel,
        out_shape=jax.ShapeDtypeStruct(o_buf.shape, o_buf.dtype),
        grid=(n_heads,),
        in_specs=[pl.BlockSpec(memory_space=pltpu.SMEM),
                  pl.BlockSpec((n_ctx, dk), lambda h: (rb, q_blk0 + h)),
                  pl.BlockSpec((n_ctx, dk), lambda h: (rb, k_blk0 + h // group)),
                  pl.BlockSpec((n_ctx, LANES), lambda h: (rb, v_blk0 + h // group)),
                  _BUF_SPEC],
        out_specs=pl.BlockSpec((n_ctx, LANES), lambda h: (rb, o_blk0 + h)),
        input_output_aliases={4: 0},
        compiler_params=_params("parallel"),
        name="ctx_attention",
    )(sink, q_arr, k_arr, v_arr, o_buf)


def _transpose_w_in(w, q_rank, kv_rank):
    a_cols = (A_HEADS + 2 * A_KV_HEADS) * HEAD_DIM
    b_cols = (B_HEADS + 2 * B_KV_HEADS) * HEAD_DIM
    c_cols = q_rank + kv_rank + C_ROPE_DIM
    return jnp.swapaxes(w, -1, -2).astype(BF16), (a_cols + b_cols, a_cols + b_cols + c_cols)


def _relayout_w_q_up(w):
    lead = w.shape[:-1]
    w = w.astype(BF16).reshape(lead + (C_HEADS, C_NOPE_DIM + C_ROPE_DIM))
    w = jnp.pad(w, ((0, 0),) * (len(lead) + 1) + ((0, C_QK_PAD - C_NOPE_DIM - C_ROPE_DIM),))
    return w.reshape(lead + (C_HEADS * C_QK_PAD,))


def _relayout_w_kv_up(w):
    lead = w.shape[:-1]
    w = w.astype(BF16).reshape(lead + (C_HEADS, C_NOPE_DIM + C_V_DIM))
    return jnp.concatenate([w[..., :C_NOPE_DIM].reshape(lead + (-1,)), w[..., C_NOPE_DIM:].reshape(lead + (-1,))],
                           axis=-1)


def kernel(x, c, ctx, c_ctx, w_mod_down, w_mod_up, norm_ffn1, ffn1_w_gu, ffn1_w_down, norm_mix, w_in,
           a_sink, b_q_norm, b_k_norm, c_q_norm, c_kv_norm, c_w_q_up, c_w_kv_up, d_rel_bias, w_out,
           norm_ffn2, ffn2_w_gu, ffn2_w_down, final_norm):
    bsz, n_lat, d = x.shape
    n_ctx = ctx.shape[1]
    n_layers = w_in.shape[0]
    q_rank, kv_rank = c_q_norm.shape[1], c_kv_norm.shape[1]
    assert bsz == 1 and n_lat % GRID_W == 0 and n_lat % n_ctx == 0 and n_ctx % LANES == 0
    rows = n_lat + n_ctx
    tr = n_ctx
    tm_all = (_pick(rows, (1408, 768, 512, 256, 128)), _pick(rows, (768, 512, 256, 128)))
    tm_lat = (_pick(n_lat, (1024, 512, 256, 128)),) * 2
    flash_tc = _pick(n_lat, (1024, 512, 256, 128))
    flash_tk = _pick(n_lat, (2048, 1024, 512, 256, 128))

    mod = _modulation(c, c_ctx, w_mod_down, w_mod_up)
    tables = _rope_tables(n_lat, n_ctx)
    no_sink = jnp.full((max(B_HEADS, C_HEADS, D_HEADS),), NEG_INF, F32)

    w_in_t, mla_rows = _transpose_w_in(w_in, q_rank, kv_rank)
    w_q_up_b, w_kv_up_b = _relayout_w_q_up(c_w_q_up), _relayout_w_kv_up(c_w_kv_up)
    w_dn1, w_dn2, w_out_b = ffn1_w_down.astype(BF16), ffn2_w_down.astype(BF16), w_out.astype(BF16)

    xs = jnp.concatenate([x[0], ctx[0]], axis=0)

    def ffn(xs, gain, w_gu, w_down, l, mod_l, k0, n_rows, tm):
        h = _norm_mod(xs, gain, mod_l, k0, k0 + 1, n_lat, tr)
        act = _gate_up(h, w_gu, l, n_rows, tm[0])
        return _residual_matmul(act, w_down, l, xs, mod_l, k0 + 2, 0.5, n_rows, n_lat, tm[1])

    for l in range(n_layers):
        need_ctx = l < n_layers - 1
        n_rows, tm = (rows, tm_all) if need_ctx else (n_lat, tm_lat)
        mod_l = mod[l]
        xs = ffn(xs, norm_ffn1[l], ffn1_w_gu, w_dn1, l, mod_l, 0, rows, tm_all)

        h = _norm_mod(xs, norm_mix[l], mod_l, 3, 4, n_lat, tr)
        main = _in_proj(h, w_in_t, l, tables, b_q_norm[l], b_k_norm[l], tm_all[0], mla_rows)
        qa, kva, kpe = _mla_down(h, w_in_t, l, tables, c_q_norm[l], c_kv_norm[l], tm_all[1], mla_rows)
        qc, kc, vc = _mla_expand(qa, kva, kpe, w_q_up_b, w_kv_up_b, l, tables[2], tables[3], tr)

        sink = a_sink[l] * LOG2E
        lat = dict(n_lat=n_lat, n_ctx=n_ctx)
        o = jnp.zeros((n_rows, MIX_COLS), BF16)
        o = _window_attention(main, sink, o, **lat)
        o = _flash(main, main, main, o, n_kv=B_KV_HEADS, group=B_HEADS // B_KV_HEADS, dk=HEAD_DIM,
                   q_blk0=B_Q0 // (B_HEADS // B_KV_HEADS), k_blk0=B_K0, v_blk0=B_V0,
                   o_blk0=MIX_B0 // (B_HEADS // B_KV_HEADS), tq=flash_tc, tc=flash_tc, tk=flash_tk, **lat)
        o = _flash(qc, kc, vc, o, n_kv=C_HEADS, group=1, dk=C_QK_PAD, q_blk0=0, k_blk0=0, v_blk0=0, o_blk0=MIX_C0,
                   tq=_pick(n_lat, (4 * flash_tc, 2 * flash_tc, flash_tc)), tc=flash_tc, tk=flash_tk, **lat)
        o = _nbr_attention(main, d_rel_bias[l], o, **lat)
        if need_ctx:
            o = _ctx_attention(main, main, main, sink, o, n_heads=A_HEADS, group=A_HEADS // A_KV_HEADS,
                               dk=HEAD_DIM, q_blk0=A_Q0, k_blk0=A_K0, v_blk0=A_V0, o_blk0=MIX_A0, **lat)
            o = _ctx_attention(main, main, main, no_sink, o, n_heads=B_HEADS, group=B_HEADS // B_KV_HEADS,
                               dk=HEAD_DIM, q_blk0=B_Q0, k_blk0=B_K0, v_blk0=B_V0, o_blk0=MIX_B0, **lat)
            o = _ctx_attention(qc, kc, vc, no_sink, o, n_heads=C_HEADS, group=1, dk=C_QK_PAD,
                               q_blk0=0, k_blk0=0, v_blk0=0, o_blk0=MIX_C0, **lat)
            o = _ctx_attention(main, main, main, no_sink, o, n_heads=D_HEADS, group=1, dk=HEAD_DIM,
                               q_blk0=D_Q0, k_blk0=D_K0, v_blk0=D_V0, o_blk0=MIX_D0, **lat)

        xs = _residual_matmul(o, w_out_b, l, xs, mod_l, 5, 1.0, n_rows, n_lat, tm[1])
        xs = ffn(xs, norm_ffn2[l], ffn2_w_gu, w_dn2, l, mod_l, 6, n_rows, tm)

    return _final_norm(xs[:n_lat], final_norm, tr)[None]
```

```python
import functools

import jax
import jax.numpy as jnp
from jax import lax
from jax.experimental import pallas as pl
from jax.experimental.pallas import tpu as pltpu

F32 = jnp.float32
BF16 = jnp.bfloat16

GRID_W = 64
HEAD_DIM = 128
ROPE_THETA = 10000.0
NORM_EPS = 1e-6
NEG_INF = -1e30
LOG2E = 1.4426950408889634
N_MOD = 9
A_HEADS, A_KV_HEADS, A_WINDOW = 8, 2, 128
B_HEADS, B_KV_HEADS = 8, 2
C_HEADS, C_NOPE_DIM, C_ROPE_DIM, C_V_DIM = 8, 128, 64, 128
D_HEADS, NA_KH, NA_KW = 8, 8, 16

LANES = 128
VMEM_LIMIT_BYTES = 56 * 1024 * 1024
C_QK_PAD = 2 * LANES

A_Q0, A_K0, A_V0 = 0, 8, 10
B_Q0, B_K0, B_V0 = 12, 20, 22
D_Q0, D_K0, D_V0 = 24, 32, 40
MAIN_COLS = 48 * LANES
MIX_A0, MIX_B0, MIX_C0, MIX_D0 = 0, 8, 16, 24
MIX_COLS = 32 * LANES
_BUF_SPEC = pl.BlockSpec(memory_space=pl.ANY)


def _params(*sem):
    return pltpu.CompilerParams(dimension_semantics=sem, vmem_limit_bytes=VMEM_LIMIT_BYTES)


def _pick(n, prefs):
    for p in prefs:
        if n % p == 0:
            return p
    raise ValueError(f"no tile in {prefs} divides {n}")


def _dot(a, b):
    return jnp.dot(a, b, preferred_element_type=F32)


def _dot_nt(a, b):
    return lax.dot_general(a, b, (((1,), (1,)), ((), ())), preferred_element_type=F32)


def _silu(x):
    return x / (1.0 + jnp.exp(-x))


def _rope(x, cos, sin_signed, half):
    n = x.shape[-1]
    lane = lax.broadcasted_iota(jnp.int32, x.shape, x.ndim - 1)
    first = jnp.bitwise_and(lane, 2 * half - 1) < half
    rot = jnp.where(first, pltpu.roll(x, n - half, x.ndim - 1), pltpu.roll(x, half, x.ndim - 1))
    return x * cos + rot * sin_signed


def _mod_down_kernel(s_ref, w_ref, o_ref, acc_ref):
    k = pl.program_id(1)

    @pl.when(k == 0)
    def _():
        acc_ref[...] = jnp.zeros_like(acc_ref)

    acc_ref[...] += _dot(_silu(s_ref[...]).astype(BF16), w_ref[...].astype(BF16))

    @pl.when(k == pl.num_programs(1) - 1)
    def _():
        o_ref[...] = acc_ref[...]


def _mod_up_kernel(t_ref, w_ref, o_ref):
    o_ref[...] = _dot(t_ref[...].astype(BF16), w_ref[...].astype(BF16))


def _modulation(c, c_ctx, w_down, w_up):
    n_layers, d, rank = w_down.shape
    n_out = w_up.shape[2]
    s = jnp.zeros((8, d), F32).at[0].set(c[0]).at[1].set(c_ctx)
    tk = _pick(d, (1024, 512, 256, 128))
    t = pl.pallas_call(
        _mod_down_kernel,
        out_shape=jax.ShapeDtypeStruct((n_layers, 8, rank), F32),
        grid=(n_layers, d // tk),
        in_specs=[pl.BlockSpec((8, tk), lambda l, k: (0, k)),
                  pl.BlockSpec((None, tk, rank), lambda l, k: (l, k, 0))],
        out_specs=pl.BlockSpec((None, 8, rank), lambda l, k: (l, 0, 0)),
        scratch_shapes=[pltpu.VMEM((8, rank), F32)],
        compiler_params=_params("parallel", "arbitrary"),
        name="mod_down",
    )(s, w_down)
    tn = _pick(n_out, (2048, 1024, 512, 256, 128))
    m = pl.pallas_call(
        _mod_up_kernel,
        out_shape=jax.ShapeDtypeStruct((n_layers, 8, n_out), F32),
        grid=(n_layers, n_out // tn),
        in_specs=[pl.BlockSpec((None, 8, rank), lambda l, j: (l, 0, 0)),
                  pl.BlockSpec((None, rank, tn), lambda l, j: (l, 0, j))],
        out_specs=pl.BlockSpec((None, 8, tn), lambda l, j: (l, 0, j)),
        compiler_params=_params("parallel", "parallel"),
        name="mod_up",
    )(t, w_up)
    return m[:, :2, :].reshape(n_layers, 2, N_MOD, 1, d)


def _norm_mod_kernel(x_ref, gain_ref, shift_ref, scale_ref, o_ref):
    x = x_ref[...]
    y = x * lax.rsqrt(jnp.mean(x * x, axis=-1, keepdims=True) + NORM_EPS) * gain_ref[...]
    o_ref[...] = (y * (1.0 + scale_ref[...]) + shift_ref[...]).astype(o_ref.dtype)


def _norm_mod(x, gain, mod_l, k_shift, k_scale, n_lat, tr):
    rows, d = x.shape
    n_lat_blocks = n_lat // tr

    def mod_spec(k):
        return pl.BlockSpec((None, None, 1, d),
                            lambda i: (jnp.where(i >= n_lat_blocks, 1, 0), k, 0, 0))

    return pl.pallas_call(
        _norm_mod_kernel,
        out_shape=jax.ShapeDtypeStruct((rows, d), BF16),
        grid=(rows // tr,),
        in_specs=[pl.BlockSpec((tr, d), lambda i: (i, 0)),
                  pl.BlockSpec((1, d), lambda i: (0, 0)),
                  mod_spec(k_shift), mod_spec(k_scale)],
        out_specs=pl.BlockSpec((tr, d), lambda i: (i, 0)),
        compiler_params=_params("parallel"),
        name="norm_mod",
    )(x, gain.reshape(1, d), mod_l, mod_l)


def _final_norm_kernel(x_ref, gain_ref, o_ref):
    x = x_ref[...]
    o_ref[...] = x * lax.rsqrt(jnp.mean(x * x, axis=-1, keepdims=True) + NORM_EPS) * gain_ref[...]


def _final_norm(x, gain, tr):
    rows, d = x.shape
    return pl.pallas_call(
        _final_norm_kernel,
        out_shape=jax.ShapeDtypeStruct((rows, d), F32),
        grid=(rows // tr,),
        in_specs=[pl.BlockSpec((tr, d), lambda i: (i, 0)),
                  pl.BlockSpec((1, d), lambda i: (0, 0))],
        out_specs=pl.BlockSpec((tr, d), lambda i: (i, 0)),
        compiler_params=_params("parallel"),
        name="final_norm",
    )(x, gain.reshape(1, d))


def _gate_up_kernel(h_ref, wg_ref, wu_ref, o_ref):
    h = h_ref[...]
    g = _dot(h, wg_ref[...].astype(BF16))
    u = _dot(h, wu_ref[...].astype(BF16))
    o_ref[...] = (_silu(g) * u).astype(o_ref.dtype)


def _gate_up(h, w_gu, layer, n_rows, tm):
    d = h.shape[1]
    f = w_gu.shape[2] // 2
    tn = _pick(f, (256, 128))
    nj = f // tn
    return pl.pallas_call(
        _gate_up_kernel,
        out_shape=jax.ShapeDtypeStruct((n_rows, f), BF16),
        grid=(n_rows // tm, nj),
        in_specs=[pl.BlockSpec((tm, d), lambda i, j: (i, 0)),
                  pl.BlockSpec((None, d, tn), lambda i, j: (layer, 0, j)),
                  pl.BlockSpec((None, d, tn), lambda i, j: (layer, 0, j + nj))],
        out_specs=pl.BlockSpec((tm, tn), lambda i, j: (i, j)),
        compiler_params=_params("parallel", "arbitrary"),
        name="gate_up",
    )(h, w_gu, w_gu)


def _residual_kernel(a_ref, w_ref, x_ref, g_ref, o_ref, *, coef, n_lat, tm):
    y = _dot(a_ref[...], w_ref[...])
    row = pl.program_id(0) * tm + lax.broadcasted_iota(jnp.int32, (tm, 1), 0)
    gate = jnp.where(row < n_lat, g_ref[0], g_ref[1])
    o_ref[...] = x_ref[...] + (coef * gate) * y


def _residual_matmul(a, w, layer, x, mod_l, k_gate, coef, n_rows, n_lat, tm):
    kdim = a.shape[1]
    d = w.shape[2]
    tn = _pick(d, (512, 256, 128))
    return pl.pallas_call(
        functools.partial(_residual_kernel, coef=coef, n_lat=n_lat, tm=tm),
        out_shape=jax.ShapeDtypeStruct((n_rows, d), F32),
        grid=(n_rows // tm, d // tn),
        in_specs=[pl.BlockSpec((tm, kdim), lambda i, j: (i, 0)),
                  pl.BlockSpec((None, kdim, tn), lambda i, j: (layer, 0, j)),
                  pl.BlockSpec((tm, tn), lambda i, j: (i, j)),
                  pl.BlockSpec((2, None, 1, tn), lambda i, j: (0, k_gate, 0, j))],
        out_specs=pl.BlockSpec((tm, tn), lambda i, j: (i, j)),
        compiler_params=_params("parallel", "arbitrary"),
        name="residual_matmul",
    )(a, w, x, mod_l)


IN_TILE_BLOCKS = 4
_MAIN_OPS = (("rope_scale",) * A_HEADS + ("rope",) * A_KV_HEADS + ("copy",) * A_KV_HEADS
             + ("norm_q_rope_scale",) * B_HEADS + ("norm_k_rope",) * B_KV_HEADS + ("copy",) * B_KV_HEADS
             + ("scale",) * D_HEADS + ("copy",) * (2 * D_HEADS))


def _head_norm(x, g):
    return x * lax.rsqrt(jnp.mean(x * x, axis=-1, keepdims=True) + NORM_EPS) * g


def _in_proj_kernel(h_ref, w_ref, cos_ref, sin_ref, gq_ref, gk_ref, o_ref, *, row_chunks):
    j = pl.program_id(1)
    scale = HEAD_DIM ** -0.5 * LOG2E
    chunk = h_ref.shape[0] // row_chunks

    def apply(op, x, rows):
        if op in ("norm_q_rope_scale", "norm_k_rope"):
            x = _head_norm(x, (gq_ref if op == "norm_q_rope_scale" else gk_ref)[...])
        if "rope" in op:
            x = _rope(x, cos_ref[rows, :], sin_ref[rows, :], 32)
        return x * scale if "scale" in op else x

    n_tiles = len(_MAIN_OPS) // IN_TILE_BLOCKS
    recipes = [_MAIN_OPS[t * IN_TILE_BLOCKS:(t + 1) * IN_TILE_BLOCKS] for t in range(n_tiles)]
    for recipe in sorted(set(recipes)):
        tiles = [t for t in range(n_tiles) if recipes[t] == recipe]
        hit = functools.reduce(jnp.logical_or, [j == t for t in tiles])

        @pl.when(hit)
        def _(recipe=recipe):
            for r in range(row_chunks):
                rows = slice(r * chunk, (r + 1) * chunk)
                y = _dot_nt(h_ref[rows, :], w_ref[0])
                for b, op in enumerate(recipe):
                    cols = slice(b * LANES, (b + 1) * LANES)
                    o_ref[rows, cols] = apply(op, y[:, cols], rows).astype(o_ref.dtype)


def _in_proj(h, w_t, layer, tables, g_q, g_k, tm, mla_rows):
    rows, d = h.shape
    tn = IN_TILE_BLOCKS * LANES
    o2, o3 = mla_rows
    assert w_t.shape[1] - (o3 - o2) == MAIN_COLS and MAIN_COLS % tn == 0 and o2 % tn == 0
    row_chunks = 4 if tm % 64 == 0 else 1

    assert (o3 - o2) % C_ROPE_DIM == 0

    def w_row(j):
        return pl.multiple_of(jnp.where(j < o2 // tn, j * tn, o3 - o2 + j * tn), C_ROPE_DIM)

    return pl.pallas_call(
        functools.partial(_in_proj_kernel, row_chunks=row_chunks),
        out_shape=jax.ShapeDtypeStruct((rows, MAIN_COLS), BF16),
        grid=(rows // tm, MAIN_COLS // tn),
        in_specs=[pl.BlockSpec((tm, d), lambda i, j: (i, 0)),
                  pl.BlockSpec((pl.Element(1), pl.Element(tn), pl.Element(d)),
                               lambda i, j: (layer, w_row(j), 0)),
                  pl.BlockSpec((tm, LANES), lambda i, j: (i, 0)),
                  pl.BlockSpec((tm, LANES), lambda i, j: (i, 0)),
                  pl.BlockSpec((1, LANES), lambda i, j: (0, 0)),
                  pl.BlockSpec((1, LANES), lambda i, j: (0, 0))],
        out_specs=pl.BlockSpec((tm, tn), lambda i, j: (i, j)),
        compiler_params=_params("parallel", "arbitrary"),
        name="in_proj",
    )(h, w_t, tables[0], tables[1], g_q.reshape(1, -1), g_k.reshape(1, -1))


def _mla_down_kernel(h_ref, w_ref, cosc_ref, sinc_ref, gq_ref, gkv_ref, qa_ref, kva_ref, kpe_ref):
    q_rank, kv_rank = qa_ref.shape[1], kva_ref.shape[1]
    y = _dot_nt(h_ref[...], w_ref[0])
    qa_ref[...] = _head_norm(y[:, :q_rank], gq_ref[...]).astype(qa_ref.dtype)
    kva_ref[...] = _head_norm(y[:, q_rank:q_rank + kv_rank], gkv_ref[...]).astype(kva_ref.dtype)
    kpe = y[:, q_rank + kv_rank:q_rank + kv_rank + LANES]
    lane = lax.broadcasted_iota(jnp.int32, kpe.shape, 1)
    kpe = jnp.where(lane < C_ROPE_DIM, _rope(kpe, cosc_ref[...], sinc_ref[...], 16), 0.0)
    kpe_ref[...] = kpe.astype(kpe_ref.dtype)


def _mla_down(h, w_t, layer, tables, g_cq, g_ckv, tm, mla_rows):
    rows, d = h.shape
    q_rank, kv_rank = g_cq.shape[0], g_ckv.shape[0]
    cols = q_rank + kv_rank + LANES
    assert mla_rows[1] - mla_rows[0] == q_rank + kv_rank + C_ROPE_DIM and mla_rows[0] + cols <= w_t.shape[1]
    row_spec = lambda w: pl.BlockSpec((tm, w), lambda i: (i, 0))
    vec_spec = lambda w: pl.BlockSpec((1, w), lambda i: (0, 0))
    return pl.pallas_call(
        _mla_down_kernel,
        out_shape=(jax.ShapeDtypeStruct((rows, q_rank), BF16),
                   jax.ShapeDtypeStruct((rows, kv_rank), BF16),
                   jax.ShapeDtypeStruct((rows, LANES), BF16)),
        grid=(rows // tm,),
        in_specs=[row_spec(d),
                  pl.BlockSpec((pl.Element(1), pl.Element(cols), pl.Element(d)), lambda i: (layer, mla_rows[0], 0)),
                  row_spec(LANES), row_spec(LANES), vec_spec(q_rank), vec_spec(kv_rank)],
        out_specs=(row_spec(q_rank), row_spec(kv_rank), row_spec(LANES)),
        compiler_params=_params("parallel"),
        name="mla_down",
    )(h, w_t, tables[2], tables[3], g_cq.reshape(1, -1), g_ckv.reshape(1, -1))


def _rope_tables(n_lat, n_ctx):
    t = jnp.arange(n_lat, dtype=jnp.int32)
    row, col = (t // GRID_W).astype(F32), (t % GRID_W).astype(F32)
    lane = jnp.arange(LANES)

    def table(dim, live):
        half = dim // 2
        pair = half // 2
        inv = ROPE_THETA ** (-jnp.arange(0, half, 2, dtype=F32) / half)
        inv_lane = inv[lane % pair]
        pos = jnp.where(((lane // half) % 2 == 0)[None, :], row[:, None], col[:, None])
        ang = pos * inv_lane[None, :]
        sign = jnp.where((lane % half) < pair, -1.0, 1.0).astype(F32)
        on = (lane < live)[None, :]
        cos = jnp.where(on, jnp.cos(ang), 1.0)
        sin = jnp.where(on, jnp.sin(ang) * sign[None, :], 0.0)
        pad = lambda a, v: jnp.concatenate([a, jnp.full((n_ctx, LANES), v, F32)], axis=0)
        return pad(cos, 1.0), pad(sin, 0.0)

    cos, sin = table(HEAD_DIM, LANES)
    cosc, sinc = table(C_ROPE_DIM, C_ROPE_DIM)
    return cos, sin, cosc, sinc


def _cq_up_kernel(a_ref, w_ref, cos_ref, sin_ref, o_ref, *, scale):
    y = _dot(a_ref[...], w_ref[...])
    cos, sin = cos_ref[...], sin_ref[...]
    for h in range(C_HEADS):
        c0 = h * C_QK_PAD
        o_ref[:, c0:c0 + LANES] = (y[:, c0:c0 + LANES] * scale).astype(o_ref.dtype)
        pe = _rope(y[:, c0 + LANES:c0 + 2 * LANES], cos, sin, 16)
        o_ref[:, c0 + LANES:c0 + 2 * LANES] = (pe * scale).astype(o_ref.dtype)


def _ckv_up_kernel(a_ref, w_ref, kpe_ref, k_ref, v_ref):
    y = _dot(a_ref[...], w_ref[...])
    kpe = kpe_ref[...]
    for h in range(C_HEADS):
        k_ref[:, h * C_QK_PAD:h * C_QK_PAD + LANES] = y[:, h * LANES:(h + 1) * LANES].astype(k_ref.dtype)
        k_ref[:, h * C_QK_PAD + LANES:(h + 1) * C_QK_PAD] = kpe
    v_ref[...] = y[:, C_HEADS * LANES:].astype(v_ref.dtype)


def _mla_expand(qa, kva, kpe, w_q, w_kv, layer, cosc, sinc, tm):
    rows, q_rank = qa.shape
    kv_rank = kva.shape[1]
    scale = (C_NOPE_DIM + C_ROPE_DIM) ** -0.5 * LOG2E
    row_spec = lambda w: pl.BlockSpec((tm, w), lambda i: (i, 0))
    layer_spec = lambda w: pl.BlockSpec((None,) + w.shape[1:], lambda i: (layer, 0, 0))
    qc = pl.pallas_call(
        functools.partial(_cq_up_kernel, scale=scale),
        out_shape=jax.ShapeDtypeStruct((rows, C_HEADS * C_QK_PAD), BF16),
        grid=(rows // tm,),
        in_specs=[row_spec(q_rank), layer_spec(w_q), row_spec(LANES), row_spec(LANES)],
        out_specs=row_spec(C_HEADS * C_QK_PAD),
        compiler_params=_params("parallel"),
        name="mla_q_up",
    )(qa, w_q, cosc, sinc)
    kc, vc = pl.pallas_call(
        _ckv_up_kernel,
        out_shape=(jax.ShapeDtypeStruct((rows, C_HEADS * C_QK_PAD), BF16),
                   jax.ShapeDtypeStruct((rows, C_HEADS * C_V_DIM), BF16)),
        grid=(rows // tm,),
        in_specs=[row_spec(kv_rank), layer_spec(w_kv), row_spec(LANES)],
        out_specs=(row_spec(C_HEADS * C_QK_PAD), row_spec(C_HEADS * C_V_DIM)),
        compiler_params=_params("parallel"),
        name="mla_kv_up",
    )(kva, w_kv, kpe)
    return qc, kc, vc


def _flash_kernel(q_ref, k_ref, v_ref, buf_ref, o_ref, m_sc, l_sc, acc_sc, *, group, dk, tc, tk, n_lat, n_ctx):
    del buf_ref
    tq = q_ref.shape[0]
    chains = [(g, r) for g in range(group) for r in range(tq // tc)]

    def chunk(k, v, first):
        n_blocks = k.shape[0] // LANES
        for ci, (g, r) in enumerate(chains):
            s = _dot_nt(q_ref[r * tc:(r + 1) * tc, g * dk:(g + 1) * dk], k)
            blocks = [s[:, b * LANES:(b + 1) * LANES] for b in range(n_blocks)]
            mx = blocks[0]
            for blk in blocks[1:]:
                mx = jnp.maximum(mx, blk)
            m_new = jnp.broadcast_to(jnp.max(mx, axis=-1, keepdims=True), (tc, LANES))
            if not first:
                m_prev = m_sc[ci]
                m_new = jnp.maximum(m_prev, m_new)
                alpha = jnp.exp2(m_prev - m_new)
            ps = [jnp.exp2(blk - m_new) for blk in blocks]
            l_new = ps[0]
            for p in ps[1:]:
                l_new = l_new + p
            pv = _dot(jnp.concatenate([p.astype(BF16) for p in ps], axis=1), v)
            if first:
                l_sc[ci] = l_new
                acc_sc[ci] = pv
            else:
                l_sc[ci] = alpha * l_sc[ci] + l_new
                acc_sc[ci] = alpha * acc_sc[ci] + pv
            m_sc[ci] = m_new

    chunk(k_ref[n_lat:n_lat + n_ctx, :], v_ref[n_lat:n_lat + n_ctx, :], True)

    def step(c, carry):
        start = pl.multiple_of(c * tk, tk)
        chunk(k_ref[pl.ds(start, tk), :], v_ref[pl.ds(start, tk), :], False)
        return carry

    lax.fori_loop(0, n_lat // tk, step, 0)
    for ci, (g, r) in enumerate(chains):
        inv = 1.0 / jnp.sum(l_sc[ci], axis=-1, keepdims=True)
        o_ref[r * tc:(r + 1) * tc, g * LANES:(g + 1) * LANES] = (acc_sc[ci] * inv).astype(o_ref.dtype)


def _flash(q_arr, k_arr, v_arr, o_buf, *, n_kv, group, dk, q_blk0, k_blk0, v_blk0, o_blk0, tq, tc, tk, n_lat,
           n_ctx):
    rows = k_arr.shape[0]
    n_chains = group * (tq // tc)
    stat = pltpu.VMEM((n_chains, tc, LANES), F32)
    return pl.pallas_call(
        functools.partial(_flash_kernel, group=group, dk=dk, tc=tc, tk=tk, n_lat=n_lat, n_ctx=n_ctx),
        out_shape=jax.ShapeDtypeStruct(o_buf.shape, o_buf.dtype),
        grid=(n_kv, n_lat // tq),
        in_specs=[pl.BlockSpec((tq, group * dk), lambda g, i: (i, q_blk0 + g)),
                  pl.BlockSpec((rows, dk), lambda g, i: (0, k_blk0 + g)),
                  pl.BlockSpec((rows, LANES), lambda g, i: (0, v_blk0 + g)),
                  _BUF_SPEC],
        out_specs=pl.BlockSpec((tq, group * LANES), lambda g, i: (i, o_blk0 + g)),
        scratch_shapes=[stat, stat, stat],
        input_output_aliases={3: 0},
        compiler_params=_params("parallel", "arbitrary"),
        name="flash",
    )(q_arr, k_arr, v_arr, o_buf)


def _lane_blocks(s):
    return [s[:, b * LANES:(b + 1) * LANES] for b in range(s.shape[1] // LANES)]


def _softmax_pv(segments, sink=None):
    blocks = [blk for blks, _ in segments for blk in blks]
    rows = blocks[0].shape[0]
    mx = blocks[0]
    for blk in blocks[1:]:
        mx = jnp.maximum(mx, blk)
    m = jnp.broadcast_to(jnp.max(mx, axis=-1, keepdims=True), (rows, LANES))
    if sink is not None:
        m = jnp.maximum(m, sink)
    out = l = None
    for blks, v in segments:
        ps = [jnp.exp2(blk - m) for blk in blks]
        for p in ps:
            l = p if l is None else l + p
        pv = _dot(jnp.concatenate([p.astype(BF16) for p in ps], axis=1), v)
        out = pv if out is None else out + pv
    if sink is not None:
        lane = lax.broadcasted_iota(jnp.int32, (rows, LANES), 1)
        l = l + jnp.where(lane == 0, jnp.exp2(sink - m), 0.0)
    return out * (1.0 / jnp.sum(l, axis=-1, keepdims=True))


def _window_kernel(sink_ref, q_ref, k_ref, v_ref, mask_ref, buf_ref, o_ref, *, group, tq, n_lat, n_ctx):
    del buf_ref
    g, n = pl.program_id(0), pl.program_id(1)
    span = tq + 2 * LANES
    kc, vc = k_ref[n_lat:n_lat + n_ctx, :], v_ref[n_lat:n_lat + n_ctx, :]
    sink = jnp.concatenate([jnp.full((tq, LANES), sink_ref[g * group + i], F32) for i in range(group)], axis=0)
    n_sub = q_ref.shape[0] // tq
    for sub in range(n_sub):
        rows = slice(sub * tq, (sub + 1) * tq)
        q0 = (n * n_sub + sub) * tq
        start = pl.multiple_of(jnp.clip(q0 - LANES, 0, n_lat - span), LANES)
        kw, vw = k_ref[pl.ds(start, span), :], v_ref[pl.ds(start, span), :]
        q = jnp.concatenate([q_ref[rows, i * HEAD_DIM:(i + 1) * HEAD_DIM] for i in range(group)], axis=0)
        mask = mask_ref[(q0 - start) // LANES]
        s_w = [blk + jnp.concatenate([mask[:, b * LANES:(b + 1) * LANES]] * group, axis=0)
               for b, blk in enumerate(_lane_blocks(_dot_nt(q, kw)))]
        out = _softmax_pv([(s_w, vw), (_lane_blocks(_dot_nt(q, kc)), vc)], sink=sink)
        for i in range(group):
            o_ref[rows, i * HEAD_DIM:(i + 1) * HEAD_DIM] = out[i * tq:(i + 1) * tq].astype(o_ref.dtype)


def _window_mask(tq, window):
    off = (jnp.arange(3) * LANES)[:, None, None]
    row = jnp.arange(tq)[None, :, None]
    col = jnp.arange(tq + 2 * LANES)[None, None, :]
    return jnp.where(jnp.abs(col - off - row) <= window, 0.0, NEG_INF).astype(F32)


def _window_attention(main, sink, o_buf, n_lat, n_ctx):
    rows = main.shape[0]
    group = A_HEADS // A_KV_HEADS
    tq = _pick(n_lat, (256, 128))
    span = tq + 2 * LANES
    assert A_WINDOW <= LANES and n_lat >= span
    n_sub = _pick(n_lat // tq, (2, 1))
    return pl.pallas_call(
        functools.partial(_window_kernel, group=group, tq=tq, n_lat=n_lat, n_ctx=n_ctx),
        out_shape=jax.ShapeDtypeStruct(o_buf.shape, o_buf.dtype),
        grid=(A_KV_HEADS, n_lat // (n_sub * tq)),
        in_specs=[pl.BlockSpec(memory_space=pltpu.SMEM),
                  pl.BlockSpec((n_sub * tq, group * HEAD_DIM), lambda g, n: (n, g)),
                  pl.BlockSpec((rows, HEAD_DIM), lambda g, n: (0, A_K0 + g)),
                  pl.BlockSpec((rows, HEAD_DIM), lambda g, n: (0, A_V0 + g)),
                  pl.BlockSpec((3, tq, span), lambda g, n: (0, 0, 0)),
                  _BUF_SPEC],
        out_specs=pl.BlockSpec((n_sub * tq, group * HEAD_DIM), lambda g, n: (n, MIX_A0 // group + g)),
        input_output_aliases={5: 0},
        compiler_params=_params("parallel", "arbitrary"),
        name="window_attention",
    )(sink, main, main, main, _window_mask(tq, A_WINDOW), o_buf)


NBR_Q_ROWS = NA_KH // 2
NBR_KEY_ROWS = 12
NBR_CHAINS = 2


def _nbr_key_start(r0, n_grid_rows):
    return jnp.clip(r0 - NA_KH // 2, 0, n_grid_rows - NBR_KEY_ROWS)


def _nbr_kernel(q_ref, k_ref, v_ref, *rest, n_grid_rows, n_lat, n_ctx):
    b_refs, o_ref = rest[:NBR_CHAINS], rest[NBR_CHAINS + 1]
    chain_rows = NBR_Q_ROWS * GRID_W
    n_keys = NBR_KEY_ROWS * GRID_W
    kc, vc = k_ref[n_lat:n_lat + n_ctx, :], v_ref[n_lat:n_lat + n_ctx, :]
    for c in range(NBR_CHAINS):
        r0 = (pl.program_id(1) * NBR_CHAINS + c) * NBR_Q_ROWS
        k0 = pl.multiple_of(_nbr_key_start(r0, n_grid_rows) * GRID_W, chain_rows)
        kw, vw = k_ref[pl.ds(k0, n_keys), :], v_ref[pl.ds(k0, n_keys), :]
        rows = slice(c * chain_rows, (c + 1) * chain_rows)
        q = q_ref[rows, :]
        s_nb = [blk + jnp.concatenate([b_refs[c][jj, kp] for jj in range(NBR_Q_ROWS)], axis=0)
                for kp, blk in enumerate(_lane_blocks(_dot_nt(q, kw)))]
        out = _softmax_pv([(s_nb, vw), (_lane_blocks(_dot_nt(q, kc)), vc)])
        o_ref[rows, :] = out.astype(o_ref.dtype)


def _nbr_bias_table(rel_bias):
    assert 2 * GRID_W == LANES
    n_heads = rel_bias.shape[0]
    hp = lax.Precision.HIGHEST
    off = jnp.array([0, NA_KH // 2, NA_KH], jnp.int32)[:, None, None]
    j = jnp.arange(NBR_Q_ROWS)[None, :, None]
    kr = jnp.arange(NBR_KEY_ROWS)[None, None, :]
    centred = j - NA_KH // 2
    rs = jnp.stack([jnp.maximum(centred[0], 0), centred[0], jnp.minimum(centred[0], -(NA_KH // 2))])
    key_row = kr - off
    row_ok = (key_row >= rs) & (key_row < rs + NA_KH)
    row_sel = jax.nn.one_hot(key_row - j + (NA_KH - 1), 2 * NA_KH - 1, dtype=F32) * row_ok[..., None]
    feat = jnp.einsum("vjka,hab->vhjkb", row_sel, rel_bias.astype(F32) * LOG2E, precision=hp)
    bad = jnp.broadcast_to((~row_ok).astype(F32)[:, None, :, :, None], feat.shape[:-1] + (1,))
    feat = jnp.concatenate([feat, bad], axis=-1)
    feat = feat.reshape(feat.shape[:3] + (NBR_KEY_ROWS // 2, 2 * feat.shape[-1]))
    feat = jnp.concatenate([feat, jnp.ones(feat.shape[:-1] + (1,), F32)], axis=-1)
    c = jnp.arange(GRID_W)[:, None]
    kc = jnp.arange(GRID_W)[None, :]
    cstart = jnp.clip(c - NA_KW // 2, 0, GRID_W - NA_KW)
    col_ok = (kc >= cstart) & (kc < cstart + NA_KW)
    col_sel = jax.nn.one_hot(kc - c + (NA_KW - 1), 2 * NA_KW - 1, dtype=F32)
    per_row = jnp.concatenate([col_sel.transpose(0, 2, 1), jnp.full((GRID_W, 1, GRID_W), NEG_INF, F32)], axis=1)
    sel = jnp.einsum("pq,cxk->cpxqk", jnp.eye(2, dtype=F32), per_row).reshape(GRID_W, 4 * NA_KW, LANES)
    col_bad = jnp.tile(jnp.where(col_ok, 0.0, NEG_INF).astype(F32), (1, 2))[:, None, :]
    sel = jnp.concatenate([sel, col_bad], axis=1)
    return jnp.einsum("vhjkx,cxl->vhjkcl", feat, sel, precision=hp)


def _nbr_attention(main, rel_bias, o_buf, n_lat, n_ctx):
    rows = main.shape[0]
    n_grid_rows = n_lat // GRID_W
    step_rows = NBR_CHAINS * NBR_Q_ROWS
    assert n_grid_rows % step_rows == 0 and n_grid_rows >= NBR_KEY_ROWS
    assert NBR_KEY_ROWS >= NBR_Q_ROWS + NA_KH - 1 and NBR_KEY_ROWS % 2 == 0 and NA_KH % 2 == 0
    tq = step_rows * GRID_W
    table = _nbr_bias_table(rel_bias)

    def table_spec(c):
        def variant(rb):
            r0 = (rb * NBR_CHAINS + c) * NBR_Q_ROWS
            return (r0 - _nbr_key_start(r0, n_grid_rows)) // (NA_KH // 2)
        return pl.BlockSpec((None, None, NBR_Q_ROWS, NBR_KEY_ROWS // 2, GRID_W, LANES),
                            lambda h, rb: (variant(rb), h, 0, 0, 0, 0))

    return pl.pallas_call(
        functools.partial(_nbr_kernel, n_grid_rows=n_grid_rows, n_lat=n_lat, n_ctx=n_ctx),
        out_shape=jax.ShapeDtypeStruct(o_buf.shape, o_buf.dtype),
        grid=(D_HEADS, n_grid_rows // step_rows),
        in_specs=[pl.BlockSpec((tq, HEAD_DIM), lambda h, rb: (rb, D_Q0 + h)),
                  pl.BlockSpec((rows, HEAD_DIM), lambda h, rb: (0, D_K0 + h)),
                  pl.BlockSpec((rows, HEAD_DIM), lambda h, rb: (0, D_V0 + h)),
                  *[table_spec(c) for c in range(NBR_CHAINS)],
                  _BUF_SPEC],
        out_specs=pl.BlockSpec((tq, HEAD_DIM), lambda h, rb: (rb, MIX_D0 + h)),
        input_output_aliases={3 + NBR_CHAINS: 0},
        compiler_params=_params("parallel", "arbitrary"),
        name="nbr_attention",
    )(main, main, main, *([table] * NBR_CHAINS), o_buf)


def _ctx_attn_kernel(sink_ref, q_ref, k_ref, v_ref, buf_ref, o_ref):
    del buf_ref
    sink = sink_ref[pl.program_id(0)]
    s = _dot_nt(q_ref[...], k_ref[...])
    m = jnp.maximum(jnp.max(s, axis=-1, keepdims=True), sink)
    p = jnp.exp2(s - m)
    denom = jnp.sum(p, axis=-1, keepdims=True) + jnp.exp2(sink - m)
    o_ref[...] = (_dot(p.astype(BF16), v_ref[...]) * (1.0 / denom)).astype(o_ref.dtype)


def _ctx_attention(q_arr, k_arr, v_arr, sink, o_buf, *, n_heads, group, dk, q_blk0, k_blk0, v_blk0, o_blk0, n_lat,
                   n_ctx):
    rb = n_lat // n_ctx
    return pl.pallas_call(
        _ctx_attn_kernel,
        out_shape=jax.ShapeDtypeStruct(o_buf.shape, o_buf.dtype),
        grid=(n_heads,),
        in_specs=[pl.BlockSpec(memory_space=pltpu.SMEM),
                  pl.BlockSpec((n_ctx, dk), lambda h: (rb, q_blk0 + h)),
                  pl.BlockSpec((n_ctx, dk), lambda h: (rb, k_blk0 + h // group)),
                  pl.BlockSpec((n_ctx, LANES), lambda h: (rb, v_blk0 + h // group)),
                  _BUF_SPEC],
        out_specs=pl.BlockSpec((n_ctx, LANES), lambda h: (rb, o_blk0 + h)),
        input_output_aliases={4: 0},
        compiler_params=_params("parallel"),
        name="ctx_attention",
    )(sink, q_arr, k_arr, v_arr, o_buf)


def _transpose_w_in(w, q_rank, kv_rank):
    a_cols = (A_HEADS + 2 * A_KV_HEADS) * HEAD_DIM
    b_cols = (B_HEADS + 2 * B_KV_HEADS) * HEAD_DIM
    c_cols = q_rank + kv_rank + C_ROPE_DIM
    return jnp.swapaxes(w, -1, -2).astype(BF16), (a_cols + b_cols, a_cols + b_cols + c_cols)


def _relayout_w_q_up(w):
    lead = w.shape[:-1]
    w = w.astype(BF16).reshape(lead + (C_HEADS, C_NOPE_DIM + C_ROPE_DIM))
    w = jnp.pad(w, ((0, 0),) * (len(lead) + 1) + ((0, C_QK_PAD - C_NOPE_DIM - C_ROPE_DIM),))
    return w.reshape(lead + (C_HEADS * C_QK_PAD,))


def _relayout_w_kv_up(w):
    lead = w.shape[:-1]
    w = w.astype(BF16).reshape(lead + (C_HEADS, C_NOPE_DIM + C_V_DIM))
    return jnp.concatenate([w[..., :C_NOPE_DIM].reshape(lead + (-1,)), w[..., C_NOPE_DIM:].reshape(lead + (-1,))],
                           axis=-1)


def kernel(x, c, ctx, c_ctx, w_mod_down, w_mod_up, norm_ffn1, ffn1_w_gu, ffn1_w_down, norm_mix, w_in,
           a_sink, b_q_norm, b_k_norm, c_q_norm, c_kv_norm, c_w_q_up, c_w_kv_up, d_rel_bias, w_out,
           norm_ffn2, ffn2_w_gu, ffn2_w_down, final_norm):
    bsz, n_lat, d = x.shape
    n_ctx = ctx.shape[1]
    n_layers = w_in.shape[0]
    q_rank, kv_rank = c_q_norm.shape[1], c_kv_norm.shape[1]
    assert bsz == 1 and n_lat % GRID_W == 0 and n_lat % n_ctx == 0 and n_ctx % LANES == 0
    rows = n_lat + n_ctx
    tr = n_ctx
    tm_all = (_pick(rows, (1408, 768, 512, 256, 128)), _pick(rows, (768, 512, 256, 128)))
    tm_lat = (_pick(n_lat, (1024, 512, 256, 128)),) * 2
    flash_tc = _pick(n_lat, (1024, 512, 256, 128))
    flash_tk = _pick(n_lat, (2048, 1024, 512, 256, 128))

    mod = _modulation(c, c_ctx, w_mod_down, w_mod_up)
    tables = _rope_tables(n_lat, n_ctx)
    no_sink = jnp.full((max(B_HEADS, C_HEADS, D_HEADS),), NEG_INF, F32)

    w_in_t, mla_rows = _transpose_w_in(w_in, q_rank, kv_rank)
    w_q_up_b, w_kv_up_b = _relayout_w_q_up(c_w_q_up), _relayout_w_kv_up(c_w_kv_up)
    w_dn1, w_dn2, w_out_b = ffn1_w_down.astype(BF16), ffn2_w_down.astype(BF16), w_out.astype(BF16)

    xs = jnp.concatenate([x[0], ctx[0]], axis=0)

    def ffn(xs, gain, w_gu, w_down, l, mod_l, k0, n_rows, tm):
        h = _norm_mod(xs, gain, mod_l, k0, k0 + 1, n_lat, tr)
        act = _gate_up(h, w_gu, l, n_rows, tm[0])
        return _residual_matmul(act, w_down, l, xs, mod_l, k0 + 2, 0.5, n_rows, n_lat, tm[1])

    for l in range(n_layers):
        need_ctx = l < n_layers - 1
        n_rows, tm = (rows, tm_all) if need_ctx else (n_lat, tm_lat)
        mod_l = mod[l]
        xs = ffn(xs, norm_ffn1[l], ffn1_w_gu, w_dn1, l, mod_l, 0, rows, tm_all)

        h = _norm_mod(xs, norm_mix[l], mod_l, 3, 4, n_lat, tr)
        main = _in_proj(h, w_in_t, l, tables, b_q_norm[l], b_k_norm[l], tm_all[0], mla_rows)
        qa, kva, kpe = _mla_down(h, w_in_t, l, tables, c_q_norm[l], c_kv_norm[l], tm_all[1], mla_rows)
        qc, kc, vc = _mla_expand(qa, kva, kpe, w_q_up_b, w_kv_up_b, l, tables[2], tables[3], tr)

        sink = a_sink[l] * LOG2E
        lat = dict(n_lat=n_lat, n_ctx=n_ctx)
        o = jnp.zeros((n_rows, MIX_COLS), BF16)
        o = _window_attention(main, sink, o, **lat)
        o = _flash(main, main, main, o, n_kv=B_KV_HEADS, group=B_HEADS // B_KV_HEADS, dk=HEAD_DIM,
                   q_blk0=B_Q0 // (B_HEADS // B_KV_HEADS), k_blk0=B_K0, v_blk0=B_V0,
                   o_blk0=MIX_B0 // (B_HEADS // B_KV_HEADS), tq=flash_tc, tc=flash_tc, tk=flash_tk, **lat)
        o = _flash(qc, kc, vc, o, n_kv=C_HEADS, group=1, dk=C_QK_PAD, q_blk0=0, k_blk0=0, v_blk0=0, o_blk0=MIX_C0,
                   tq=_pick(n_lat, (4 * flash_tc, 2 * flash_tc, flash_tc)), tc=flash_tc, tk=flash_tk, **lat)
        o = _nbr_attention(main, d_rel_bias[l], o, **lat)
        if need_ctx:
            o = _ctx_attention(main, main, main, sink, o, n_heads=A_HEADS, group=A_HEADS // A_KV_HEADS,
                               dk=HEAD_DIM, q_blk0=A_Q0, k_blk0=A_K0, v_blk0=A_V0, o_blk0=MIX_A0, **lat)
            o = _ctx_attention(main, main, main, no_sink, o, n_heads=B_HEADS, group=B_HEADS // B_KV_HEADS,
                               dk=HEAD_DIM, q_blk0=B_Q0, k_blk0=B_K0, v_blk0=B_V0, o_blk0=MIX_B0, **lat)
            o = _ctx_attention(qc, kc, vc, no_sink, o, n_heads=C_HEADS, group=1, dk=C_QK_PAD,
                               q_blk0=0, k_blk0=0, v_blk0=0, o_blk0=MIX_C0, **lat)
            o = _ctx_attention(main, main, main, no_sink, o, n_heads=D_HEADS, group=1, dk=HEAD_DIM,
                               q_blk0=D_Q0, k_blk0=D_K0, v_blk0=D_V0, o_blk0=MIX_D0, **lat)

        xs = _residual_matmul(o, w_out_b, l, xs, mod_l, 5, 1.0, n_rows, n_lat, tm[1])
        xs = ffn(xs, norm_ffn2[l], ffn2_w_gu, w_dn2, l, mod_l, 6, n_rows, tm)

    return _final_norm(xs[:n_lat], final_norm, tr)[None]
```

```python
import functools

import jax
import jax.numpy as jnp
from jax import lax
from jax.experimental import pallas as pl
from jax.experimental.pallas import tpu as pltpu

F32 = jnp.float32
BF16 = jnp.bfloat16

GRID_W = 64
HEAD_DIM = 128
ROPE_THETA = 10000.0
NORM_EPS = 1e-6
NEG_INF = -1e30
LOG2E = 1.4426950408889634
N_MOD = 9
A_HEADS, A_KV_HEADS, A_WINDOW = 8, 2, 128
B_HEADS, B_KV_HEADS = 8, 2
C_HEADS, C_NOPE_DIM, C_ROPE_DIM, C_V_DIM = 8, 128, 64, 128
D_HEADS, NA_KH, NA_KW = 8, 8, 16

LANES = 128
VMEM_LIMIT_BYTES = 56 * 1024 * 1024
C_QK_PAD = 2 * LANES

A_Q0, A_K0, A_V0 = 0, 8, 10
B_Q0, B_K0, B_V0 = 12, 20, 22
D_Q0, D_K0, D_V0 = 24, 32, 40
MAIN_COLS = 48 * LANES
MIX_A0, MIX_B0, MIX_C0, MIX_D0 = 0, 8, 16, 24
MIX_COLS = 32 * LANES
_BUF_SPEC = pl.BlockSpec(memory_space=pl.ANY)


def _params(*sem):
    return pltpu.CompilerParams(dimension_semantics=sem, vmem_limit_bytes=VMEM_LIMIT_BYTES)


def _pick(n, prefs):
    for p in prefs:
        if n % p == 0:
            return p
    raise ValueError(f"no tile in {prefs} divides {n}")


def _dot(a, b):
    return jnp.dot(a, b, preferred_element_type=F32)


def _dot_nt(a, b):
    return lax.dot_general(a, b, (((1,), (1,)), ((), ())), preferred_element_type=F32)


def _silu(x):
    return x / (1.0 + jnp.exp(-x))


def _rope(x, cos, sin_signed, half):
    n = x.shape[-1]
    lane = lax.broadcasted_iota(jnp.int32, x.shape, x.ndim - 1)
    first = jnp.bitwise_and(lane, 2 * half - 1) < half
    rot = jnp.where(first, pltpu.roll(x, n - half, x.ndim - 1), pltpu.roll(x, half, x.ndim - 1))
    return x * cos + rot * sin_signed


def _mod_down_kernel(s_ref, w_ref, o_ref, acc_ref):
    k = pl.program_id(1)

    @pl.when(k == 0)
    def _():
        acc_ref[...] = jnp.zeros_like(acc_ref)

    acc_ref[...] += _dot(_silu(s_ref[...]).astype(BF16), w_ref[...].astype(BF16))

    @pl.when(k == pl.num_programs(1) - 1)
    def _():
        o_ref[...] = acc_ref[...]


def _mod_up_kernel(t_ref, w_ref, o_ref):
    o_ref[...] = _dot(t_ref[...].astype(BF16), w_ref[...].astype(BF16))


def _modulation(c, c_ctx, w_down, w_up):
    n_layers, d, rank = w_down.shape
    n_out = w_up.shape[2]
    s = jnp.zeros((8, d), F32).at[0].set(c[0]).at[1].set(c_ctx)
    tk = _pick(d, (1024, 512, 256, 128))
    t = pl.pallas_call(
        _mod_down_kernel,
        out_shape=jax.ShapeDtypeStruct((n_layers, 8, rank), F32),
        grid=(n_layers, d // tk),
        in_specs=[pl.BlockSpec((8, tk), lambda l, k: (0, k)),
                  pl.BlockSpec((None, tk, rank), lambda l, k: (l, k, 0))],
        out_specs=pl.BlockSpec((None, 8, rank), lambda l, k: (l, 0, 0)),
        scratch_shapes=[pltpu.VMEM((8, rank), F32)],
        compiler_params=_params("parallel", "arbitrary"),
        name="mod_down",
    )(s, w_down)
    tn = _pick(n_out, (2048, 1024, 512, 256, 128))
    m = pl.pallas_call(
        _mod_up_kernel,
        out_shape=jax.ShapeDtypeStruct((n_layers, 8, n_out), F32),
        grid=(n_layers, n_out // tn),
        in_specs=[pl.BlockSpec((None, 8, rank), lambda l, j: (l, 0, 0)),
                  pl.BlockSpec((None, rank, tn), lambda l, j: (l, 0, j))],
        out_specs=pl.BlockSpec((None, 8, tn), lambda l, j: (l, 0, j)),
        compiler_params=_params("parallel", "parallel"),
        name="mod_up",
    )(t, w_up)
    return m[:, :2, :].reshape(n_layers, 2, N_MOD, 1, d)


def _norm_mod_kernel(x_ref, gain_ref, shift_ref, scale_ref, o_ref):
    x = x_ref[...]
    y = x * lax.rsqrt(jnp.mean(x * x, axis=-1, keepdims=True) + NORM_EPS) * gain_ref[...]
    o_ref[...] = (y * (1.0 + scale_ref[...]) + shift_ref[...]).astype(o_ref.dtype)


def _norm_mod(x, gain, mod_l, k_shift, k_scale, n_lat, tr):
    rows, d = x.shape
    n_lat_blocks = n_lat // tr

    def mod_spec(k):
        return pl.BlockSpec((None, None, 1, d),
                            lambda i: (jnp.where(i >= n_lat_blocks, 1, 0), k, 0, 0))

    return pl.pallas_call(
        _norm_mod_kernel,
        out_shape=jax.ShapeDtypeStruct((rows, d), BF16),
        grid=(rows // tr,),
        in_specs=[pl.BlockSpec((tr, d), lambda i: (i, 0)),
                  pl.BlockSpec((1, d), lambda i: (0, 0)),
                  mod_spec(k_shift), mod_spec(k_scale)],
        out_specs=pl.BlockSpec((tr, d), lambda i: (i, 0)),
        compiler_params=_params("parallel"),
        name="norm_mod",
    )(x, gain.reshape(1, d), mod_l, mod_l)


def _final_norm_kernel(x_ref, gain_ref, o_ref):
    x = x_ref[...]
    o_ref[...] = x * lax.rsqrt(jnp.mean(x * x, axis=-1, keepdims=True) + NORM_EPS) * gain_ref[...]


def _final_norm(x, gain, tr):
    rows, d = x.shape
    return pl.pallas_call(
        _final_norm_kernel,
        out_shape=jax.ShapeDtypeStruct((rows, d), F32),
        grid=(rows // tr,),
        in_specs=[pl.BlockSpec((tr, d), lambda i: (i, 0)),
                  pl.BlockSpec((1, d), lambda i: (0, 0))],
        out_specs=pl.BlockSpec((tr, d), lambda i: (i, 0)),
        compiler_params=_params("parallel"),
        name="final_norm",
    )(x, gain.reshape(1, d))


def _gate_up_kernel(h_ref, wg_ref, wu_ref, o_ref, *, row_chunks):
    wg, wu = wg_ref[...].astype(BF16), wu_ref[...].astype(BF16)
    chunk = h_ref.shape[0] // row_chunks
    for r in range(row_chunks):
        rows = slice(r * chunk, (r + 1) * chunk)
        h = h_ref[rows, :]
        o_ref[rows, :] = (_silu(_dot(h, wg)) * _dot(h, wu)).astype(o_ref.dtype)


def _gate_up(h, w_gu, layer, n_rows, tm):
    d = h.shape[1]
    f = w_gu.shape[2] // 2
    tn = _pick(f, (256, 128))
    nj = f // tn
    return pl.pallas_call(
        functools.partial(_gate_up_kernel, row_chunks=2 if tm % 32 == 0 else 1),
        out_shape=jax.ShapeDtypeStruct((n_rows, f), BF16),
        grid=(n_rows // tm, nj),
        in_specs=[pl.BlockSpec((tm, d), lambda i, j: (i, 0)),
                  pl.BlockSpec((None, d, tn), lambda i, j: (layer, 0, j)),
                  pl.BlockSpec((None, d, tn), lambda i, j: (layer, 0, j + nj))],
        out_specs=pl.BlockSpec((tm, tn), lambda i, j: (i, j)),
        compiler_params=_params("parallel", "arbitrary"),
        name="gate_up",
    )(h, w_gu, w_gu)


def _residual_kernel(a_ref, w_ref, x_ref, g_ref, o_ref, *, coef, n_lat, tm):
    y = _dot(a_ref[...], w_ref[...])
    row = pl.program_id(0) * tm + lax.broadcasted_iota(jnp.int32, (tm, 1), 0)
    gate = jnp.where(row < n_lat, g_ref[0], g_ref[1])
    o_ref[...] = x_ref[...] + (coef * gate) * y


def _residual_matmul(a, w, layer, x, mod_l, k_gate, coef, n_rows, n_lat, tm):
    kdim = a.shape[1]
    d = w.shape[2]
    tn = _pick(d, (512, 256, 128))
    return pl.pallas_call(
        functools.partial(_residual_kernel, coef=coef, n_lat=n_lat, tm=tm),
        out_shape=jax.ShapeDtypeStruct((n_rows, d), F32),
        grid=(n_rows // tm, d // tn),
        in_specs=[pl.BlockSpec((tm, kdim), lambda i, j: (i, 0)),
                  pl.BlockSpec((None, kdim, tn), lambda i, j: (layer, 0, j)),
                  pl.BlockSpec((tm, tn), lambda i, j: (i, j)),
                  pl.BlockSpec((2, None, 1, tn), lambda i, j: (0, k_gate, 0, j))],
        out_specs=pl.BlockSpec((tm, tn), lambda i, j: (i, j)),
        compiler_params=_params("parallel", "arbitrary"),
        name="residual_matmul",
    )(a, w, x, mod_l)


IN_TILE_BLOCKS = 4
_MAIN_OPS = (("rope_scale",) * A_HEADS + ("rope",) * A_KV_HEADS + ("copy",) * A_KV_HEADS
             + ("norm_q_rope_scale",) * B_HEADS + ("norm_k_rope",) * B_KV_HEADS + ("copy",) * B_KV_HEADS
             + ("scale",) * D_HEADS + ("copy",) * (2 * D_HEADS))


def _head_norm(x, g):
    return x * lax.rsqrt(jnp.mean(x * x, axis=-1, keepdims=True) + NORM_EPS) * g


def _in_proj_kernel(h_ref, w_ref, cos_ref, sin_ref, gq_ref, gk_ref, o_ref, *, row_chunks):
    j = pl.program_id(1)
    scale = HEAD_DIM ** -0.5 * LOG2E
    chunk = h_ref.shape[0] // row_chunks

    def apply(op, x, rows):
        if op in ("norm_q_rope_scale", "norm_k_rope"):
            x = _head_norm(x, (gq_ref if op == "norm_q_rope_scale" else gk_ref)[...])
        if "rope" in op:
            x = _rope(x, cos_ref[rows, :], sin_ref[rows, :], 32)
        return x * scale if "scale" in op else x

    n_tiles = len(_MAIN_OPS) // IN_TILE_BLOCKS
    recipes = [_MAIN_OPS[t * IN_TILE_BLOCKS:(t + 1) * IN_TILE_BLOCKS] for t in range(n_tiles)]
    for recipe in sorted(set(recipes)):
        tiles = [t for t in range(n_tiles) if recipes[t] == recipe]
        hit = functools.reduce(jnp.logical_or, [j == t for t in tiles])

        @pl.when(hit)
        def _(recipe=recipe):
            for r in range(row_chunks):
                rows = slice(r * chunk, (r + 1) * chunk)
                y = _dot_nt(h_ref[rows, :], w_ref[0])
                for b, op in enumerate(recipe):
                    cols = slice(b * LANES, (b + 1) * LANES)
                    o_ref[rows, cols] = apply(op, y[:, cols], rows).astype(o_ref.dtype)


def _in_proj(h, w_t, layer, tables, g_q, g_k, tm, mla_rows):
    rows, d = h.shape
    tn = IN_TILE_BLOCKS * LANES
    o2, o3 = mla_rows
    assert w_t.shape[1] - (o3 - o2) == MAIN_COLS and MAIN_COLS % tn == 0 and o2 % tn == 0
    row_chunks = 4 if tm % 64 == 0 else 1

    assert (o3 - o2) % C_ROPE_DIM == 0

    def w_row(j):
        return pl.multiple_of(jnp.where(j < o2 // tn, j * tn, o3 - o2 + j * tn), C_ROPE_DIM)

    return pl.pallas_call(
        functools.partial(_in_proj_kernel, row_chunks=row_chunks),
        out_shape=jax.ShapeDtypeStruct((rows, MAIN_COLS), BF16),
        grid=(rows // tm, MAIN_COLS // tn),
        in_specs=[pl.BlockSpec((tm, d), lambda i, j: (i, 0)),
                  pl.BlockSpec((pl.Element(1), pl.Element(tn), pl.Element(d)),
                               lambda i, j: (layer, w_row(j), 0)),
                  pl.BlockSpec((tm, LANES), lambda i, j: (i, 0)),
                  pl.BlockSpec((tm, LANES), lambda i, j: (i, 0)),
                  pl.BlockSpec((1, LANES), lambda i, j: (0, 0)),
                  pl.BlockSpec((1, LANES), lambda i, j: (0, 0))],
        out_specs=pl.BlockSpec((tm, tn), lambda i, j: (i, j)),
        compiler_params=_params("parallel", "arbitrary"),
        name="in_proj",
    )(h, w_t, tables[0], tables[1], g_q.reshape(1, -1), g_k.reshape(1, -1))


def _mla_down_kernel(h_ref, w_ref, cosc_ref, sinc_ref, gq_ref, gkv_ref, qa_ref, kva_ref, kpe_ref):
    q_rank, kv_rank = qa_ref.shape[1], kva_ref.shape[1]
    y = _dot_nt(h_ref[...], w_ref[0])
    qa_ref[...] = _head_norm(y[:, :q_rank], gq_ref[...]).astype(qa_ref.dtype)
    kva_ref[...] = _head_norm(y[:, q_rank:q_rank + kv_rank], gkv_ref[...]).astype(kva_ref.dtype)
    kpe = y[:, q_rank + kv_rank:q_rank + kv_rank + LANES]
    lane = lax.broadcasted_iota(jnp.int32, kpe.shape, 1)
    kpe = jnp.where(lane < C_ROPE_DIM, _rope(kpe, cosc_ref[...], sinc_ref[...], 16), 0.0)
    kpe_ref[...] = kpe.astype(kpe_ref.dtype)


def _mla_down(h, w_t, layer, tables, g_cq, g_ckv, tm, mla_rows):
    rows, d = h.shape
    q_rank, kv_rank = g_cq.shape[0], g_ckv.shape[0]
    cols = q_rank + kv_rank + LANES
    assert mla_rows[1] - mla_rows[0] == q_rank + kv_rank + C_ROPE_DIM and mla_rows[0] + cols <= w_t.shape[1]
    row_spec = lambda w: pl.BlockSpec((tm, w), lambda i: (i, 0))
    vec_spec = lambda w: pl.BlockSpec((1, w), lambda i: (0, 0))
    return pl.pallas_call(
        _mla_down_kernel,
        out_shape=(jax.ShapeDtypeStruct((rows, q_rank), BF16),
                   jax.ShapeDtypeStruct((rows, kv_rank), BF16),
                   jax.ShapeDtypeStruct((rows, LANES), BF16)),
        grid=(rows // tm,),
        in_specs=[row_spec(d),
                  pl.BlockSpec((pl.Element(1), pl.Element(cols), pl.Element(d)), lambda i: (layer, mla_rows[0], 0)),
                  row_spec(LANES), row_spec(LANES), vec_spec(q_rank), vec_spec(kv_rank)],
        out_specs=(row_spec(q_rank), row_spec(kv_rank), row_spec(LANES)),
        compiler_params=_params("parallel"),
        name="mla_down",
    )(h, w_t, tables[2], tables[3], g_cq.reshape(1, -1), g_ckv.reshape(1, -1))


def _rope_tables(n_lat, n_ctx):
    t = jnp.arange(n_lat, dtype=jnp.int32)
    row, col = (t // GRID_W).astype(F32), (t % GRID_W).astype(F32)
    lane = jnp.arange(LANES)

    def table(dim, live):
        half = dim // 2
        pair = half // 2
        inv = ROPE_THETA ** (-jnp.arange(0, half, 2, dtype=F32) / half)
        inv_lane = inv[lane % pair]
        pos = jnp.where(((lane // half) % 2 == 0)[None, :], row[:, None], col[:, None])
        ang = pos * inv_lane[None, :]
        sign = jnp.where((lane % half) < pair, -1.0, 1.0).astype(F32)
        on = (lane < live)[None, :]
        cos = jnp.where(on, jnp.cos(ang), 1.0)
        sin = jnp.where(on, jnp.sin(ang) * sign[None, :], 0.0)
        pad = lambda a, v: jnp.concatenate([a, jnp.full((n_ctx, LANES), v, F32)], axis=0)
        return pad(cos, 1.0), pad(sin, 0.0)

    cos, sin = table(HEAD_DIM, LANES)
    cosc, sinc = table(C_ROPE_DIM, C_ROPE_DIM)
    return cos, sin, cosc, sinc


def _cq_up_kernel(a_ref, w_ref, cos_ref, sin_ref, o_ref, *, scale):
    y = _dot(a_ref[...], w_ref[...])
    cos, sin = cos_ref[...], sin_ref[...]
    for h in range(C_HEADS):
        c0 = h * C_QK_PAD
        o_ref[:, c0:c0 + LANES] = (y[:, c0:c0 + LANES] * scale).astype(o_ref.dtype)
        pe = _rope(y[:, c0 + LANES:c0 + 2 * LANES], cos, sin, 16)
        o_ref[:, c0 + LANES:c0 + 2 * LANES] = (pe * scale).astype(o_ref.dtype)


def _ckv_up_kernel(a_ref, w_ref, kpe_ref, k_ref, v_ref):
    y = _dot(a_ref[...], w_ref[...])
    kpe = kpe_ref[...]
    for h in range(C_HEADS):
        k_ref[:, h * C_QK_PAD:h * C_QK_PAD + LANES] = y[:, h * LANES:(h + 1) * LANES].astype(k_ref.dtype)
        k_ref[:, h * C_QK_PAD + LANES:(h + 1) * C_QK_PAD] = kpe
    v_ref[...] = y[:, C_HEADS * LANES:].astype(v_ref.dtype)


def _mla_expand(qa, kva, kpe, w_q, w_kv, layer, cosc, sinc, tm):
    rows, q_rank = qa.shape
    kv_rank = kva.shape[1]
    scale = (C_NOPE_DIM + C_ROPE_DIM) ** -0.5 * LOG2E
    row_spec = lambda w: pl.BlockSpec((tm, w), lambda i: (i, 0))
    layer_spec = lambda w: pl.BlockSpec((None,) + w.shape[1:], lambda i: (layer, 0, 0))
    qc = pl.pallas_call(
        functools.partial(_cq_up_kernel, scale=scale),
        out_shape=jax.ShapeDtypeStruct((rows, C_HEADS * C_QK_PAD), BF16),
        grid=(rows // tm,),
        in_specs=[row_spec(q_rank), layer_spec(w_q), row_spec(LANES), row_spec(LANES)],
        out_specs=row_spec(C_HEADS * C_QK_PAD),
        compiler_params=_params("parallel"),
        name="mla_q_up",
    )(qa, w_q, cosc, sinc)
    kc, vc = pl.pallas_call(
        _ckv_up_kernel,
        out_shape=(jax.ShapeDtypeStruct((rows, C_HEADS * C_QK_PAD), BF16),
                   jax.ShapeDtypeStruct((rows, C_HEADS * C_V_DIM), BF16)),
        grid=(rows // tm,),
        in_specs=[row_spec(kv_rank), layer_spec(w_kv), row_spec(LANES)],
        out_specs=(row_spec(C_HEADS * C_QK_PAD), row_spec(C_HEADS * C_V_DIM)),
        compiler_params=_params("parallel"),
        name="mla_kv_up",
    )(kva, w_kv, kpe)
    return qc, kc, vc


def _flash_kernel(q_ref, k_ref, v_ref, buf_ref, o_ref, m_sc, l_sc, acc_sc, *, group, dk, tc, tk, n_lat, n_ctx):
    del buf_ref
    tq = q_ref.shape[0]
    chains = [(g, r) for g in range(group) for r in range(tq // tc)]

    def chunk(k, v, first):
        n_blocks = k.shape[0] // LANES
        for ci, (g, r) in enumerate(chains):
            s = _dot_nt(q_ref[r * tc:(r + 1) * tc, g * dk:(g + 1) * dk], k)
            blocks = [s[:, b * LANES:(b + 1) * LANES] for b in range(n_blocks)]
            mx = blocks[0]
            for blk in blocks[1:]:
                mx = jnp.maximum(mx, blk)
            m_new = jnp.broadcast_to(jnp.max(mx, axis=-1, keepdims=True), (tc, LANES))
            if not first:
                m_prev = m_sc[ci]
                m_new = jnp.maximum(m_prev, m_new)
                alpha = jnp.exp2(m_prev - m_new)
            ps = [jnp.exp2(blk - m_new) for blk in blocks]
            l_new = ps[0]
            for p in ps[1:]:
                l_new = l_new + p
            pv = _dot(jnp.concatenate([p.astype(BF16) for p in ps], axis=1), v)
            if first:
                l_sc[ci] = l_new
                acc_sc[ci] = pv
            else:
                l_sc[ci] = alpha * l_sc[ci] + l_new
                acc_sc[ci] = alpha * acc_sc[ci] + pv
            m_sc[ci] = m_new

    chunk(k_ref[n_lat:n_lat + n_ctx, :], v_ref[n_lat:n_lat + n_ctx, :], True)

    def step(c, carry):
        start = pl.multiple_of(c * tk, tk)
        chunk(k_ref[pl.ds(start, tk), :], v_ref[pl.ds(start, tk), :], False)
        return carry

    lax.fori_loop(0, n_lat // tk, step, 0)
    for ci, (g, r) in enumerate(chains):
        inv = 1.0 / jnp.sum(l_sc[ci], axis=-1, keepdims=True)
        o_ref[r * tc:(r + 1) * tc, g * LANES:(g + 1) * LANES] = (acc_sc[ci] * inv).astype(o_ref.dtype)


def _flash(q_arr, k_arr, v_arr, o_buf, *, n_kv, group, dk, q_blk0, k_blk0, v_blk0, o_blk0, tq, tc, tk, n_lat,
           n_ctx):
    rows = k_arr.shape[0]
    n_chains = group * (tq // tc)
    stat = pltpu.VMEM((n_chains, tc, LANES), F32)
    return pl.pallas_call(
        functools.partial(_flash_kernel, group=group, dk=dk, tc=tc, tk=tk, n_lat=n_lat, n_ctx=n_ctx),
        out_shape=jax.ShapeDtypeStruct(o_buf.shape, o_buf.dtype),
        grid=(n_kv, n_lat // tq),
        in_specs=[pl.BlockSpec((tq, group * dk), lambda g, i: (i, q_blk0 + g)),
                  pl.BlockSpec((rows, dk), lambda g, i: (0, k_blk0 + g)),
                  pl.BlockSpec((rows, LANES), lambda g, i: (0, v_blk0 + g)),
                  _BUF_SPEC],
        out_specs=pl.BlockSpec((tq, group * LANES), lambda g, i: (i, o_blk0 + g)),
        scratch_shapes=[stat, stat, stat],
        input_output_aliases={3: 0},
        compiler_params=_params("parallel", "arbitrary"),
        name="flash",
    )(q_arr, k_arr, v_arr, o_buf)


def _lane_blocks(s):
    return [s[:, b * LANES:(b + 1) * LANES] for b in range(s.shape[1] // LANES)]


def _softmax_pv(segments, sink=None):
    blocks = [blk for blks, _ in segments for blk in blks]
    rows = blocks[0].shape[0]
    mx = blocks[0]
    for blk in blocks[1:]:
        mx = jnp.maximum(mx, blk)
    m = jnp.broadcast_to(jnp.max(mx, axis=-1, keepdims=True), (rows, LANES))
    if sink is not None:
        m = jnp.maximum(m, sink)
    out = l = None
    for blks, v in segments:
        ps = [jnp.exp2(blk - m) for blk in blks]
        for p in ps:
            l = p if l is None else l + p
        pv = _dot(jnp.concatenate([p.astype(BF16) for p in ps], axis=1), v)
        out = pv if out is None else out + pv
    if sink is not None:
        lane = lax.broadcasted_iota(jnp.int32, (rows, LANES), 1)
        l = l + jnp.where(lane == 0, jnp.exp2(sink - m), 0.0)
    return out * (1.0 / jnp.sum(l, axis=-1, keepdims=True))


def _window_kernel(sink_ref, q_ref, k_ref, v_ref, mask_ref, buf_ref, o_ref, *, group, tq, n_lat, n_ctx):
    del buf_ref
    g, n = pl.program_id(0), pl.program_id(1)
    span = tq + 2 * LANES
    kc, vc = k_ref[n_lat:n_lat + n_ctx, :], v_ref[n_lat:n_lat + n_ctx, :]
    sink = jnp.concatenate([jnp.full((tq, LANES), sink_ref[g * group + i], F32) for i in range(group)], axis=0)
    n_sub = q_ref.shape[0] // tq
    for sub in range(n_sub):
        rows = slice(sub * tq, (sub + 1) * tq)
        q0 = (n * n_sub + sub) * tq
        start = pl.multiple_of(jnp.clip(q0 - LANES, 0, n_lat - span), LANES)
        kw, vw = k_ref[pl.ds(start, span), :], v_ref[pl.ds(start, span), :]
        q = jnp.concatenate([q_ref[rows, i * HEAD_DIM:(i + 1) * HEAD_DIM] for i in range(group)], axis=0)
        mask = mask_ref[(q0 - start) // LANES]
        s_w = [blk + jnp.concatenate([mask[:, b * LANES:(b + 1) * LANES]] * group, axis=0)
               for b, blk in enumerate(_lane_blocks(_dot_nt(q, kw)))]
        out = _softmax_pv([(s_w, vw), (_lane_blocks(_dot_nt(q, kc)), vc)], sink=sink)
        for i in range(group):
            o_ref[rows, i * HEAD_DIM:(i + 1) * HEAD_DIM] = out[i * tq:(i + 1) * tq].astype(o_ref.dtype)


def _window_mask(tq, window):
    off = (jnp.arange(3) * LANES)[:, None, None]
    row = jnp.arange(tq)[None, :, None]
    col = jnp.arange(tq + 2 * LANES)[None, None, :]
    return jnp.where(jnp.abs(col - off - row) <= window, 0.0, NEG_INF).astype(F32)


def _window_attention(main, sink, o_buf, n_lat, n_ctx):
    rows = main.shape[0]
    group = A_HEADS // A_KV_HEADS
    tq = _pick(n_lat, (256, 128))
    span = tq + 2 * LANES
    assert A_WINDOW <= LANES and n_lat >= span
    n_sub = _pick(n_lat // tq, (2, 1))
    return pl.pallas_call(
        functools.partial(_window_kernel, group=group, tq=tq, n_lat=n_lat, n_ctx=n_ctx),
        out_shape=jax.ShapeDtypeStruct(o_buf.shape, o_buf.dtype),
        grid=(A_KV_HEADS, n_lat // (n_sub * tq)),
        in_specs=[pl.BlockSpec(memory_space=pltpu.SMEM),
                  pl.BlockSpec((n_sub * tq, group * HEAD_DIM), lambda g, n: (n, g)),
                  pl.BlockSpec((rows, HEAD_DIM), lambda g, n: (0, A_K0 + g)),
                  pl.BlockSpec((rows, HEAD_DIM), lambda g, n: (0, A_V0 + g)),
                  pl.BlockSpec((3, tq, span), lambda g, n: (0, 0, 0)),
                  _BUF_SPEC],
        out_specs=pl.BlockSpec((n_sub * tq, group * HEAD_DIM), lambda g, n: (n, MIX_A0 // group + g)),
        input_output_aliases={5: 0},
        compiler_params=_params("parallel", "arbitrary"),
        name="window_attention",
    )(sink, main, main, main, _window_mask(tq, A_WINDOW), o_buf)


NBR_Q_ROWS = NA_KH // 2
NBR_KEY_ROWS = 12
NBR_CHAINS = 2


def _nbr_key_start(r0, n_grid_rows):
    return jnp.clip(r0 - NA_KH // 2, 0, n_grid_rows - NBR_KEY_ROWS)


def _nbr_kernel(q_ref, k_ref, v_ref, *rest, n_grid_rows, n_lat, n_ctx):
    b_refs, o_ref = rest[:NBR_CHAINS], rest[NBR_CHAINS + 1]
    chain_rows = NBR_Q_ROWS * GRID_W
    n_keys = NBR_KEY_ROWS * GRID_W
    kc, vc = k_ref[n_lat:n_lat + n_ctx, :], v_ref[n_lat:n_lat + n_ctx, :]
    for c in range(NBR_CHAINS):
        r0 = (pl.program_id(1) * NBR_CHAINS + c) * NBR_Q_ROWS
        k0 = pl.multiple_of(_nbr_key_start(r0, n_grid_rows) * GRID_W, chain_rows)
        kw, vw = k_ref[pl.ds(k0, n_keys), :], v_ref[pl.ds(k0, n_keys), :]
        rows = slice(c * chain_rows, (c + 1) * chain_rows)
        q = q_ref[rows, :]
        s_nb = [blk + jnp.concatenate([b_refs[c][jj, kp] for jj in range(NBR_Q_ROWS)], axis=0)
                for kp, blk in enumerate(_lane_blocks(_dot_nt(q, kw)))]
        out = _softmax_pv([(s_nb, vw), (_lane_blocks(_dot_nt(q, kc)), vc)])
        o_ref[rows, :] = out.astype(o_ref.dtype)


def _nbr_bias_table(rel_bias):
    assert 2 * GRID_W == LANES
    n_heads = rel_bias.shape[0]
    hp = lax.Precision.HIGHEST
    off = jnp.array([0, NA_KH // 2, NA_KH], jnp.int32)[:, None, None]
    j = jnp.arange(NBR_Q_ROWS)[None, :, None]
    kr = jnp.arange(NBR_KEY_ROWS)[None, None, :]
    centred = j - NA_KH // 2
    rs = jnp.stack([jnp.maximum(centred[0], 0), centred[0], jnp.minimum(centred[0], -(NA_KH // 2))])
    key_row = kr - off
    row_ok = (key_row >= rs) & (key_row < rs + NA_KH)
    row_sel = jax.nn.one_hot(key_row - j + (NA_KH - 1), 2 * NA_KH - 1, dtype=F32) * row_ok[..., None]
    feat = jnp.einsum("vjka,hab->vhjkb", row_sel, rel_bias.astype(F32) * LOG2E, precision=hp)
    bad = jnp.broadcast_to((~row_ok).astype(F32)[:, None, :, :, None], feat.shape[:-1] + (1,))
    feat = jnp.concatenate([feat, bad], axis=-1)
    feat = feat.reshape(feat.shape[:3] + (NBR_KEY_ROWS // 2, 2 * feat.shape[-1]))
    feat = jnp.concatenate([feat, jnp.ones(feat.shape[:-1] + (1,), F32)], axis=-1)
    c = jnp.arange(GRID_W)[:, None]
    kc = jnp.arange(GRID_W)[None, :]
    cstart = jnp.clip(c - NA_KW // 2, 0, GRID_W - NA_KW)
    col_ok = (kc >= cstart) & (kc < cstart + NA_KW)
    col_sel = jax.nn.one_hot(kc - c + (NA_KW - 1), 2 * NA_KW - 1, dtype=F32)
    per_row = jnp.concatenate([col_sel.transpose(0, 2, 1), jnp.full((GRID_W, 1, GRID_W), NEG_INF, F32)], axis=1)
    sel = jnp.einsum("pq,cxk->cpxqk", jnp.eye(2, dtype=F32), per_row).reshape(GRID_W, 4 * NA_KW, LANES)
    col_bad = jnp.tile(jnp.where(col_ok, 0.0, NEG_INF).astype(F32), (1, 2))[:, None, :]
    sel = jnp.concatenate([sel, col_bad], axis=1)
    return jnp.einsum("vhjkx,cxl->vhjkcl", feat, sel, precision=hp)


def _nbr_attention(main, rel_bias, o_buf, n_lat, n_ctx):
    rows = main.shape[0]
    n_grid_rows = n_lat // GRID_W
    step_rows = NBR_CHAINS * NBR_Q_ROWS
    assert n_grid_rows % step_rows == 0 and n_grid_rows >= NBR_KEY_ROWS
    assert NBR_KEY_ROWS >= NBR_Q_ROWS + NA_KH - 1 and NBR_KEY_ROWS % 2 == 0 and NA_KH % 2 == 0
    tq = step_rows * GRID_W
    table = _nbr_bias_table(rel_bias)

    def table_spec(c):
        def variant(rb):
            r0 = (rb * NBR_CHAINS + c) * NBR_Q_ROWS
            return (r0 - _nbr_key_start(r0, n_grid_rows)) // (NA_KH // 2)
        return pl.BlockSpec((None, None, NBR_Q_ROWS, NBR_KEY_ROWS // 2, GRID_W, LANES),
                            lambda h, rb: (variant(rb), h, 0, 0, 0, 0))

    return pl.pallas_call(
        functools.partial(_nbr_kernel, n_grid_rows=n_grid_rows, n_lat=n_lat, n_ctx=n_ctx),
        out_shape=jax.ShapeDtypeStruct(o_buf.shape, o_buf.dtype),
        grid=(D_HEADS, n_grid_rows // step_rows),
        in_specs=[pl.BlockSpec((tq, HEAD_DIM), lambda h, rb: (rb, D_Q0 + h)),
                  pl.BlockSpec((rows, HEAD_DIM), lambda h, rb: (0, D_K0 + h)),
                  pl.BlockSpec((rows, HEAD_DIM), lambda h, rb: (0, D_V0 + h)),
                  *[table_spec(c) for c in range(NBR_CHAINS)],
                  _BUF_SPEC],
        out_specs=pl.BlockSpec((tq, HEAD_DIM), lambda h, rb: (rb, MIX_D0 + h)),
        input_output_aliases={3 + NBR_CHAINS: 0},
        compiler_params=_params("parallel", "arbitrary"),
        name="nbr_attention",
    )(main, main, main, *([table] * NBR_CHAINS), o_buf)


def _ctx_attn_kernel(sink_ref, q_ref, k_ref, v_ref, buf_ref, o_ref):
    del buf_ref
    sink = sink_ref[pl.program_id(0)]
    s = _dot_nt(q_ref[...], k_ref[...])
    m = jnp.maximum(jnp.max(s, axis=-1, keepdims=True), sink)
    p = jnp.exp2(s - m)
    denom = jnp.sum(p, axis=-1, keepdims=True) + jnp.exp2(sink - m)
    o_ref[...] = (_dot(p.astype(BF16), v_ref[...]) * (1.0 / denom)).astype(o_ref.dtype)


def _ctx_attention(q_arr, k_arr, v_arr, sink, o_buf, *, n_heads, group, dk, q_blk0, k_blk0, v_blk0, o_blk0, n_lat,
                   n_ctx):
    rb = n_lat // n_ctx
    return pl.pallas_call(
        _ctx_attn_kernel,
        out_shape=jax.ShapeDtypeStruct(o_buf.shape, o_buf.dtype),
        grid=(n_heads,),
        in_specs=[pl.BlockSpec(memory_space=pltpu.SMEM),
                  pl.BlockSpec((n_ctx, dk), lambda h: (rb, q_blk0 + h)),
                  pl.BlockSpec((n_ctx, dk), lambda h: (rb, k_blk0 + h // group)),
                  pl.BlockSpec((n_ctx, LANES), lambda h: (rb, v_blk0 + h // group)),
                  _BUF_SPEC],
        out_specs=pl.BlockSpec((n_ctx, LANES), lambda h: (rb, o_blk0 + h)),
        input_output_aliases={4: 0},
        compiler_params=_params("parallel"),
        name="ctx_attention",
    )(sink, q_arr, k_arr, v_arr, o_buf)


def _transpose_w_in(w, q_rank, kv_rank):
    a_cols = (A_HEADS + 2 * A_KV_HEADS) * HEAD_DIM
    b_cols = (B_HEADS + 2 * B_KV_HEADS) * HEAD_DIM
    c_cols = q_rank + kv_rank + C_ROPE_DIM
    return jnp.swapaxes(w, -1, -2).astype(BF16), (a_cols + b_cols, a_cols + b_cols + c_cols)


def _relayout_w_q_up(w):
    lead = w.shape[:-1]
    w = w.astype(BF16).reshape(lead + (C_HEADS, C_NOPE_DIM + C_ROPE_DIM))
    w = jnp.pad(w, ((0, 0),) * (len(lead) + 1) + ((0, C_QK_PAD - C_NOPE_DIM - C_ROPE_DIM),))
    return w.reshape(lead + (C_HEADS * C_QK_PAD,))


def _relayout_w_kv_up(w):
    lead = w.shape[:-1]
    w = w.astype(BF16).reshape(lead + (C_HEADS, C_NOPE_DIM + C_V_DIM))
    return jnp.concatenate([w[..., :C_NOPE_DIM].reshape(lead + (-1,)), w[..., C_NOPE_DIM:].reshape(lead + (-1,))],
                           axis=-1)


def kernel(x, c, ctx, c_ctx, w_mod_down, w_mod_up, norm_ffn1, ffn1_w_gu, ffn1_w_down, norm_mix, w_in,
           a_sink, b_q_norm, b_k_norm, c_q_norm, c_kv_norm, c_w_q_up, c_w_kv_up, d_rel_bias, w_out,
           norm_ffn2, ffn2_w_gu, ffn2_w_down, final_norm):
    bsz, n_lat, d = x.shape
    n_ctx = ctx.shape[1]
    n_layers = w_in.shape[0]
    q_rank, kv_rank = c_q_norm.shape[1], c_kv_norm.shape[1]
    assert bsz == 1 and n_lat % GRID_W == 0 and n_lat % n_ctx == 0 and n_ctx % LANES == 0
    rows = n_lat + n_ctx
    tr = n_ctx
    tm_all = (_pick(rows, (1408, 768, 512, 256, 128)), _pick(rows, (768, 512, 256, 128)))
    tm_lat = (_pick(n_lat, (1024, 512, 256, 128)),) * 2
    flash_tc = _pick(n_lat, (1024, 512, 256, 128))
    flash_tk = _pick(n_lat, (2048, 1024, 512, 256, 128))

    mod = _modulation(c, c_ctx, w_mod_down, w_mod_up)
    tables = _rope_tables(n_lat, n_ctx)
    no_sink = jnp.full((max(B_HEADS, C_HEADS, D_HEADS),), NEG_INF, F32)

    w_in_t, mla_rows = _transpose_w_in(w_in, q_rank, kv_rank)
    w_q_up_b, w_kv_up_b = _relayout_w_q_up(c_w_q_up), _relayout_w_kv_up(c_w_kv_up)
    w_dn1, w_dn2, w_out_b = ffn1_w_down.astype(BF16), ffn2_w_down.astype(BF16), w_out.astype(BF16)

    xs = jnp.concatenate([x[0], ctx[0]], axis=0)

    def ffn(xs, gain, w_gu, w_down, l, mod_l, k0, n_rows, tm):
        h = _norm_mod(xs, gain, mod_l, k0, k0 + 1, n_lat, tr)
        act = _gate_up(h, w_gu, l, n_rows, tm[0])
        return _residual_matmul(act, w_down, l, xs, mod_l, k0 + 2, 0.5, n_rows, n_lat, tm[1])

    for l in range(n_layers):
        need_ctx = l < n_layers - 1
        n_rows, tm = (rows, tm_all) if need_ctx else (n_lat, tm_lat)
        mod_l = mod[l]
        xs = ffn(xs, norm_ffn1[l], ffn1_w_gu, w_dn1, l, mod_l, 0, rows, tm_all)

        h = _norm_mod(xs, norm_mix[l], mod_l, 3, 4, n_lat, tr)
        main = _in_proj(h, w_in_t, l, tables, b_q_norm[l], b_k_norm[l], tm_all[0], mla_rows)
        qa, kva, kpe = _mla_down(h, w_in_t, l, tables, c_q_norm[l], c_kv_norm[l], tm_all[1], mla_rows)
        qc, kc, vc = _mla_expand(qa, kva, kpe, w_q_up_b, w_kv_up_b, l, tables[2], tables[3], tr)

        sink = a_sink[l] * LOG2E
        lat = dict(n_lat=n_lat, n_ctx=n_ctx)
        o = jnp.zeros((n_rows, MIX_COLS), BF16)
        o = _window_attention(main, sink, o, **lat)
        o = _flash(main, main, main, o, n_kv=B_KV_HEADS, group=B_HEADS // B_KV_HEADS, dk=HEAD_DIM,
                   q_blk0=B_Q0 // (B_HEADS // B_KV_HEADS), k_blk0=B_K0, v_blk0=B_V0,
                   o_blk0=MIX_B0 // (B_HEADS // B_KV_HEADS), tq=flash_tc, tc=flash_tc, tk=flash_tk, **lat)
        o = _flash(qc, kc, vc, o, n_kv=C_HEADS, group=1, dk=C_QK_PAD, q_blk0=0, k_blk0=0, v_blk0=0, o_blk0=MIX_C0,
                   tq=_pick(n_lat, (4 * flash_tc, 2 * flash_tc, flash_tc)), tc=flash_tc, tk=flash_tk, **lat)
        o = _nbr_attention(main, d_rel_bias[l], o, **lat)
        if need_ctx:
            o = _ctx_attention(main, main, main, sink, o, n_heads=A_HEADS, group=A_HEADS // A_KV_HEADS,
                               dk=HEAD_DIM, q_blk0=A_Q0, k_blk0=A_K0, v_blk0=A_V0, o_blk0=MIX_A0, **lat)
            o = _ctx_attention(main, main, main, no_sink, o, n_heads=B_HEADS, group=B_HEADS // B_KV_HEADS,
                               dk=HEAD_DIM, q_blk0=B_Q0, k_blk0=B_K0, v_blk0=B_V0, o_blk0=MIX_B0, **lat)
            o = _ctx_attention(qc, kc, vc, no_sink, o, n_heads=C_HEADS, group=1, dk=C_QK_PAD,
                               q_blk0=0, k_blk0=0, v_blk0=0, o_blk0=MIX_C0, **lat)
            o = _ctx_attention(main, main, main, no_sink, o, n_heads=D_HEADS, group=1, dk=HEAD_DIM,
                               q_blk0=D_Q0, k_blk0=D_K0, v_blk0=D_V0, o_blk0=MIX_D0, **lat)

        xs = _residual_matmul(o, w_out_b, l, xs, mod_l, 5, 1.0, n_rows, n_lat, tm[1])
        xs = ffn(xs, norm_ffn2[l], ffn2_w_gu, w_dn2, l, mod_l, 6, n_rows, tm)

    return _final_norm(xs[:n_lat], final_norm, tr)[None]
```
